```python
import jax
import jax.numpy as jnp
from jax import lax
import numpy as np


D_MODEL = 1024
BATCH = 1
SEQ = 16384
DEPTH = 1

ATTN_HEADS = 8
ATTN_HEAD_DIM = 64
ATTN_WIDTH = ATTN_HEADS * ATTN_HEAD_DIM
MOBA_BLOCK = 256
MOBA_TOP_K = 3
Q_BLOCK = 128
MLSTM_HEADS = 4
MLSTM_HEAD_DIM = 128
MLSTM_WIDTH = MLSTM_HEADS * MLSTM_HEAD_DIM
MLSTM_CHUNK = 128
CONV_WIDTH = 4
FFN_HIDDEN = -(-8 * D_MODEL // (3 * 256)) * 256
NORM_EPS = 1e-6
IN_SPLITS = (ATTN_WIDTH, ATTN_WIDTH, ATTN_WIDTH,
             MLSTM_WIDTH, MLSTM_WIDTH, MLSTM_WIDTH, MLSTM_WIDTH,
             MLSTM_HEADS, MLSTM_HEADS, D_MODEL, D_MODEL)
IN_WIDTH = sum(IN_SPLITS)

kernel_name = 'hybrid_moba_mlstm_swiglu'


def rms_norm(x, g):
    xf = x.astype(jnp.float32)
    y = xf * lax.rsqrt(jnp.mean(xf * xf, axis=-1, keepdims=True) + NORM_EPS)
    return (y * g.astype(jnp.float32)).astype(x.dtype)


def to_heads(t, n_heads):
    b, s, w = t.shape
    return t.reshape(b, s, n_heads, w // n_heads).transpose(0, 2, 1, 3)


def from_heads(t):
    b, h, s, d = t.shape
    return t.transpose(0, 2, 1, 3).reshape(b, s, h * d)


def causal_depthwise_conv(x, w):
    width = w.shape[0]
    s = x.shape[1]
    xp = jnp.pad(x, ((0, 0), (width - 1, 0), (0, 0)))
    y = xp[:, 0:s] * w[0]
    for j in range(1, width):
        y = y + xp[:, j:j + s] * w[j]
    return y


def moba_attention(q, k, v, slopes):
    b, h, s, d = q.shape
    n_blk = -(-s // MOBA_BLOCK)
    s_pad = n_blk * MOBA_BLOCK
    pad = ((0, 0), (0, 0), (0, s_pad - s), (0, 0))
    k_pad = jnp.pad(k, pad)
    v_pad = jnp.pad(v, pad)
    k_blk = k_pad.reshape(b, h, n_blk, MOBA_BLOCK, d)
    v_blk = v_pad.reshape(b, h, n_blk, MOBA_BLOCK, d)
    k_mean = jnp.mean(k_blk.astype(jnp.float32), axis=3)
    top_k = min(MOBA_TOP_K, n_blk)
    n_sel = top_k * MOBA_BLOCK
    n_q = s // Q_BLOCK
    q_chunks = jnp.moveaxis(q.reshape(b, h, n_q, Q_BLOCK, d), 2, 0)
    b_ix = jnp.arange(b)[:, None, None, None]
    h_ix = jnp.arange(h)[None, :, None, None]
    blk_ids = jnp.arange(n_blk)
    key_off = jnp.arange(MOBA_BLOCK)
    scale = d ** -0.5
    slope = slopes[None, :, None, None]

    def one_block(args):
        q_i, q_blk = args
        t = q_blk * Q_BLOCK + jnp.arange(Q_BLOCK)
        own = (q_blk * Q_BLOCK) // MOBA_BLOCK
        qf = q_i.astype(jnp.float32)
        gate = jnp.einsum('bhqd,bhnd->bhqn', qf, k_mean)
        gate = jnp.where(blk_ids < own, gate, -jnp.inf)
        _, idx = lax.top_k(gate, top_k)
        k_sel = k_blk[b_ix, h_ix, idx].reshape(b, h, Q_BLOCK, n_sel, d)
        v_sel = v_blk[b_ix, h_ix, idx].reshape(b, h, Q_BLOCK, n_sel, d)
        pos_sel = (idx[..., None] * MOBA_BLOCK + key_off).reshape(b, h, Q_BLOCK, n_sel)
        valid_sel = jnp.repeat(jnp.arange(top_k) < own, MOBA_BLOCK)
        s_sel = jnp.einsum('bhqd,bhqkd->bhqk', qf, k_sel.astype(jnp.float32)) * scale
        s_sel = s_sel - slope * (t[:, None] - pos_sel)
        s_sel = jnp.where(valid_sel, s_sel, -jnp.inf)
        k_own = lax.dynamic_slice_in_dim(k_pad, own * MOBA_BLOCK, MOBA_BLOCK, axis=2)
        v_own = lax.dynamic_slice_in_dim(v_pad, own * MOBA_BLOCK, MOBA_BLOCK, axis=2)
        pos_own = own * MOBA_BLOCK + key_off
        s_own = jnp.einsum('bhqd,bhkd->bhqk', qf, k_own.astype(jnp.float32)) * scale
        s_own = s_own - slope * (t[:, None] - pos_own[None, :])
        s_own = jnp.where(pos_own[None, :] <= t[:, None], s_own, -jnp.inf)
        p = jax.nn.softmax(jnp.concatenate([s_sel, s_own], axis=-1), axis=-1)
        o = (jnp.einsum('bhqk,bhqkd->bhqd', p[..., :n_sel], v_sel.astype(jnp.float32))
             + jnp.einsum('bhqk,bhkd->bhqd', p[..., n_sel:], v_own.astype(jnp.float32)))
        return o.astype(q.dtype)

    out = lax.map(one_block, (q_chunks, jnp.arange(n_q)))
    return jnp.moveaxis(out, 0, 2).reshape(b, h, s, d)


def mlstm_chunkwise(q, k, v, log_i, log_f):
    b, h, s, d = q.shape
    L = MLSTM_CHUNK
    nc = s // L

    def chunks(t):
        return jnp.moveaxis(t.reshape((b, h, nc, L) + t.shape[3:]), 2, 0)

    qf = q.astype(jnp.float32)
    kf = k.astype(jnp.float32) * (d ** -0.5)
    vf = v.astype(jnp.float32)
    bcum = jnp.moveaxis(jnp.cumsum(log_f.reshape(b, h, nc, L), axis=-1), 2, 0)
    causal = jnp.tril(jnp.ones((L, L), dtype=bool))

    def step(carry, xs):
        C, n, m = carry
        qc, kc, vc, ic, bc = xs
        dmat = bc[..., :, None] - bc[..., None, :] + ic[..., None, :]
        dmat = jnp.where(causal, dmat, -jnp.inf)
        inter = bc + m[..., None]
        m_t = jnp.maximum(inter, jnp.max(dmat, axis=-1))
        w_intra = jnp.exp(dmat - m_t[..., None])
        w_inter = jnp.exp(inter - m_t)
        sc = jnp.einsum('bhtd,bhsd->bhts', qc, kc) * w_intra
        num = (w_inter[..., None] * jnp.einsum('bhtd,bhde->bhte', qc, C)
               + jnp.einsum('bhts,bhse->bhte', sc, vc))
        den = w_inter * jnp.einsum('bhtd,bhd->bht', qc, n) + jnp.sum(sc, axis=-1)
        h_out = num / jnp.maximum(jnp.abs(den), jnp.exp(-m_t))[..., None]
        b_last = bc[..., -1]
        g = b_last[..., None] - bc + ic
        m_new = jnp.maximum(b_last + m, jnp.max(g, axis=-1))
        w_s = jnp.exp(g - m_new[..., None])
        decay = jnp.exp(b_last + m - m_new)
        C = decay[..., None, None] * C + jnp.einsum('bhs,bhsd,bhse->bhde', w_s, kc, vc)
        n = decay[..., None] * n + jnp.einsum('bhs,bhsd->bhd', w_s, kc)
        return (C, n, m_new), h_out

    init = (jnp.zeros((b, h, d, d), jnp.float32),
            jnp.zeros((b, h, d), jnp.float32),
            jnp.zeros((b, h), jnp.float32))
    _, hs = lax.scan(step, init, (chunks(qf), chunks(kf), chunks(vf), chunks(log_i), bcum))
    return jnp.moveaxis(hs, 0, 2).reshape(b, h, s, d)


def head_layer_norm(h, g):
    mu = jnp.mean(h, axis=-1, keepdims=True)
    var = jnp.mean(jnp.square(h - mu), axis=-1, keepdims=True)
    nh, dh = h.shape[1], h.shape[3]
    return (h - mu) * lax.rsqrt(var + NORM_EPS) * g.astype(jnp.float32).reshape(nh, dh)[None, :, None, :]


def setup_inputs(seed: int = 0) -> dict:
    key = jax.random.key(seed)
    ks = jax.random.split(key, 16)
    f32 = jnp.float32
    nrm = lambda k, shp: jax.random.normal(k, shp, f32)
    return {
        'x': nrm(ks[0], (BATCH, SEQ, D_MODEL)),
        'mix_norm_g': 1.0 + 0.02 * nrm(ks[1], (DEPTH, D_MODEL)),
        'w_in': nrm(ks[2], (DEPTH, D_MODEL, IN_WIDTH)) * D_MODEL ** -0.5,
        'conv_qk': nrm(ks[3], (DEPTH, CONV_WIDTH, 2 * MLSTM_WIDTH)) * CONV_WIDTH ** -0.5,
        'b_igate': 0.1 * nrm(ks[4], (DEPTH, MLSTM_HEADS)),
        'b_fgate': jnp.linspace(3.0, 6.0, MLSTM_HEADS, dtype=f32)[None, :] + 0.1 * nrm(ks[5], (DEPTH, MLSTM_HEADS)),
        'mlstm_norm_g': 1.0 + 0.02 * nrm(ks[6], (DEPTH, MLSTM_WIDTH)),
        'w_out_attn': nrm(ks[7], (DEPTH, ATTN_WIDTH, D_MODEL)) * ATTN_WIDTH ** -0.5,
        'w_out_mlstm': nrm(ks[8], (DEPTH, MLSTM_WIDTH, D_MODEL)) * MLSTM_WIDTH ** -0.5,
        'w_o': nrm(ks[9], (DEPTH, D_MODEL, D_MODEL)) * D_MODEL ** -0.5,
        'ffn_norm_g': 1.0 + 0.02 * nrm(ks[10], (DEPTH, D_MODEL)),
        'w_ffn_gate': nrm(ks[11], (DEPTH, D_MODEL, FFN_HIDDEN)) * D_MODEL ** -0.5,
        'w_ffn_up': nrm(ks[12], (DEPTH, D_MODEL, FFN_HIDDEN)) * D_MODEL ** -0.5,
        'w_ffn_down': nrm(ks[13], (DEPTH, FFN_HIDDEN, D_MODEL)) * FFN_HIDDEN ** -0.5,
        'final_norm_g': 1.0 + 0.02 * nrm(ks[14], (D_MODEL,)),
    }


def reference(x, mix_norm_g, w_in, conv_qk, b_igate, b_fgate, mlstm_norm_g,
              w_out_attn, w_out_mlstm, w_o, ffn_norm_g, w_ffn_gate, w_ffn_up,
              w_ffn_down, final_norm_g):
    f32 = jnp.float32
    slopes = jnp.exp2(-8.0 * jnp.arange(1, ATTN_HEADS + 1, dtype=f32) / ATTN_HEADS)
    split_points = np.cumsum(IN_SPLITS)[:-1].tolist()
    for layer in range(DEPTH):
        h = rms_norm(x, mix_norm_g[layer])
        proj = h @ w_in[layer]
        (q_a, k_a, v_a, q_m, k_m, v_m, o_m, i_m, f_m, g_a, g_m) = jnp.split(proj, split_points, axis=-1)
        y_a = from_heads(moba_attention(to_heads(q_a, ATTN_HEADS), to_heads(k_a, ATTN_HEADS),
                                        to_heads(v_a, ATTN_HEADS), slopes))
        qk_m = jax.nn.silu(causal_depthwise_conv(jnp.concatenate([q_m, k_m], axis=-1), conv_qk[layer]))
        q_m, k_m = jnp.split(qk_m, 2, axis=-1)
        log_i = (i_m.astype(f32) + b_igate[layer].astype(f32)).transpose(0, 2, 1)
        log_f = jax.nn.log_sigmoid(f_m.astype(f32) + b_fgate[layer].astype(f32)).transpose(0, 2, 1)
        h_m = mlstm_chunkwise(to_heads(q_m, MLSTM_HEADS), to_heads(k_m, MLSTM_HEADS),
                              to_heads(v_m, MLSTM_HEADS), log_i, log_f)
        h_m = h_m * jax.nn.sigmoid(to_heads(o_m, MLSTM_HEADS).astype(f32))
        y_m = from_heads(head_layer_norm(h_m, mlstm_norm_g[layer])).astype(x.dtype)
        merged = (jax.nn.sigmoid(g_a) * (y_a @ w_out_attn[layer])
                  + jax.nn.sigmoid(g_m) * (y_m @ w_out_mlstm[layer]))
        x = x + merged @ w_o[layer]
        h = rms_norm(x, ffn_norm_g[layer])
        x = x + (jax.nn.silu(h @ w_ffn_gate[layer]) * (h @ w_ffn_up[layer])) @ w_ffn_down[layer]
    return rms_norm(x, final_norm_g)
```

```python
import functools

import numpy as np
import jax
import jax.numpy as jnp
from jax import lax
from jax.experimental import pallas as pl
from jax.experimental.pallas import tpu as pltpu

D_MODEL = 1024
ATTN_HEADS = 8
ATTN_HEAD_DIM = 64
ATTN_WIDTH = ATTN_HEADS * ATTN_HEAD_DIM
MOBA_BLOCK = 256
MOBA_TOP_K = 3
MLSTM_HEADS = 4
MLSTM_HEAD_DIM = 128
MLSTM_WIDTH = MLSTM_HEADS * MLSTM_HEAD_DIM
MLSTM_CHUNK = 128
CONV_WIDTH = 4
FFN_HIDDEN = 2816
NORM_EPS = 1e-6

LANES = 128
AUG = 2 * ATTN_HEAD_DIM
ATTN_AUG_WIDTH = ATTN_HEADS * AUG
N_BLOCK_COLS = AUG - ATTN_HEAD_DIM
HEADS_PER_STEP = 2
NEG_BIAS = -1e9
FFN_CHUNK = 256
OUT_TILE = 512
VMEM_LIMIT = 56 * 1024 * 1024

F32 = jnp.float32
BF16 = jnp.bfloat16
NT_DIMS = (((1,), (1,)), ((), ()))


def _rms(x, g):
    return x * lax.rsqrt(jnp.mean(x * x, axis=-1, keepdims=True) + NORM_EPS) * g


def _sigmoid(x):
    return 1.0 / (1.0 + jnp.exp(-x))


def _inproj_kernel(x_ref, g_ref, wq_ref, wk_ref, wv_ref, wm_ref, wif_ref, wg_ref, alibi_ref,
                   q_ref, k_ref, v_ref, kmean_ref, qkm_ref, vom_ref, gates_ref, gg_ref):
    blk = pl.program_id(0)
    hb = _rms(x_ref[...], g_ref[...]).astype(BF16)
    q_ref[...] = jnp.dot(hb, wq_ref[...], preferred_element_type=F32)
    kf = jnp.dot(hb, wk_ref[...], preferred_element_type=F32)
    kmean_ref[0] = jnp.mean(kf, axis=0, keepdims=True)
    lane = lax.broadcasted_iota(jnp.int32, (1, ATTN_AUG_WIDTH), 1) & (AUG - 1)
    row = lax.broadcasted_iota(jnp.int32, (MOBA_BLOCK, 1), 0).astype(F32)
    onehot = jnp.where(lane - ATTN_HEAD_DIM == blk, 1.0, 0.0)
    onehot = jnp.where(lane == AUG - 1, 0.0, onehot)
    k_ref[...] = (kf + onehot + alibi_ref[...] * row).astype(BF16)
    vf = jnp.dot(hb, wv_ref[...], preferred_element_type=F32)
    v_ref[...] = (vf + jnp.where(lane == ATTN_HEAD_DIM, 1.0, 0.0)).astype(BF16)
    m = jnp.dot(hb, wm_ref[...], preferred_element_type=F32)
    qkm_ref[...] = m[:, :2 * MLSTM_WIDTH]
    vom_ref[...] = m[:, 2 * MLSTM_WIDTH:]
    gates_ref[...] = jnp.dot(hb, wif_ref[...], preferred_element_type=F32)
    gg_ref[...] = jnp.dot(hb, wg_ref[...], preferred_element_type=F32)


def _inproj(x, g, wq, wk, wv, wm, wif, wg, alibi):
    s = x.shape[0]
    nb = s // MOBA_BLOCK
    row = lambda w: pl.BlockSpec((MOBA_BLOCK, w), lambda i: (i, 0))
    whole = pl.BlockSpec(memory_space=pltpu.VMEM)
    return pl.pallas_call(
        _inproj_kernel,
        grid=(nb,),
        in_specs=[row(D_MODEL), whole, whole, whole, whole, whole, whole, whole, whole],
        out_specs=[row(ATTN_WIDTH), row(ATTN_AUG_WIDTH), row(ATTN_AUG_WIDTH),
                   pl.BlockSpec((1, 1, ATTN_AUG_WIDTH), lambda i: (i, 0, 0)),
                   row(2 * MLSTM_WIDTH), row(2 * MLSTM_WIDTH), row(LANES), row(2 * D_MODEL)],
        out_shape=[jax.ShapeDtypeStruct((s, ATTN_WIDTH), F32),
                   jax.ShapeDtypeStruct((s, ATTN_AUG_WIDTH), BF16),
                   jax.ShapeDtypeStruct((s, ATTN_AUG_WIDTH), BF16),
                   jax.ShapeDtypeStruct((nb, 1, ATTN_AUG_WIDTH), F32),
                   jax.ShapeDtypeStruct((s, 2 * MLSTM_WIDTH), F32),
                   jax.ShapeDtypeStruct((s, 2 * MLSTM_WIDTH), F32),
                   jax.ShapeDtypeStruct((s, LANES), F32),
                   jax.ShapeDtypeStruct((s, 2 * D_MODEL), F32)],
        compiler_params=pltpu.CompilerParams(dimension_semantics=("arbitrary",),
                                             vmem_limit_bytes=VMEM_LIMIT),
        name="inproj",
    )(x, g, wq, wk, wv, wm, wif, wg, alibi)


def _moba_kernel(slope_ref, q_ref, kmean_ref, k_ref, v_ref, o_ref):
    i = pl.program_id(1)
    tq = q_ref.shape[0]
    nb = N_BLOCK_COLS
    blk = lax.broadcasted_iota(jnp.int32, (tq, nb), 1)
    r_io = lax.broadcasted_iota(jnp.int32, (tq, MOBA_BLOCK), 0)
    c_io = lax.broadcasted_iota(jnp.int32, (tq, MOBA_BLOCK), 1)
    own_start = pl.multiple_of(i * MOBA_BLOCK, MOBA_BLOCK)
    outs = []
    for hh in range(HEADS_PER_STEP):
        lanes = slice(hh * AUG, (hh + 1) * AUG)
        slope = slope_ref[0, hh:hh + 1, 0:1]
        qh = q_ref[:, hh * ATTN_HEAD_DIM:(hh + 1) * ATTN_HEAD_DIM]
        km = kmean_ref[:, hh * AUG:hh * AUG + ATTN_HEAD_DIM]
        gate = lax.dot_general(qh, km, NT_DIMS, precision=lax.Precision.HIGHEST,
                               preferred_element_type=F32)
        valid = blk < i
        g = jnp.where(valid, gate, -jnp.inf)
        sel = jnp.zeros((tq, nb), jnp.bool_)
        for _ in range(MOBA_TOP_K):
            mx = jnp.max(g, axis=1, keepdims=True)
            idx = jnp.min(jnp.where(g == mx, blk, nb), axis=1, keepdims=True)
            pick = blk == idx
            sel = jnp.logical_or(sel, pick)
            g = jnp.where(pick, -jnp.inf, g)
        bias = jnp.where(sel, (blk - i).astype(F32) * (slope * float(MOBA_BLOCK)), NEG_BIAS)
        bias = jnp.where(valid, bias, 0.0)
        bias = jnp.where(blk == nb - 1, 1.0, bias)
        q_aug = jnp.concatenate([qh * (ATTN_HEAD_DIM ** -0.5), bias], axis=1).astype(BF16)

        k_own = k_ref[pl.ds(own_start, MOBA_BLOCK), lanes]
        v_own = v_ref[pl.ds(own_start, MOBA_BLOCK), lanes]
        s = lax.dot_general(q_aug, k_own, NT_DIMS, preferred_element_type=F32)
        s = jnp.where(c_io <= r_io, s, -jnp.inf)
        m0 = jnp.max(s, axis=1, keepdims=True)
        acc0 = jnp.dot(jnp.exp(s - m0).astype(BF16), v_own, preferred_element_type=F32)

        def body(j, carry, q_aug=q_aug, lanes=lanes):
            m, acc = carry
            start = pl.multiple_of(j * MOBA_BLOCK, MOBA_BLOCK)
            kj = k_ref[pl.ds(start, MOBA_BLOCK), lanes]
            vj = v_ref[pl.ds(start, MOBA_BLOCK), lanes]
            sj = lax.dot_general(q_aug, kj, NT_DIMS, preferred_element_type=F32)
            m_new = jnp.maximum(m, jnp.max(sj, axis=1, keepdims=True))
            p = jnp.exp(sj - m_new).astype(BF16)
            acc = jnp.exp(m - m_new) * acc + jnp.dot(p, vj, preferred_element_type=F32)
            return m_new, acc

        _, acc = lax.fori_loop(0, i, body, (m0, acc0))
        outs.append(acc[:, :ATTN_HEAD_DIM] / acc[:, ATTN_HEAD_DIM:ATTN_HEAD_DIM + 1])
    o_ref[...] = jnp.concatenate(outs, axis=1).astype(o_ref.dtype)


def _moba(slope_tab, q, kmean, k_aug, v_aug):
    s = q.shape[0]
    nb = s // MOBA_BLOCK
    n_steps = ATTN_HEADS // HEADS_PER_STEP
    kv_spec = pl.BlockSpec((s, HEADS_PER_STEP * AUG), lambda p, i: (0, p))
    return pl.pallas_call(
        _moba_kernel,
        grid=(n_steps, nb),
        in_specs=[pl.BlockSpec((1, HEADS_PER_STEP, LANES), lambda p, i: (p, 0, 0)),
                  pl.BlockSpec((MOBA_BLOCK, HEADS_PER_STEP * ATTN_HEAD_DIM), lambda p, i: (i, p)),
                  pl.BlockSpec((N_BLOCK_COLS, HEADS_PER_STEP * AUG), lambda p, i: (0, p)),
                  kv_spec, kv_spec],
        out_specs=pl.BlockSpec((MOBA_BLOCK, HEADS_PER_STEP * ATTN_HEAD_DIM), lambda p, i: (i, p)),
        out_shape=jax.ShapeDtypeStruct((s, ATTN_WIDTH), BF16),
        compiler_params=pltpu.CompilerParams(dimension_semantics=("arbitrary", "arbitrary"),
                                             vmem_limit_bytes=VMEM_LIMIT),
        name="moba",
    )(slope_tab, q, kmean, k_aug, v_aug)


def _mlstm_kernel(qk_ref, vo_ref, gates_ref, conv_ref, gbias_ref, ng_ref, y_ref,
                  xbuf, c_ref, n_ref, m_ref):
    L = MLSTM_CHUNK
    d = MLSTM_HEAD_DIM
    halo = 8

    @pl.when(pl.program_id(0) == 0)
    def _():
        xbuf[0:halo, :] = jnp.zeros((halo, xbuf.shape[1]), F32)
        c_ref[...] = jnp.zeros(c_ref.shape, F32)
        n_ref[...] = jnp.zeros(n_ref.shape, F32)
        m_ref[...] = jnp.zeros(m_ref.shape, F32)

    xbuf[halo:halo + L, :] = qk_ref[...]
    w = conv_ref[...]
    y = xbuf[halo - CONV_WIDTH + 1:halo - CONV_WIDTH + 1 + L, :] * w[0:1, :]
    for j in range(1, CONV_WIDTH):
        off = halo - CONV_WIDTH + 1 + j
        y = y + xbuf[off:off + L, :] * w[j:j + 1, :]
    xbuf[0:halo, :] = xbuf[L:L + halo, :]
    qk = y * _sigmoid(y)

    a = gates_ref[...] + gbias_ref[...]
    lane = lax.broadcasted_iota(jnp.int32, (L, LANES), 1)
    is_f = jnp.logical_and(lane >= MLSTM_HEADS, lane < 2 * MLSTM_HEADS)
    log_f = jnp.where(is_f, jnp.minimum(a, 0.0) - jnp.log(1.0 + jnp.exp(-jnp.abs(a))), 0.0)
    t_io = lax.broadcasted_iota(jnp.int32, (L, L), 0)
    s_io = lax.broadcasted_iota(jnp.int32, (L, L), 1)
    causal = s_io <= t_io
    bcum = jnp.dot(causal.astype(F32), log_f, precision=lax.Precision.HIGHEST,
                   preferred_element_type=F32)
    colm = jnp.where(lane < MLSTM_HEADS, a, bcum)
    rowm = colm.T

    for hd in range(MLSTM_HEADS):
        hs = slice(hd * d, (hd + 1) * d)
        q = qk[:, hs]
        k = qk[:, MLSTM_WIDTH + hd * d:MLSTM_WIDTH + (hd + 1) * d] * (d ** -0.5)
        v = vo_ref[:, hs]
        og = vo_ref[:, MLSTM_WIDTH + hd * d:MLSTM_WIDTH + (hd + 1) * d]
        ic_c = colm[:, hd:hd + 1]
        bc_c = colm[:, MLSTM_HEADS + hd:MLSTM_HEADS + hd + 1]
        ic_r = rowm[hd:hd + 1, :]
        bc_r = rowm[MLSTM_HEADS + hd:MLSTM_HEADS + hd + 1, :]
        m_prev = m_ref[hd:hd + 1, 0:1]

        dmat = jnp.where(causal, bc_c - bc_r + ic_r, -jnp.inf)
        inter = bc_c + m_prev
        m_t = jnp.maximum(inter, jnp.max(dmat, axis=1, keepdims=True))
        w_intra = jnp.exp(dmat - m_t)
        w_inter = jnp.exp(inter - m_t)
        qb, kb, vb = q.astype(BF16), k.astype(BF16), v.astype(BF16)
        sc = lax.dot_general(qb, kb, NT_DIMS, preferred_element_type=F32) * w_intra
        c_st = c_ref[hd]
        n_st = n_ref[hd]
        num = (w_inter * jnp.dot(qb, c_st.astype(BF16), preferred_element_type=F32)
               + jnp.dot(sc.astype(BF16), vb, preferred_element_type=F32))
        den = (w_inter * jnp.sum(q * n_st, axis=1, keepdims=True)
               + jnp.sum(sc, axis=1, keepdims=True))
        h_out = num / jnp.maximum(jnp.abs(den), jnp.exp(-m_t))

        b_last = bc_c[L - 1:L, :]
        g_c = b_last - bc_c + ic_c
        m_new = jnp.maximum(b_last + m_prev, jnp.max(g_c, axis=0, keepdims=True))
        decay = jnp.exp(b_last + m_prev - m_new)
        kw = k * jnp.exp(g_c - m_new)
        c_ref[hd] = decay * c_st + jnp.dot(kw.T.astype(BF16), vb, preferred_element_type=F32)
        n_ref[hd] = decay * n_st + jnp.sum(kw, axis=0, keepdims=True)
        m_ref[hd:hd + 1, :] = jnp.broadcast_to(m_new, (1, LANES))

        hg = h_out * _sigmoid(og)
        mu = jnp.mean(hg, axis=1, keepdims=True)
        var = jnp.mean(jnp.square(hg - mu), axis=1, keepdims=True)
        y_ref[:, hs] = ((hg - mu) * lax.rsqrt(var + NORM_EPS) * ng_ref[:, hs]).astype(y_ref.dtype)


def _mlstm(qkm, vom, gates, conv, gbias, ng):
    s = qkm.shape[0]
    L = MLSTM_CHUNK
    row = lambda w: pl.BlockSpec((L, w), lambda t: (t, 0))
    whole = pl.BlockSpec(memory_space=pltpu.VMEM)
    return pl.pallas_call(
        _mlstm_kernel,
        grid=(s // L,),
        in_specs=[row(2 * MLSTM_WIDTH), row(2 * MLSTM_WIDTH), row(LANES), whole, whole, whole],
        out_specs=row(MLSTM_WIDTH),
        out_shape=jax.ShapeDtypeStruct((s, MLSTM_WIDTH), BF16),
        scratch_shapes=[pltpu.VMEM((L + 8, 2 * MLSTM_WIDTH), F32),
                        pltpu.VMEM((MLSTM_HEADS, MLSTM_HEAD_DIM, MLSTM_HEAD_DIM), F32),
                        pltpu.VMEM((MLSTM_HEADS, 1, MLSTM_HEAD_DIM), F32),
                        pltpu.VMEM((8, LANES), F32)],
        compiler_params=pltpu.CompilerParams(dimension_semantics=("arbitrary",),
                                             vmem_limit_bytes=VMEM_LIMIT),
        name="mlstm",
    )(qkm, vom, gates, conv, gbias, ng)


def _out_ffn_kernel(x_ref, ya_ref, ym_ref, gg_ref, woa_ref, wom_ref, wo_ref, fg_ref,
                    wgt_ref, wup_ref, wdn_ref, fin_ref, o_ref, *, final_norm):
    a = jnp.dot(ya_ref[...], woa_ref[...], preferred_element_type=F32)
    b = jnp.dot(ym_ref[...], wom_ref[...], preferred_element_type=F32)
    merged = _sigmoid(gg_ref[:, :D_MODEL]) * a + _sigmoid(gg_ref[:, D_MODEL:]) * b
    x1 = x_ref[...] + jnp.dot(merged.astype(BF16), wo_ref[...], preferred_element_type=F32)
    h2 = _rms(x1, fg_ref[...]).astype(BF16)
    acc = jnp.zeros(x1.shape, F32)
    for c in range(0, FFN_HIDDEN, FFN_CHUNK):
        gt = jnp.dot(h2, wgt_ref[:, c:c + FFN_CHUNK], preferred_element_type=F32)
        up = jnp.dot(h2, wup_ref[:, c:c + FFN_CHUNK], preferred_element_type=F32)
        act = (gt * _sigmoid(gt) * up).astype(BF16)
        acc = acc + jnp.dot(act, wdn_ref[c:c + FFN_CHUNK, :], preferred_element_type=F32)
    x2 = x1 + acc
    o_ref[...] = _rms(x2, fin_ref[...]) if final_norm else x2


def _out_ffn(x, ya, ym, gg, woa, wom, wo, fg, wgt, wup, wdn, fin, final_norm):
    s = x.shape[0]
    row = lambda w: pl.BlockSpec((OUT_TILE, w), lambda i: (i, 0))
    whole = pl.BlockSpec(memory_space=pltpu.VMEM)
    return pl.pallas_call(
        functools.partial(_out_ffn_kernel, final_norm=final_norm),
        grid=(s // OUT_TILE,),
        in_specs=[row(D_MODEL), row(ATTN_WIDTH), row(MLSTM_WIDTH), row(2 * D_MODEL)] + [whole] * 8,
        out_specs=row(D_MODEL),
        out_shape=jax.ShapeDtypeStruct((s, D_MODEL), F32),
        compiler_params=pltpu.CompilerParams(dimension_semantics=("arbitrary",),
                                             vmem_limit_bytes=VMEM_LIMIT),
        name="out_ffn",
    )(x, ya, ym, gg, woa, wom, wo, fg, wgt, wup, wdn, fin)


def _alibi_slopes():
    return np.exp2(-8.0 * np.arange(1, ATTN_HEADS + 1, dtype=np.float64) / ATTN_HEADS).astype(np.float32)


def _pad_heads(w):
    d = w.shape[0]
    w = w.reshape(d, ATTN_HEADS, ATTN_HEAD_DIM)
    w = jnp.pad(w, ((0, 0), (0, 0), (0, AUG - ATTN_HEAD_DIM)))
    return w.reshape(d, ATTN_AUG_WIDTH)


def kernel(x, mix_norm_g, w_in, conv_qk, b_igate, b_fgate, mlstm_norm_g, w_out_attn, w_out_mlstm,
           w_o, ffn_norm_g, w_ffn_gate, w_ffn_up, w_ffn_down, final_norm_g):
    batch, seq, _ = x.shape
    depth = w_in.shape[0]
    assert seq % OUT_TILE == 0 and seq // MOBA_BLOCK <= N_BLOCK_COLS
    slopes = _alibi_slopes()
    alibi = np.zeros((1, ATTN_AUG_WIDTH), np.float32)
    alibi[0, AUG - 1::AUG] = slopes
    alibi = jnp.asarray(alibi)
    slope_tab = jnp.asarray(np.broadcast_to(
        slopes.reshape(ATTN_HEADS // HEADS_PER_STEP, HEADS_PER_STEP, 1),
        (ATTN_HEADS // HEADS_PER_STEP, HEADS_PER_STEP, LANES)).copy())
    a0, a1, a2 = ATTN_WIDTH, 2 * ATTN_WIDTH, 3 * ATTN_WIDTH
    m1 = a2 + 4 * MLSTM_WIDTH
    g0 = m1 + 2 * MLSTM_HEADS

    outs = []
    for bi in range(batch):
        xb = x[bi]
        for layer in range(depth):
            w = w_in[layer]
            wq = w[:, :a0].astype(BF16)
            wk = _pad_heads(w[:, a0:a1]).astype(BF16)
            wv = _pad_heads(w[:, a1:a2]).astype(BF16)
            wm = w[:, a2:m1].astype(BF16)
            wif = jnp.pad(w[:, m1:g0], ((0, 0), (0, LANES - 2 * MLSTM_HEADS))).astype(BF16)
            wg = w[:, g0:].astype(BF16)
            q, k_aug, v_aug, kmean, qkm, vom, gates, gg = _inproj(
                xb, mix_norm_g[layer][None, :], wq, wk, wv, wm, wif, wg, alibi)
            kmean = jnp.pad(kmean.reshape(kmean.shape[0], ATTN_AUG_WIDTH),
                            ((0, N_BLOCK_COLS - kmean.shape[0]), (0, 0)))
            ya = _moba(slope_tab, q, kmean, k_aug, v_aug)
            gbias = jnp.pad(jnp.concatenate([b_igate[layer], b_fgate[layer]])[None, :],
                            ((0, 0), (0, LANES - 2 * MLSTM_HEADS))).astype(F32)
            ym = _mlstm(qkm, vom, gates, conv_qk[layer], gbias, mlstm_norm_g[layer][None, :])
            xb = _out_ffn(xb, ya, ym, gg,
                          w_out_attn[layer].astype(BF16), w_out_mlstm[layer].astype(BF16),
                          w_o[layer].astype(BF16), ffn_norm_g[layer][None, :],
                          w_ffn_gate[layer].astype(BF16), w_ffn_up[layer].astype(BF16),
                          w_ffn_down[layer].astype(BF16), final_norm_g[None, :],
                          final_norm=(layer == depth - 1))
        outs.append(xb)
    return jnp.stack(outs, axis=0)
```

```python
import functools

import numpy as np
import jax
import jax.numpy as jnp
from jax import lax
from jax.experimental import pallas as pl
from jax.experimental.pallas import tpu as pltpu

D_MODEL = 1024
ATTN_HEADS = 8
ATTN_HEAD_DIM = 64
ATTN_WIDTH = ATTN_HEADS * ATTN_HEAD_DIM
MOBA_BLOCK = 256
MOBA_TOP_K = 3
MLSTM_HEADS = 4
MLSTM_HEAD_DIM = 128
MLSTM_WIDTH = MLSTM_HEADS * MLSTM_HEAD_DIM
MLSTM_CHUNK = 128
CONV_WIDTH = 4
FFN_HIDDEN = 2816
NORM_EPS = 1e-6

LANES = 128
AUG = 2 * ATTN_HEAD_DIM
ATTN_AUG_WIDTH = ATTN_HEADS * AUG
N_BLOCK_COLS = AUG - ATTN_HEAD_DIM
HEADS_PER_STEP = 2
KEY_GROUP = 4
NEG_BIAS = -1e9
FFN_CHUNK = 256
OUT_TILE = 512
VMEM_LIMIT = 56 * 1024 * 1024

F32 = jnp.float32
BF16 = jnp.bfloat16
NT_DIMS = (((1,), (1,)), ((), ()))


def _rms(x, g):
    return x * lax.rsqrt(jnp.mean(x * x, axis=-1, keepdims=True) + NORM_EPS) * g


def _sigmoid(x):
    return 1.0 / (1.0 + jnp.exp(-x))


def _inproj_kernel(x_ref, g_ref, wq_ref, wk_ref, wv_ref, wm_ref, wif_ref, wg_ref, alibi_ref,
                   q_ref, k_ref, v_ref, kmean_ref, qkm_ref, vom_ref, gates_ref, gg_ref):
    blk = pl.program_id(0)
    hb = _rms(x_ref[...], g_ref[...]).astype(BF16)
    q_ref[...] = jnp.dot(hb, wq_ref[...], preferred_element_type=F32)
    kf = jnp.dot(hb, wk_ref[...], preferred_element_type=F32)
    kmean_ref[0] = jnp.mean(kf, axis=0, keepdims=True)
    lane = lax.broadcasted_iota(jnp.int32, (1, ATTN_AUG_WIDTH), 1) & (AUG - 1)
    row = lax.broadcasted_iota(jnp.int32, (MOBA_BLOCK, 1), 0).astype(F32)
    onehot = jnp.where(lane - ATTN_HEAD_DIM == blk, 1.0, 0.0)
    onehot = jnp.where(lane == AUG - 1, 0.0, onehot)
    k_ref[...] = (kf + onehot + alibi_ref[...] * row).astype(BF16)
    vf = jnp.dot(hb, wv_ref[...], preferred_element_type=F32)
    v_ref[...] = (vf + jnp.where(lane == ATTN_HEAD_DIM, 1.0, 0.0)).astype(BF16)
    m = jnp.dot(hb, wm_ref[...], preferred_element_type=F32)
    qkm_ref[...] = m[:, :2 * MLSTM_WIDTH]
    vom_ref[...] = m[:, 2 * MLSTM_WIDTH:]
    gates_ref[...] = jnp.dot(hb, wif_ref[...], preferred_element_type=F32)
    gg_ref[...] = jnp.dot(hb, wg_ref[...], preferred_element_type=F32)


def _inproj(x, g, wq, wk, wv, wm, wif, wg, alibi):
    s = x.shape[0]
    nb = s // MOBA_BLOCK
    row = lambda w: pl.BlockSpec((MOBA_BLOCK, w), lambda i: (i, 0))
    whole = pl.BlockSpec(memory_space=pltpu.VMEM)
    return pl.pallas_call(
        _inproj_kernel,
        grid=(nb,),
        in_specs=[row(D_MODEL), whole, whole, whole, whole, whole, whole, whole, whole],
        out_specs=[row(ATTN_WIDTH), row(ATTN_AUG_WIDTH), row(ATTN_AUG_WIDTH),
                   pl.BlockSpec((1, 1, ATTN_AUG_WIDTH), lambda i: (i, 0, 0)),
                   row(2 * MLSTM_WIDTH), row(2 * MLSTM_WIDTH), row(LANES), row(2 * D_MODEL)],
        out_shape=[jax.ShapeDtypeStruct((s, ATTN_WIDTH), F32),
                   jax.ShapeDtypeStruct((s, ATTN_AUG_WIDTH), BF16),
                   jax.ShapeDtypeStruct((s, ATTN_AUG_WIDTH), BF16),
                   jax.ShapeDtypeStruct((nb, 1, ATTN_AUG_WIDTH), F32),
                   jax.ShapeDtypeStruct((s, 2 * MLSTM_WIDTH), F32),
                   jax.ShapeDtypeStruct((s, 2 * MLSTM_WIDTH), F32),
                   jax.ShapeDtypeStruct((s, LANES), F32),
                   jax.ShapeDtypeStruct((s, 2 * D_MODEL), F32)],
        compiler_params=pltpu.CompilerParams(dimension_semantics=("arbitrary",),
                                             vmem_limit_bytes=VMEM_LIMIT),
        name="inproj",
    )(x, g, wq, wk, wv, wm, wif, wg, alibi)


def _moba_kernel(slope_ref, q_ref, kmean_ref, k_ref, v_ref, o_ref):
    i = pl.program_id(1)
    tq = q_ref.shape[0]
    nb = N_BLOCK_COLS
    blk = lax.broadcasted_iota(jnp.int32, (tq, nb), 1)
    r_io = lax.broadcasted_iota(jnp.int32, (tq, MOBA_BLOCK), 0)
    c_io = lax.broadcasted_iota(jnp.int32, (tq, MOBA_BLOCK), 1)
    own_start = pl.multiple_of(i * MOBA_BLOCK, MOBA_BLOCK)
    head_lanes = [slice(hh * AUG, (hh + 1) * AUG) for hh in range(HEADS_PER_STEP)]
    q_augs, carry0 = [], []
    for hh in range(HEADS_PER_STEP):
        lanes = head_lanes[hh]
        slope = slope_ref[0, hh:hh + 1, 0:1]
        qh = q_ref[:, hh * ATTN_HEAD_DIM:(hh + 1) * ATTN_HEAD_DIM]
        km = kmean_ref[:, hh * AUG:hh * AUG + ATTN_HEAD_DIM]
        gate = lax.dot_general(qh, km, NT_DIMS, precision=lax.Precision.HIGHEST,
                               preferred_element_type=F32)
        valid = blk < i
        g = jnp.where(valid, gate, -jnp.inf)
        sel = jnp.zeros((tq, nb), jnp.bool_)
        for _ in range(MOBA_TOP_K):
            mx = jnp.max(g, axis=1, keepdims=True)
            idx = jnp.min(jnp.where(g == mx, blk, nb), axis=1, keepdims=True)
            pick = blk == idx
            sel = jnp.logical_or(sel, pick)
            g = jnp.where(pick, -jnp.inf, g)
        bias = jnp.where(sel, (blk - i).astype(F32) * (slope * float(MOBA_BLOCK)), NEG_BIAS)
        bias = jnp.where(valid, bias, 0.0)
        bias = jnp.where(blk == nb - 1, 1.0, bias)
        q_aug = jnp.concatenate([qh * (ATTN_HEAD_DIM ** -0.5), bias], axis=1).astype(BF16)

        k_own = k_ref[pl.ds(own_start, MOBA_BLOCK), lanes]
        v_own = v_ref[pl.ds(own_start, MOBA_BLOCK), lanes]
        s = lax.dot_general(q_aug, k_own, NT_DIMS, preferred_element_type=F32)
        s = jnp.where(c_io <= r_io, s, -jnp.inf)
        m0 = jnp.max(s, axis=1, keepdims=True)
        acc0 = jnp.dot(jnp.exp(s - m0).astype(BF16), v_own, preferred_element_type=F32)
        q_augs.append(q_aug)
        carry0.extend([m0, acc0])

    def attend(start, rows, carry):
        new = []
        for hh in range(HEADS_PER_STEP):
            m, acc = carry[2 * hh], carry[2 * hh + 1]
            kj = k_ref[pl.ds(start, rows), head_lanes[hh]]
            vj = v_ref[pl.ds(start, rows), head_lanes[hh]]
            sj = lax.dot_general(q_augs[hh], kj, NT_DIMS, preferred_element_type=F32)
            m_new = jnp.maximum(m, jnp.max(sj, axis=1, keepdims=True))
            p = jnp.exp(sj - m_new).astype(BF16)
            acc = jnp.exp(m - m_new) * acc + jnp.dot(p, vj, preferred_element_type=F32)
            new.extend([m_new, acc])
        return tuple(new)

    group_rows = KEY_GROUP * MOBA_BLOCK
    n_groups = i // KEY_GROUP
    carry = lax.fori_loop(
        0, n_groups,
        lambda jg, c: attend(pl.multiple_of(jg * group_rows, group_rows), group_rows, c),
        tuple(carry0))
    carry = lax.fori_loop(
        n_groups * KEY_GROUP, i,
        lambda j, c: attend(pl.multiple_of(j * MOBA_BLOCK, MOBA_BLOCK), MOBA_BLOCK, c),
        carry)
    outs = [carry[2 * hh + 1][:, :ATTN_HEAD_DIM] / carry[2 * hh + 1][:, ATTN_HEAD_DIM:ATTN_HEAD_DIM + 1]
            for hh in range(HEADS_PER_STEP)]
    o_ref[...] = jnp.concatenate(outs, axis=1).astype(o_ref.dtype)


def _moba(slope_tab, q, kmean, k_aug, v_aug):
    s = q.shape[0]
    nb = s // MOBA_BLOCK
    n_steps = ATTN_HEADS // HEADS_PER_STEP
    kv_spec = pl.BlockSpec((s, HEADS_PER_STEP * AUG), lambda p, i: (0, p))
    return pl.pallas_call(
        _moba_kernel,
        grid=(n_steps, nb),
        in_specs=[pl.BlockSpec((1, HEADS_PER_STEP, LANES), lambda p, i: (p, 0, 0)),
                  pl.BlockSpec((MOBA_BLOCK, HEADS_PER_STEP * ATTN_HEAD_DIM), lambda p, i: (i, p)),
                  pl.BlockSpec((N_BLOCK_COLS, HEADS_PER_STEP * AUG), lambda p, i: (0, p)),
                  kv_spec, kv_spec],
        out_specs=pl.BlockSpec((MOBA_BLOCK, HEADS_PER_STEP * ATTN_HEAD_DIM), lambda p, i: (i, p)),
        out_shape=jax.ShapeDtypeStruct((s, ATTN_WIDTH), BF16),
        compiler_params=pltpu.CompilerParams(dimension_semantics=("arbitrary", "arbitrary"),
                                             vmem_limit_bytes=VMEM_LIMIT),
        name="moba",
    )(slope_tab, q, kmean, k_aug, v_aug)


def _mlstm_kernel(qk_ref, vo_ref, gates_ref, conv_ref, gbias_ref, ng_ref, y_ref,
                  xbuf, c_ref, n_ref, m_ref):
    L = MLSTM_CHUNK
    d = MLSTM_HEAD_DIM
    halo = 8

    @pl.when(pl.program_id(0) == 0)
    def _():
        xbuf[0:halo, :] = jnp.zeros((halo, xbuf.shape[1]), F32)
        c_ref[...] = jnp.zeros(c_ref.shape, F32)
        n_ref[...] = jnp.zeros(n_ref.shape, F32)
        m_ref[...] = jnp.zeros(m_ref.shape, F32)

    xbuf[halo:halo + L, :] = qk_ref[...]
    w = conv_ref[...]
    y = xbuf[halo - CONV_WIDTH + 1:halo - CONV_WIDTH + 1 + L, :] * w[0:1, :]
    for j in range(1, CONV_WIDTH):
        off = halo - CONV_WIDTH + 1 + j
        y = y + xbuf[off:off + L, :] * w[j:j + 1, :]
    xbuf[0:halo, :] = xbuf[L:L + halo, :]
    qk = y * _sigmoid(y)

    a = gates_ref[...] + gbias_ref[...]
    lane = lax.broadcasted_iota(jnp.int32, (L, LANES), 1)
    is_f = jnp.logical_and(lane >= MLSTM_HEADS, lane < 2 * MLSTM_HEADS)
    log_f = jnp.where(is_f, jnp.minimum(a, 0.0) - jnp.log(1.0 + jnp.exp(-jnp.abs(a))), 0.0)
    t_io = lax.broadcasted_iota(jnp.int32, (L, L), 0)
    s_io = lax.broadcasted_iota(jnp.int32, (L, L), 1)
    causal = s_io <= t_io
    bcum = jnp.dot(causal.astype(F32), log_f, precision=lax.Precision.HIGHEST,
                   preferred_element_type=F32)
    colm = jnp.where(lane < MLSTM_HEADS, a, bcum)
    rowm = colm.T

    for hd in range(MLSTM_HEADS):
        hs = slice(hd * d, (hd + 1) * d)
        q = qk[:, hs]
        k = qk[:, MLSTM_WIDTH + hd * d:MLSTM_WIDTH + (hd + 1) * d] * (d ** -0.5)
        v = vo_ref[:, hs]
        og = vo_ref[:, MLSTM_WIDTH + hd * d:MLSTM_WIDTH + (hd + 1) * d]
        ic_c = colm[:, hd:hd + 1]
        bc_c = colm[:, MLSTM_HEADS + hd:MLSTM_HEADS + hd + 1]
        ic_r = rowm[hd:hd + 1, :]
        bc_r = rowm[MLSTM_HEADS + hd:MLSTM_HEADS + hd + 1, :]
        m_prev = m_ref[hd:hd + 1, 0:1]

        dmat = jnp.where(causal, bc_c - bc_r + ic_r, -jnp.inf)
        inter = bc_c + m_prev
        m_t = jnp.maximum(inter, jnp.max(dmat, axis=1, keepdims=True))
        w_intra = jnp.exp(dmat - m_t)
        w_inter = jnp.exp(inter - m_t)
        qb, kb, vb = q.astype(BF16), k.astype(BF16), v.astype(BF16)
        sc = lax.dot_general(qb, kb, NT_DIMS, preferred_element_type=F32) * w_intra
        c_st = c_ref[hd]
        n_st = n_ref[hd]
        num = (w_inter * jnp.dot(qb, c_st.astype(BF16), preferred_element_type=F32)
               + jnp.dot(sc.astype(BF16), vb, preferred_element_type=F32))
        den = (w_inter * jnp.sum(q * n_st, axis=1, keepdims=True)
               + jnp.sum(sc, axis=1, keepdims=True))
        h_out = num / jnp.maximum(jnp.abs(den), jnp.exp(-m_t))

        b_last = bc_c[L - 1:L, :]
        g_c = b_last - bc_c + ic_c
        m_new = jnp.maximum(b_last + m_prev, jnp.max(g_c, axis=0, keepdims=True))
        decay = jnp.exp(b_last + m_prev - m_new)
        kw = k * jnp.exp(g_c - m_new)
        c_ref[hd] = decay * c_st + jnp.dot(kw.T.astype(BF16), vb, preferred_element_type=F32)
        n_ref[hd] = decay * n_st + jnp.sum(kw, axis=0, keepdims=True)
        m_ref[hd:hd + 1, :] = jnp.broadcast_to(m_new, (1, LANES))

        hg = h_out * _sigmoid(og)
        mu = jnp.mean(hg, axis=1, keepdims=True)
        var = jnp.mean(jnp.square(hg - mu), axis=1, keepdims=True)
        y_ref[:, hs] = ((hg - mu) * lax.rsqrt(var + NORM_EPS) * ng_ref[:, hs]).astype(y_ref.dtype)


def _mlstm(qkm, vom, gates, conv, gbias, ng):
    s = qkm.shape[0]
    L = MLSTM_CHUNK
    row = lambda w: pl.BlockSpec((L, w), lambda t: (t, 0))
    whole = pl.BlockSpec(memory_space=pltpu.VMEM)
    return pl.pallas_call(
        _mlstm_kernel,
        grid=(s // L,),
        in_specs=[row(2 * MLSTM_WIDTH), row(2 * MLSTM_WIDTH), row(LANES), whole, whole, whole],
        out_specs=row(MLSTM_WIDTH),
        out_shape=jax.ShapeDtypeStruct((s, MLSTM_WIDTH), BF16),
        scratch_shapes=[pltpu.VMEM((L + 8, 2 * MLSTM_WIDTH), F32),
                        pltpu.VMEM((MLSTM_HEADS, MLSTM_HEAD_DIM, MLSTM_HEAD_DIM), F32),
                        pltpu.VMEM((MLSTM_HEADS, 1, MLSTM_HEAD_DIM), F32),
                        pltpu.VMEM((8, LANES), F32)],
        compiler_params=pltpu.CompilerParams(dimension_semantics=("arbitrary",),
                                             vmem_limit_bytes=VMEM_LIMIT),
        name="mlstm",
    )(qkm, vom, gates, conv, gbias, ng)


def _out_ffn_kernel(x_ref, ya_ref, ym_ref, gg_ref, woa_ref, wom_ref, wo_ref, fg_ref,
                    wgt_ref, wup_ref, wdn_ref, fin_ref, o_ref, *, final_norm):
    a = jnp.dot(ya_ref[...], woa_ref[...], preferred_element_type=F32)
    b = jnp.dot(ym_ref[...], wom_ref[...], preferred_element_type=F32)
    merged = _sigmoid(gg_ref[:, :D_MODEL]) * a + _sigmoid(gg_ref[:, D_MODEL:]) * b
    x1 = x_ref[...] + jnp.dot(merged.astype(BF16), wo_ref[...], preferred_element_type=F32)
    h2 = _rms(x1, fg_ref[...]).astype(BF16)
    acc = jnp.zeros(x1.shape, F32)
    for c in range(0, FFN_HIDDEN, FFN_CHUNK):
        gt = jnp.dot(h2, wgt_ref[:, c:c + FFN_CHUNK], preferred_element_type=F32)
        up = jnp.dot(h2, wup_ref[:, c:c + FFN_CHUNK], preferred_element_type=F32)
        act = (gt * _sigmoid(gt) * up).astype(BF16)
        acc = acc + jnp.dot(act, wdn_ref[c:c + FFN_CHUNK, :], preferred_element_type=F32)
    x2 = x1 + acc
    o_ref[...] = _rms(x2, fin_ref[...]) if final_norm else x2


def _out_ffn(x, ya, ym, gg, woa, wom, wo, fg, wgt, wup, wdn, fin, final_norm):
    s = x.shape[0]
    row = lambda w: pl.BlockSpec((OUT_TILE, w), lambda i: (i, 0))
    whole = pl.BlockSpec(memory_space=pltpu.VMEM)
    return pl.pallas_call(
        functools.partial(_out_ffn_kernel, final_norm=final_norm),
        grid=(s // OUT_TILE,),
        in_specs=[row(D_MODEL), row(ATTN_WIDTH), row(MLSTM_WIDTH), row(2 * D_MODEL)] + [whole] * 8,
        out_specs=row(D_MODEL),
        out_shape=jax.ShapeDtypeStruct((s, D_MODEL), F32),
        compiler_params=pltpu.CompilerParams(dimension_semantics=("arbitrary",),
                                             vmem_limit_bytes=VMEM_LIMIT),
        name="out_ffn",
    )(x, ya, ym, gg, woa, wom, wo, fg, wgt, wup, wdn, fin)


def _alibi_slopes():
    return np.exp2(-8.0 * np.arange(1, ATTN_HEADS + 1, dtype=np.float64) / ATTN_HEADS).astype(np.float32)


def _pad_heads(w):
    d = w.shape[0]
    w = w.reshape(d, ATTN_HEADS, ATTN_HEAD_DIM)
    w = jnp.pad(w, ((0, 0), (0, 0), (0, AUG - ATTN_HEAD_DIM)))
    return w.reshape(d, ATTN_AUG_WIDTH)


def kernel(x, mix_norm_g, w_in, conv_qk, b_igate, b_fgate, mlstm_norm_g, w_out_attn, w_out_mlstm,
           w_o, ffn_norm_g, w_ffn_gate, w_ffn_up, w_ffn_down, final_norm_g):
    batch, seq, _ = x.shape
    depth = w_in.shape[0]
    assert seq % OUT_TILE == 0 and seq // MOBA_BLOCK <= N_BLOCK_COLS
    slopes = _alibi_slopes()
    alibi = np.zeros((1, ATTN_AUG_WIDTH), np.float32)
    alibi[0, AUG - 1::AUG] = slopes
    alibi = jnp.asarray(alibi)
    slope_tab = jnp.asarray(np.broadcast_to(
        slopes.reshape(ATTN_HEADS // HEADS_PER_STEP, HEADS_PER_STEP, 1),
        (ATTN_HEADS // HEADS_PER_STEP, HEADS_PER_STEP, LANES)).copy())
    a0, a1, a2 = ATTN_WIDTH, 2 * ATTN_WIDTH, 3 * ATTN_WIDTH
    m1 = a2 + 4 * MLSTM_WIDTH
    g0 = m1 + 2 * MLSTM_HEADS

    outs = []
    for bi in range(batch):
        xb = x[bi]
        for layer in range(depth):
            w = w_in[layer]
            wq = w[:, :a0].astype(BF16)
            wk = _pad_heads(w[:, a0:a1]).astype(BF16)
            wv = _pad_heads(w[:, a1:a2]).astype(BF16)
            wm = w[:, a2:m1].astype(BF16)
            wif = jnp.pad(w[:, m1:g0], ((0, 0), (0, LANES - 2 * MLSTM_HEADS))).astype(BF16)
            wg = w[:, g0:].astype(BF16)
            q, k_aug, v_aug, kmean, qkm, vom, gates, gg = _inproj(
                xb, mix_norm_g[layer][None, :], wq, wk, wv, wm, wif, wg, alibi)
            kmean = jnp.pad(kmean.reshape(kmean.shape[0], ATTN_AUG_WIDTH),
                            ((0, N_BLOCK_COLS - kmean.shape[0]), (0, 0)))
            ya = _moba(slope_tab, q, kmean, k_aug, v_aug)
            gbias = jnp.pad(jnp.concatenate([b_igate[layer], b_fgate[layer]])[None, :],
                            ((0, 0), (0, LANES - 2 * MLSTM_HEADS))).astype(F32)
            ym = _mlstm(qkm, vom, gates, conv_qk[layer], gbias, mlstm_norm_g[layer][None, :])
            xb = _out_ffn(xb, ya, ym, gg,
                          w_out_attn[layer].astype(BF16), w_out_mlstm[layer].astype(BF16),
                          w_o[layer].astype(BF16), ffn_norm_g[layer][None, :],
                          w_ffn_gate[layer].astype(BF16), w_ffn_up[layer].astype(BF16),
                          w_ffn_down[layer].astype(BF16), final_norm_g[None, :],
                          final_norm=(layer == depth - 1))
        outs.append(xb)
    return jnp.stack(outs, axis=0)
```

```python
import functools

import numpy as np
import jax
import jax.numpy as jnp
from jax import lax
from jax.experimental import pallas as pl
from jax.experimental.pallas import tpu as pltpu

D_MODEL = 1024
ATTN_HEADS = 8
ATTN_HEAD_DIM = 64
ATTN_WIDTH = ATTN_HEADS * ATTN_HEAD_DIM
MOBA_BLOCK = 256
MOBA_TOP_K = 3
MLSTM_HEADS = 4
MLSTM_HEAD_DIM = 128
MLSTM_WIDTH = MLSTM_HEADS * MLSTM_HEAD_DIM
MLSTM_CHUNK = 128
CONV_WIDTH = 4
FFN_HIDDEN = 2816
NORM_EPS = 1e-6

LANES = 128
AUG = 2 * ATTN_HEAD_DIM
ATTN_AUG_WIDTH = ATTN_HEADS * AUG
N_BLOCK_COLS = AUG - ATTN_HEAD_DIM
HEADS_PER_STEP = 2
KEY_GROUP = 4
NEG_BIAS = -1e9
FFN_CHUNK = 256
OUT_TILE = 512
VMEM_LIMIT = 56 * 1024 * 1024

F32 = jnp.float32
BF16 = jnp.bfloat16
NT_DIMS = (((1,), (1,)), ((), ()))


def _rms(x, g):
    return x * lax.rsqrt(jnp.mean(x * x, axis=-1, keepdims=True) + NORM_EPS) * g


def _sigmoid(x):
    return 1.0 / (1.0 + jnp.exp(-x))


def _inproj_kernel(x_ref, g_ref, wq_ref, wk_ref, wv_ref, wm_ref, wif_ref, wg_ref, alibi_ref,
                   q_ref, k_ref, v_ref, kmean_ref, qkm_ref, vom_ref, gates_ref, gg_ref):
    blk = pl.program_id(0)
    hb = _rms(x_ref[...], g_ref[...]).astype(BF16)
    q_ref[...] = jnp.dot(hb, wq_ref[...], preferred_element_type=F32)
    kf = jnp.dot(hb, wk_ref[...], preferred_element_type=F32)
    kmean_ref[0] = jnp.mean(kf, axis=0, keepdims=True)
    lane = lax.broadcasted_iota(jnp.int32, (1, ATTN_AUG_WIDTH), 1) & (AUG - 1)
    row = lax.broadcasted_iota(jnp.int32, (MOBA_BLOCK, 1), 0).astype(F32)
    onehot = jnp.where(lane - ATTN_HEAD_DIM == blk, 1.0, 0.0)
    onehot = jnp.where(lane == AUG - 1, 0.0, onehot)
    k_ref[...] = (kf + onehot + alibi_ref[...] * row).astype(BF16)
    vt = lax.dot_general(wv_ref[...], hb, NT_DIMS, preferred_element_type=F32)
    feat = lax.broadcasted_iota(jnp.int32, (ATTN_AUG_WIDTH, 1), 0) & (AUG - 1)
    v_ref[0] = (vt + jnp.where(feat == ATTN_HEAD_DIM, 1.0, 0.0)).astype(BF16)
    m = jnp.dot(hb, wm_ref[...], preferred_element_type=F32)
    qkm_ref[...] = m[:, :2 * MLSTM_WIDTH]
    vom_ref[...] = m[:, 2 * MLSTM_WIDTH:]
    gates_ref[...] = jnp.dot(hb, wif_ref[...], preferred_element_type=F32)
    gg_ref[...] = jnp.dot(hb, wg_ref[...], preferred_element_type=F32)


def _inproj(x, g, wq, wk, wv, wm, wif, wg, alibi):
    s = x.shape[0]
    nb = s // MOBA_BLOCK
    row = lambda w: pl.BlockSpec((MOBA_BLOCK, w), lambda i: (i, 0))
    whole = pl.BlockSpec(memory_space=pltpu.VMEM)
    return pl.pallas_call(
        _inproj_kernel,
        grid=(nb,),
        in_specs=[row(D_MODEL), whole, whole, whole, whole, whole, whole, whole, whole],
        out_specs=[row(ATTN_WIDTH), row(ATTN_AUG_WIDTH),
                   pl.BlockSpec((1, ATTN_AUG_WIDTH, MOBA_BLOCK), lambda i: (i, 0, 0)),
                   pl.BlockSpec((1, 1, ATTN_AUG_WIDTH), lambda i: (i, 0, 0)),
                   row(2 * MLSTM_WIDTH), row(2 * MLSTM_WIDTH), row(LANES), row(2 * D_MODEL)],
        out_shape=[jax.ShapeDtypeStruct((s, ATTN_WIDTH), F32),
                   jax.ShapeDtypeStruct((s, ATTN_AUG_WIDTH), BF16),
                   jax.ShapeDtypeStruct((nb, ATTN_AUG_WIDTH, MOBA_BLOCK), BF16),
                   jax.ShapeDtypeStruct((nb, 1, ATTN_AUG_WIDTH), F32),
                   jax.ShapeDtypeStruct((s, 2 * MLSTM_WIDTH), F32),
                   jax.ShapeDtypeStruct((s, 2 * MLSTM_WIDTH), F32),
                   jax.ShapeDtypeStruct((s, LANES), F32),
                   jax.ShapeDtypeStruct((s, 2 * D_MODEL), F32)],
        compiler_params=pltpu.CompilerParams(dimension_semantics=("arbitrary",),
                                             vmem_limit_bytes=VMEM_LIMIT),
        name="inproj",
    )(x, g, wq, wk, wv, wm, wif, wg, alibi)


def _col_max(st):
    rows, n = st.shape
    fan = 8
    while rows > fan * 8 and rows % fan == 0:
        st = jnp.max(st.reshape(fan, rows // fan, n), axis=0)
        rows //= fan
    return jnp.max(st, axis=0, keepdims=True)


def _moba_kernel(slope_ref, q_ref, kmean_ref, k_ref, vt_ref, o_ref, s_ref):
    i = pl.program_id(1)
    tq = q_ref.shape[0]
    nb = N_BLOCK_COLS
    group_rows = KEY_GROUP * MOBA_BLOCK
    blk = lax.broadcasted_iota(jnp.int32, (tq, nb), 1)
    head_lanes = [slice(hh * AUG, (hh + 1) * AUG) for hh in range(HEADS_PER_STEP)]
    q_augs = []
    for hh in range(HEADS_PER_STEP):
        slope = slope_ref[0, hh:hh + 1, 0:1]
        qh = q_ref[:, hh * ATTN_HEAD_DIM:(hh + 1) * ATTN_HEAD_DIM]
        km = kmean_ref[:, hh * AUG:hh * AUG + ATTN_HEAD_DIM]
        gate = lax.dot_general(qh, km, NT_DIMS, precision=lax.Precision.HIGHEST,
                               preferred_element_type=F32)
        valid = blk < i
        g = jnp.where(valid, gate, -jnp.inf)
        sel = jnp.zeros((tq, nb), jnp.bool_)
        for _ in range(MOBA_TOP_K):
            mx = jnp.max(g, axis=1, keepdims=True)
            idx = jnp.min(jnp.where(g == mx, blk, nb), axis=1, keepdims=True)
            pick = blk == idx
            sel = jnp.logical_or(sel, pick)
            g = jnp.where(pick, -jnp.inf, g)
        bias = jnp.where(sel, (blk - i).astype(F32) * (slope * float(MOBA_BLOCK)), NEG_BIAS)
        bias = jnp.where(valid, bias, 0.0)
        bias = jnp.where(blk == nb - 1, 1.0, bias)
        q_aug = jnp.concatenate([qh * (ATTN_HEAD_DIM ** -0.5), bias], axis=1)
        qt = q_aug.T.astype(BF16)

        q_augs.append(qt)

    heads = range(HEADS_PER_STEP)
    n_past = i // KEY_GROUP
    own_first = n_past * KEY_GROUP
    hidden = (lax.broadcasted_iota(jnp.int32, (group_rows, tq), 0)
              - lax.broadcasted_iota(jnp.int32, (group_rows, tq), 1)) > (i - own_first) * MOBA_BLOCK

    def score(first, slot, causal):
        start = pl.multiple_of(first * MOBA_BLOCK, group_rows)
        col_max = []
        for hh in heads:
            st = jnp.dot(k_ref[pl.ds(start, group_rows), head_lanes[hh]], q_augs[hh],
                         preferred_element_type=F32)
            if causal:
                st = jnp.where(hidden, -jnp.inf, st)
            s_ref[slot, hh] = st
            col_max.append(_col_max(st))
        return col_max

    def consume(first, slot, m_prev, m_cur, accs):
        ps = [jnp.exp(s_ref[slot, hh] - m_cur[hh]).astype(BF16) for hh in heads]
        accs = [jnp.exp(m_prev[hh] - m_cur[hh]) * accs[hh] for hh in heads]
        for b in range(KEY_GROUP):
            for hh in heads:
                accs[hh] = accs[hh] + jnp.dot(vt_ref[first + b, head_lanes[hh], :],
                                              ps[hh][b * MOBA_BLOCK:(b + 1) * MOBA_BLOCK, :],
                                              preferred_element_type=F32)
        return accs

    def step(first, slot, carry):
        pend, m_prev, m_cur, accs = carry
        cm = score(first, slot, causal=False)
        accs = consume(pend, 1 - slot, m_prev, m_cur, accs)
        return (jnp.asarray(first, jnp.int32), m_cur,
                [jnp.maximum(m_cur[hh], cm[hh]) for hh in heads], accs)

    zeros = [jnp.zeros((AUG, tq), F32) for _ in heads]

    def start_even(_):
        cm = score(own_first, 0, causal=True)
        return jnp.asarray(own_first, jnp.int32), cm, cm, zeros

    def start_odd(_):
        cm = score(own_first, 1, causal=True)
        return step(0, 0, (own_first, cm, cm, zeros))

    odd = n_past % 2
    carry = lax.cond(odd == 1, start_odd, start_even, 0)

    def pair(u, carry):
        g = (odd + 2 * u) * KEY_GROUP
        return step(g + KEY_GROUP, 0, step(g, 1, carry))

    pend, m_prev, m_cur, accs = lax.fori_loop(0, n_past // 2, pair, carry)
    accs = consume(pend, 0, m_prev, m_cur, accs)
    outs = []
    for hh in heads:
        ot = accs[hh][:ATTN_HEAD_DIM, :] / accs[hh][ATTN_HEAD_DIM:ATTN_HEAD_DIM + 1, :]
        outs.append(ot.T)
    o_ref[...] = jnp.concatenate(outs, axis=1).astype(o_ref.dtype)


def _moba(slope_tab, q, kmean, k_aug, vt_aug):
    s = q.shape[0]
    nb = s // MOBA_BLOCK
    n_steps = ATTN_HEADS // HEADS_PER_STEP
    return pl.pallas_call(
        _moba_kernel,
        grid=(n_steps, nb),
        in_specs=[pl.BlockSpec((1, HEADS_PER_STEP, LANES), lambda p, i: (p, 0, 0)),
                  pl.BlockSpec((MOBA_BLOCK, HEADS_PER_STEP * ATTN_HEAD_DIM), lambda p, i: (i, p)),
                  pl.BlockSpec((N_BLOCK_COLS, HEADS_PER_STEP * AUG), lambda p, i: (0, p)),
                  pl.BlockSpec((s, HEADS_PER_STEP * AUG), lambda p, i: (0, p)),
                  pl.BlockSpec((nb, HEADS_PER_STEP * AUG, MOBA_BLOCK), lambda p, i: (0, p, 0))],
        out_specs=pl.BlockSpec((MOBA_BLOCK, HEADS_PER_STEP * ATTN_HEAD_DIM), lambda p, i: (i, p)),
        out_shape=jax.ShapeDtypeStruct((s, ATTN_WIDTH), BF16),
        scratch_shapes=[pltpu.VMEM((2, HEADS_PER_STEP, KEY_GROUP * MOBA_BLOCK, MOBA_BLOCK), F32)],
        compiler_params=pltpu.CompilerParams(dimension_semantics=("arbitrary", "arbitrary"),
                                             vmem_limit_bytes=VMEM_LIMIT),
        name="moba",
    )(slope_tab, q, kmean, k_aug, vt_aug)


def _mlstm_kernel(qk_ref, vo_ref, gates_ref, conv_ref, gbias_ref, ng_ref, y_ref,
                  xbuf, c_ref, n_ref, m_ref):
    L = MLSTM_CHUNK
    d = MLSTM_HEAD_DIM
    halo = 8

    @pl.when(pl.program_id(0) == 0)
    def _():
        xbuf[0:halo, :] = jnp.zeros((halo, xbuf.shape[1]), F32)
        c_ref[...] = jnp.zeros(c_ref.shape, F32)
        n_ref[...] = jnp.zeros(n_ref.shape, F32)
        m_ref[...] = jnp.zeros(m_ref.shape, F32)

    xbuf[halo:halo + L, :] = qk_ref[...]
    w = conv_ref[...]
    y = xbuf[halo - CONV_WIDTH + 1:halo - CONV_WIDTH + 1 + L, :] * w[0:1, :]
    for j in range(1, CONV_WIDTH):
        off = halo - CONV_WIDTH + 1 + j
        y = y + xbuf[off:off + L, :] * w[j:j + 1, :]
    xbuf[0:halo, :] = xbuf[L:L + halo, :]
    qk = y * _sigmoid(y)

    a = gates_ref[...] + gbias_ref[...]
    lane = lax.broadcasted_iota(jnp.int32, (L, LANES), 1)
    is_f = jnp.logical_and(lane >= MLSTM_HEADS, lane < 2 * MLSTM_HEADS)
    log_f = jnp.where(is_f, jnp.minimum(a, 0.0) - jnp.log(1.0 + jnp.exp(-jnp.abs(a))), 0.0)
    t_io = lax.broadcasted_iota(jnp.int32, (L, L), 0)
    s_io = lax.broadcasted_iota(jnp.int32, (L, L), 1)
    causal = s_io <= t_io
    bcum = jnp.dot(causal.astype(F32), log_f, precision=lax.Precision.HIGHEST,
                   preferred_element_type=F32)
    colm = jnp.where(lane < MLSTM_HEADS, a, bcum)
    rowm = colm.T

    for hd in range(MLSTM_HEADS):
        hs = slice(hd * d, (hd + 1) * d)
        q = qk[:, hs]
        k = qk[:, MLSTM_WIDTH + hd * d:MLSTM_WIDTH + (hd + 1) * d] * (d ** -0.5)
        v = vo_ref[:, hs]
        og = vo_ref[:, MLSTM_WIDTH + hd * d:MLSTM_WIDTH + (hd + 1) * d]
        ic_c = colm[:, hd:hd + 1]
        bc_c = colm[:, MLSTM_HEADS + hd:MLSTM_HEADS + hd + 1]
        ic_r = rowm[hd:hd + 1, :]
        bc_r = rowm[MLSTM_HEADS + hd:MLSTM_HEADS + hd + 1, :]
        m_prev = m_ref[hd:hd + 1, 0:1]

        dmat = jnp.where(causal, bc_c - bc_r + ic_r, -jnp.inf)
        inter = bc_c + m_prev
        m_t = jnp.maximum(inter, jnp.max(dmat, axis=1, keepdims=True))
        w_intra = jnp.exp(dmat - m_t)
        w_inter = jnp.exp(inter - m_t)
        qb, kb, vb = q.astype(BF16), k.astype(BF16), v.astype(BF16)
        sc = lax.dot_general(qb, kb, NT_DIMS, preferred_element_type=F32) * w_intra
        c_st = c_ref[hd]
        n_st = n_ref[hd]
        num = (w_inter * jnp.dot(qb, c_st.astype(BF16), preferred_element_type=F32)
               + jnp.dot(sc.astype(BF16), vb, preferred_element_type=F32))
        den = (w_inter * jnp.sum(q * n_st, axis=1, keepdims=True)
               + jnp.sum(sc, axis=1, keepdims=True))
        h_out = num / jnp.maximum(jnp.abs(den), jnp.exp(-m_t))

        b_last = bc_c[L - 1:L, :]
        g_c = b_last - bc_c + ic_c
        m_new = jnp.maximum(b_last + m_prev, jnp.max(g_c, axis=0, keepdims=True))
        decay = jnp.exp(b_last + m_prev - m_new)
        kw = k * jnp.exp(g_c - m_new)
        c_ref[hd] = decay * c_st + jnp.dot(kw.T.astype(BF16), vb, preferred_element_type=F32)
        n_ref[hd] = decay * n_st + jnp.sum(kw, axis=0, keepdims=True)
        m_ref[hd:hd + 1, :] = jnp.broadcast_to(m_new, (1, LANES))

        hg = h_out * _sigmoid(og)
        mu = jnp.mean(hg, axis=1, keepdims=True)
        var = jnp.mean(jnp.square(hg - mu), axis=1, keepdims=True)
        y_ref[:, hs] = ((hg - mu) * lax.rsqrt(var + NORM_EPS) * ng_ref[:, hs]).astype(y_ref.dtype)


def _mlstm(qkm, vom, gates, conv, gbias, ng):
    s = qkm.shape[0]
    L = MLSTM_CHUNK
    row = lambda w: pl.BlockSpec((L, w), lambda t: (t, 0))
    whole = pl.BlockSpec(memory_space=pltpu.VMEM)
    return pl.pallas_call(
        _mlstm_kernel,
        grid=(s // L,),
        in_specs=[row(2 * MLSTM_WIDTH), row(2 * MLSTM_WIDTH), row(LANES), whole, whole, whole],
        out_specs=row(MLSTM_WIDTH),
        out_shape=jax.ShapeDtypeStruct((s, MLSTM_WIDTH), BF16),
        scratch_shapes=[pltpu.VMEM((L + 8, 2 * MLSTM_WIDTH), F32),
                        pltpu.VMEM((MLSTM_HEADS, MLSTM_HEAD_DIM, MLSTM_HEAD_DIM), F32),
                        pltpu.VMEM((MLSTM_HEADS, 1, MLSTM_HEAD_DIM), F32),
                        pltpu.VMEM((8, LANES), F32)],
        compiler_params=pltpu.CompilerParams(dimension_semantics=("arbitrary",),
                                             vmem_limit_bytes=VMEM_LIMIT),
        name="mlstm",
    )(qkm, vom, gates, conv, gbias, ng)


def _out_ffn_kernel(x_ref, ya_ref, ym_ref, gg_ref, woa_ref, wom_ref, wo_ref, fg_ref,
                    wgt_ref, wup_ref, wdn_ref, fin_ref, o_ref, *, final_norm):
    a = jnp.dot(ya_ref[...], woa_ref[...], preferred_element_type=F32)
    b = jnp.dot(ym_ref[...], wom_ref[...], preferred_element_type=F32)
    merged = _sigmoid(gg_ref[:, :D_MODEL]) * a + _sigmoid(gg_ref[:, D_MODEL:]) * b
    x1 = x_ref[...] + jnp.dot(merged.astype(BF16), wo_ref[...], preferred_element_type=F32)
    h2 = _rms(x1, fg_ref[...]).astype(BF16)
    acc = jnp.zeros(x1.shape, F32)
    for c in range(0, FFN_HIDDEN, FFN_CHUNK):
        gt = jnp.dot(h2, wgt_ref[:, c:c + FFN_CHUNK], preferred_element_type=F32)
        up = jnp.dot(h2, wup_ref[:, c:c + FFN_CHUNK], preferred_element_type=F32)
        act = (gt * _sigmoid(gt) * up).astype(BF16)
        acc = acc + jnp.dot(act, wdn_ref[c:c + FFN_CHUNK, :], preferred_element_type=F32)
    x2 = x1 + acc
    o_ref[...] = _rms(x2, fin_ref[...]) if final_norm else x2


def _out_ffn(x, ya, ym, gg, woa, wom, wo, fg, wgt, wup, wdn, fin, final_norm):
    s = x.shape[0]
    row = lambda w: pl.BlockSpec((OUT_TILE, w), lambda i: (i, 0))
    whole = pl.BlockSpec(memory_space=pltpu.VMEM)
    return pl.pallas_call(
        functools.partial(_out_ffn_kernel, final_norm=final_norm),
        grid=(s // OUT_TILE,),
        in_specs=[row(D_MODEL), row(ATTN_WIDTH), row(MLSTM_WIDTH), row(2 * D_MODEL)] + [whole] * 8,
        out_specs=row(D_MODEL),
        out_shape=jax.ShapeDtypeStruct((s, D_MODEL), F32),
        compiler_params=pltpu.CompilerParams(dimension_semantics=("arbitrary",),
                                             vmem_limit_bytes=VMEM_LIMIT),
        name="out_ffn",
    )(x, ya, ym, gg, woa, wom, wo, fg, wgt, wup, wdn, fin)


def _alibi_slopes():
    return np.exp2(-8.0 * np.arange(1, ATTN_HEADS + 1, dtype=np.float64) / ATTN_HEADS).astype(np.float32)


def _pad_heads(w):
    d = w.shape[0]
    w = w.reshape(d, ATTN_HEADS, ATTN_HEAD_DIM)
    w = jnp.pad(w, ((0, 0), (0, 0), (0, AUG - ATTN_HEAD_DIM)))
    return w.reshape(d, ATTN_AUG_WIDTH)


def kernel(x, mix_norm_g, w_in, conv_qk, b_igate, b_fgate, mlstm_norm_g, w_out_attn, w_out_mlstm,
           w_o, ffn_norm_g, w_ffn_gate, w_ffn_up, w_ffn_down, final_norm_g):
    batch, seq, _ = x.shape
    depth = w_in.shape[0]
    assert seq % OUT_TILE == 0 and seq // MOBA_BLOCK <= N_BLOCK_COLS
    assert seq % (KEY_GROUP * MOBA_BLOCK) == 0
    slopes = _alibi_slopes()
    alibi = np.zeros((1, ATTN_AUG_WIDTH), np.float32)
    alibi[0, AUG - 1::AUG] = slopes
    alibi = jnp.asarray(alibi)
    slope_tab = jnp.asarray(np.broadcast_to(
        slopes.reshape(ATTN_HEADS // HEADS_PER_STEP, HEADS_PER_STEP, 1),
        (ATTN_HEADS // HEADS_PER_STEP, HEADS_PER_STEP, LANES)).copy())
    a0, a1, a2 = ATTN_WIDTH, 2 * ATTN_WIDTH, 3 * ATTN_WIDTH
    m1 = a2 + 4 * MLSTM_WIDTH
    g0 = m1 + 2 * MLSTM_HEADS

    outs = []
    for bi in range(batch):
        xb = x[bi]
        for layer in range(depth):
            w = w_in[layer]
            wq = w[:, :a0].astype(BF16)
            wk = _pad_heads(w[:, a0:a1]).astype(BF16)
            wv = _pad_heads(w[:, a1:a2]).T.astype(BF16)
            wm = w[:, a2:m1].astype(BF16)
            wif = jnp.pad(w[:, m1:g0], ((0, 0), (0, LANES - 2 * MLSTM_HEADS))).astype(BF16)
            wg = w[:, g0:].astype(BF16)
            q, k_aug, v_aug, kmean, qkm, vom, gates, gg = _inproj(
                xb, mix_norm_g[layer][None, :], wq, wk, wv, wm, wif, wg, alibi)
            kmean = jnp.pad(kmean.reshape(kmean.shape[0], ATTN_AUG_WIDTH),
                            ((0, N_BLOCK_COLS - kmean.shape[0]), (0, 0)))
            ya = _moba(slope_tab, q, kmean, k_aug, v_aug)
            gbias = jnp.pad(jnp.concatenate([b_igate[layer], b_fgate[layer]])[None, :],
                            ((0, 0), (0, LANES - 2 * MLSTM_HEADS))).astype(F32)
            ym = _mlstm(qkm, vom, gates, conv_qk[layer], gbias, mlstm_norm_g[layer][None, :])
            xb = _out_ffn(xb, ya, ym, gg,
                          w_out_attn[layer].astype(BF16), w_out_mlstm[layer].astype(BF16),
                          w_o[layer].astype(BF16), ffn_norm_g[layer][None, :],
                          w_ffn_gate[layer].astype(BF16), w_ffn_up[layer].astype(BF16),
                          w_ffn_down[layer].astype(BF16), final_norm_g[None, :],
                          final_norm=(layer == depth - 1))
        outs.append(xb)
    return jnp.stack(outs, axis=0)
```

```python
import functools

import numpy as np
import jax
import jax.numpy as jnp
from jax import lax
from jax.experimental import pallas as pl
from jax.experimental.pallas import tpu as pltpu

D_MODEL = 1024
ATTN_HEADS = 8
ATTN_HEAD_DIM = 64
ATTN_WIDTH = ATTN_HEADS * ATTN_HEAD_DIM
MOBA_BLOCK = 256
MOBA_TOP_K = 3
MLSTM_HEADS = 4
MLSTM_HEAD_DIM = 128
MLSTM_WIDTH = MLSTM_HEADS * MLSTM_HEAD_DIM
MLSTM_CHUNK = 128
CONV_WIDTH = 4
FFN_HIDDEN = 2816
NORM_EPS = 1e-6

LANES = 128
AUG = 2 * ATTN_HEAD_DIM
ATTN_AUG_WIDTH = ATTN_HEADS * AUG
V_ROWS = 80
N_BLOCK_COLS = AUG - ATTN_HEAD_DIM
HEADS_PER_STEP = 2
KEY_GROUP = 4
NEG_BIAS = -1e9
FFN_CHUNK = 256
OUT_TILE = 512
VMEM_LIMIT = 56 * 1024 * 1024

F32 = jnp.float32
BF16 = jnp.bfloat16
NT_DIMS = (((1,), (1,)), ((), ()))


def _rms(x, g):
    return x * lax.rsqrt(jnp.mean(x * x, axis=-1, keepdims=True) + NORM_EPS) * g


def _sigmoid(x):
    return 1.0 / (1.0 + jnp.exp(-x))


def _inproj_kernel(x_ref, g_ref, wq_ref, wk_ref, wv_ref, wm_ref, wif_ref, wg_ref, alibi_ref,
                   q_ref, k_ref, v_ref, kmean_ref, qkm_ref, vom_ref, gates_ref, gg_ref):
    blk = pl.program_id(0)
    hb = _rms(x_ref[...], g_ref[...]).astype(BF16)
    q_ref[0] = lax.dot_general(wq_ref[...], hb, NT_DIMS, preferred_element_type=F32)
    kf = jnp.dot(hb, wk_ref[...], preferred_element_type=F32)
    kmean_ref[0] = jnp.mean(kf, axis=0, keepdims=True)
    lane = lax.broadcasted_iota(jnp.int32, (1, ATTN_AUG_WIDTH), 1) & (AUG - 1)
    row = lax.broadcasted_iota(jnp.int32, (MOBA_BLOCK, 1), 0).astype(F32)
    onehot = jnp.where(lane - ATTN_HEAD_DIM == blk, 1.0, 0.0)
    onehot = jnp.where(lane == AUG - 1, 0.0, onehot)
    k_ref[...] = (kf + onehot + alibi_ref[...] * row).astype(BF16)
    vt = lax.dot_general(wv_ref[...], hb, NT_DIMS, preferred_element_type=F32)
    feat = lax.rem(lax.broadcasted_iota(jnp.int32, (ATTN_HEADS * V_ROWS, 1), 0), V_ROWS)
    v_ref[0] = (vt + jnp.where(feat == ATTN_HEAD_DIM, 1.0, 0.0)).astype(BF16)
    m = jnp.dot(hb, wm_ref[...], preferred_element_type=F32)
    qkm_ref[...] = m[:, :2 * MLSTM_WIDTH]
    vom_ref[...] = m[:, 2 * MLSTM_WIDTH:]
    gates_ref[...] = jnp.dot(hb, wif_ref[...], preferred_element_type=F32)
    gg_ref[...] = jnp.dot(hb, wg_ref[...], preferred_element_type=F32)


def _inproj(x, g, wq, wk, wv, wm, wif, wg, alibi):
    s = x.shape[0]
    nb = s // MOBA_BLOCK
    row = lambda w: pl.BlockSpec((MOBA_BLOCK, w), lambda i: (i, 0))
    whole = pl.BlockSpec(memory_space=pltpu.VMEM)
    return pl.pallas_call(
        _inproj_kernel,
        grid=(nb,),
        in_specs=[row(D_MODEL), whole, whole, whole, whole, whole, whole, whole, whole],
        out_specs=[pl.BlockSpec((1, ATTN_WIDTH, MOBA_BLOCK), lambda i: (i, 0, 0)),
                   row(ATTN_AUG_WIDTH),
                   pl.BlockSpec((1, ATTN_HEADS * V_ROWS, MOBA_BLOCK), lambda i: (i, 0, 0)),
                   pl.BlockSpec((1, 1, ATTN_AUG_WIDTH), lambda i: (i, 0, 0)),
                   row(2 * MLSTM_WIDTH), row(2 * MLSTM_WIDTH), row(LANES), row(2 * D_MODEL)],
        out_shape=[jax.ShapeDtypeStruct((nb, ATTN_WIDTH, MOBA_BLOCK), F32),
                   jax.ShapeDtypeStruct((s, ATTN_AUG_WIDTH), BF16),
                   jax.ShapeDtypeStruct((nb, ATTN_HEADS * V_ROWS, MOBA_BLOCK), BF16),
                   jax.ShapeDtypeStruct((nb, 1, ATTN_AUG_WIDTH), F32),
                   jax.ShapeDtypeStruct((s, 2 * MLSTM_WIDTH), F32),
                   jax.ShapeDtypeStruct((s, 2 * MLSTM_WIDTH), F32),
                   jax.ShapeDtypeStruct((s, LANES), F32),
                   jax.ShapeDtypeStruct((s, 2 * D_MODEL), F32)],
        compiler_params=pltpu.CompilerParams(dimension_semantics=("arbitrary",),
                                             vmem_limit_bytes=VMEM_LIMIT),
        name="inproj",
    )(x, g, wq, wk, wv, wm, wif, wg, alibi)


def _col_max(st):
    rows, n = st.shape
    fan = 8
    while rows > fan * 8 and rows % fan == 0:
        st = jnp.max(st.reshape(fan, rows // fan, n), axis=0)
        rows //= fan
    return jnp.max(st, axis=0, keepdims=True)


def _moba_kernel(slope_ref, q_ref, kmean_ref, k_ref, vt_ref, o_ref, s_ref):
    i = pl.program_id(1)
    tq = q_ref.shape[2]
    nb = N_BLOCK_COLS
    group_rows = KEY_GROUP * MOBA_BLOCK
    blk = lax.broadcasted_iota(jnp.int32, (nb, tq), 0).astype(F32)
    i_f = i.astype(F32)
    head_lanes = [slice(hh * AUG, (hh + 1) * AUG) for hh in range(HEADS_PER_STEP)]
    head_rows = [slice(hh * V_ROWS, (hh + 1) * V_ROWS) for hh in range(HEADS_PER_STEP)]
    q_augs = []
    for hh in range(HEADS_PER_STEP):
        slope = slope_ref[0, hh:hh + 1, 0:1]
        qh = q_ref[0, hh * ATTN_HEAD_DIM:(hh + 1) * ATTN_HEAD_DIM, :]
        km = kmean_ref[:, hh * AUG:hh * AUG + ATTN_HEAD_DIM]
        gate = jnp.dot(km, qh, precision=lax.Precision.HIGHEST, preferred_element_type=F32)
        valid = blk < i_f
        g = jnp.where(valid, gate, -jnp.inf)
        sel = jnp.zeros((nb, tq), jnp.bool_)
        for _ in range(MOBA_TOP_K):
            mx = jnp.max(g, axis=0, keepdims=True)
            idx = jnp.min(jnp.where(g == mx, blk, float(nb)), axis=0, keepdims=True)
            pick = blk == idx
            sel = jnp.logical_or(sel, pick)
            g = jnp.where(pick, -jnp.inf, g)
        bias = jnp.where(sel, (blk - i_f) * (slope * float(MOBA_BLOCK)), NEG_BIAS)
        bias = jnp.where(valid, bias, 0.0)
        bias = jnp.where(blk == float(nb - 1), 1.0, bias)
        q_augs.append(jnp.concatenate([qh * (ATTN_HEAD_DIM ** -0.5), bias], axis=0).astype(BF16))

    heads = range(HEADS_PER_STEP)
    n_past = i // KEY_GROUP
    own_first = n_past * KEY_GROUP
    hidden = (lax.broadcasted_iota(jnp.int32, (group_rows, tq), 0)
              - lax.broadcasted_iota(jnp.int32, (group_rows, tq), 1)) > (i - own_first) * MOBA_BLOCK

    def score(first, slot, causal):
        start = pl.multiple_of(first * MOBA_BLOCK, group_rows)
        col_max = []
        for hh in heads:
            st = jnp.dot(k_ref[pl.ds(start, group_rows), head_lanes[hh]], q_augs[hh],
                         preferred_element_type=F32)
            if causal:
                st = jnp.where(hidden, -jnp.inf, st)
            s_ref[slot, hh] = st
            col_max.append(_col_max(st))
        return col_max

    def consume(first, slot, m_prev, m_cur, accs):
        ps = [jnp.exp(s_ref[slot, hh] - m_cur[hh]).astype(BF16) for hh in heads]
        accs = [jnp.exp(m_prev[hh] - m_cur[hh]) * accs[hh] for hh in heads]
        for b in range(KEY_GROUP):
            for hh in heads:
                accs[hh] = accs[hh] + jnp.dot(vt_ref[first + b, head_rows[hh], :],
                                              ps[hh][b * MOBA_BLOCK:(b + 1) * MOBA_BLOCK, :],
                                              preferred_element_type=F32)
        return accs

    def step(first, slot, carry):
        pend, m_prev, m_cur, accs = carry
        cm = score(first, slot, causal=False)
        accs = consume(pend, 1 - slot, m_prev, m_cur, accs)
        return (jnp.asarray(first, jnp.int32), m_cur,
                [jnp.maximum(m_cur[hh], cm[hh]) for hh in heads], accs)

    zeros = [jnp.zeros((V_ROWS, tq), F32) for _ in heads]

    def start_even(_):
        cm = score(own_first, 0, causal=True)
        return jnp.asarray(own_first, jnp.int32), cm, cm, zeros

    def start_odd(_):
        cm = score(own_first, 1, causal=True)
        return step(0, 0, (own_first, cm, cm, zeros))

    odd = n_past % 2
    carry = lax.cond(odd == 1, start_odd, start_even, 0)

    def pair(u, carry):
        g = (odd + 2 * u) * KEY_GROUP
        return step(g + KEY_GROUP, 0, step(g, 1, carry))

    pend, m_prev, m_cur, accs = lax.fori_loop(0, n_past // 2, pair, carry)
    accs = consume(pend, 0, m_prev, m_cur, accs)
    outs = []
    for hh in heads:
        ot = accs[hh][:ATTN_HEAD_DIM, :] / accs[hh][ATTN_HEAD_DIM:ATTN_HEAD_DIM + 1, :]
        outs.append(ot.T)
    o_ref[...] = jnp.concatenate(outs, axis=1).astype(o_ref.dtype)


def _moba(slope_tab, qt, kmean, k_aug, vt_aug):
    s = k_aug.shape[0]
    nb = s // MOBA_BLOCK
    n_steps = ATTN_HEADS // HEADS_PER_STEP
    return pl.pallas_call(
        _moba_kernel,
        grid=(n_steps, nb),
        in_specs=[pl.BlockSpec((1, HEADS_PER_STEP, LANES), lambda p, i: (p, 0, 0)),
                  pl.BlockSpec((1, HEADS_PER_STEP * ATTN_HEAD_DIM, MOBA_BLOCK), lambda p, i: (i, p, 0)),
                  pl.BlockSpec((N_BLOCK_COLS, HEADS_PER_STEP * AUG), lambda p, i: (0, p)),
                  pl.BlockSpec((s, HEADS_PER_STEP * AUG), lambda p, i: (0, p)),
                  pl.BlockSpec((nb, HEADS_PER_STEP * V_ROWS, MOBA_BLOCK), lambda p, i: (0, p, 0))],
        out_specs=pl.BlockSpec((MOBA_BLOCK, HEADS_PER_STEP * ATTN_HEAD_DIM), lambda p, i: (i, p)),
        out_shape=jax.ShapeDtypeStruct((s, ATTN_WIDTH), BF16),
        scratch_shapes=[pltpu.VMEM((2, HEADS_PER_STEP, KEY_GROUP * MOBA_BLOCK, MOBA_BLOCK), F32)],
        compiler_params=pltpu.CompilerParams(dimension_semantics=("arbitrary", "arbitrary"),
                                             vmem_limit_bytes=VMEM_LIMIT),
        name="moba",
    )(slope_tab, qt, kmean, k_aug, vt_aug)


def _mlstm_kernel(qk_ref, vo_ref, gates_ref, conv_ref, gbias_ref, ng_ref, y_ref,
                  xbuf, c_ref, n_ref, m_ref):
    L = MLSTM_CHUNK
    d = MLSTM_HEAD_DIM
    halo = 8

    @pl.when(pl.program_id(0) == 0)
    def _():
        xbuf[0:halo, :] = jnp.zeros((halo, xbuf.shape[1]), F32)
        c_ref[...] = jnp.zeros(c_ref.shape, F32)
        n_ref[...] = jnp.zeros(n_ref.shape, F32)
        m_ref[...] = jnp.zeros(m_ref.shape, F32)

    xbuf[halo:halo + L, :] = qk_ref[...]
    w = conv_ref[...]
    y = xbuf[halo - CONV_WIDTH + 1:halo - CONV_WIDTH + 1 + L, :] * w[0:1, :]
    for j in range(1, CONV_WIDTH):
        off = halo - CONV_WIDTH + 1 + j
        y = y + xbuf[off:off + L, :] * w[j:j + 1, :]
    xbuf[0:halo, :] = xbuf[L:L + halo, :]
    qk = y * _sigmoid(y)

    a = gates_ref[...] + gbias_ref[...]
    lane = lax.broadcasted_iota(jnp.int32, (L, LANES), 1)
    is_f = jnp.logical_and(lane >= MLSTM_HEADS, lane < 2 * MLSTM_HEADS)
    log_f = jnp.where(is_f, jnp.minimum(a, 0.0) - jnp.log(1.0 + jnp.exp(-jnp.abs(a))), 0.0)
    t_io = lax.broadcasted_iota(jnp.int32, (L, L), 0)
    s_io = lax.broadcasted_iota(jnp.int32, (L, L), 1)
    causal = s_io <= t_io
    bcum = jnp.dot(causal.astype(F32), log_f, precision=lax.Precision.HIGHEST,
                   preferred_element_type=F32)
    colm = jnp.where(lane < MLSTM_HEADS, a, bcum)
    rowm = colm.T

    for hd in range(MLSTM_HEADS):
        hs = slice(hd * d, (hd + 1) * d)
        q = qk[:, hs]
        k = qk[:, MLSTM_WIDTH + hd * d:MLSTM_WIDTH + (hd + 1) * d] * (d ** -0.5)
        v = vo_ref[:, hs]
        og = vo_ref[:, MLSTM_WIDTH + hd * d:MLSTM_WIDTH + (hd + 1) * d]
        ic_c = colm[:, hd:hd + 1]
        bc_c = colm[:, MLSTM_HEADS + hd:MLSTM_HEADS + hd + 1]
        ic_r = rowm[hd:hd + 1, :]
        bc_r = rowm[MLSTM_HEADS + hd:MLSTM_HEADS + hd + 1, :]
        m_prev = m_ref[hd:hd + 1, 0:1]

        dmat = jnp.where(causal, bc_c - bc_r + ic_r, -jnp.inf)
        inter = bc_c + m_prev
        m_t = jnp.maximum(inter, jnp.max(dmat, axis=1, keepdims=True))
        w_intra = jnp.exp(dmat - m_t)
        w_inter = jnp.exp(inter - m_t)
        qb, kb, vb = q.astype(BF16), k.astype(BF16), v.astype(BF16)
        sc = lax.dot_general(qb, kb, NT_DIMS, preferred_element_type=F32) * w_intra
        c_st = c_ref[hd]
        n_st = n_ref[hd]
        num = (w_inter * jnp.dot(qb, c_st.astype(BF16), preferred_element_type=F32)
               + jnp.dot(sc.astype(BF16), vb, preferred_element_type=F32))
        den = (w_inter * jnp.sum(q * n_st, axis=1, keepdims=True)
               + jnp.sum(sc, axis=1, keepdims=True))
        h_out = num / jnp.maximum(jnp.abs(den), jnp.exp(-m_t))

        b_last = bc_c[L - 1:L, :]
        g_c = b_last - bc_c + ic_c
        m_new = jnp.maximum(b_last + m_prev, jnp.max(g_c, axis=0, keepdims=True))
        decay = jnp.exp(b_last + m_prev - m_new)
        kw = k * jnp.exp(g_c - m_new)
        c_ref[hd] = decay * c_st + jnp.dot(kw.T.astype(BF16), vb, preferred_element_type=F32)
        n_ref[hd] = decay * n_st + jnp.sum(kw, axis=0, keepdims=True)
        m_ref[hd:hd + 1, :] = jnp.broadcast_to(m_new, (1, LANES))

        hg = h_out * _sigmoid(og)
        mu = jnp.mean(hg, axis=1, keepdims=True)
        var = jnp.mean(jnp.square(hg - mu), axis=1, keepdims=True)
        y_ref[:, hs] = ((hg - mu) * lax.rsqrt(var + NORM_EPS) * ng_ref[:, hs]).astype(y_ref.dtype)


def _mlstm(qkm, vom, gates, conv, gbias, ng):
    s = qkm.shape[0]
    L = MLSTM_CHUNK
    row = lambda w: pl.BlockSpec((L, w), lambda t: (t, 0))
    whole = pl.BlockSpec(memory_space=pltpu.VMEM)
    return pl.pallas_call(
        _mlstm_kernel,
        grid=(s // L,),
        in_specs=[row(2 * MLSTM_WIDTH), row(2 * MLSTM_WIDTH), row(LANES), whole, whole, whole],
        out_specs=row(MLSTM_WIDTH),
        out_shape=jax.ShapeDtypeStruct((s, MLSTM_WIDTH), BF16),
        scratch_shapes=[pltpu.VMEM((L + 8, 2 * MLSTM_WIDTH), F32),
                        pltpu.VMEM((MLSTM_HEADS, MLSTM_HEAD_DIM, MLSTM_HEAD_DIM), F32),
                        pltpu.VMEM((MLSTM_HEADS, 1, MLSTM_HEAD_DIM), F32),
                        pltpu.VMEM((8, LANES), F32)],
        compiler_params=pltpu.CompilerParams(dimension_semantics=("arbitrary",),
                                             vmem_limit_bytes=VMEM_LIMIT),
        name="mlstm",
    )(qkm, vom, gates, conv, gbias, ng)


def _out_ffn_kernel(x_ref, ya_ref, ym_ref, gg_ref, woa_ref, wom_ref, wo_ref, fg_ref,
                    wgt_ref, wup_ref, wdn_ref, fin_ref, o_ref, *, final_norm):
    a = jnp.dot(ya_ref[...], woa_ref[...], preferred_element_type=F32)
    b = jnp.dot(ym_ref[...], wom_ref[...], preferred_element_type=F32)
    merged = _sigmoid(gg_ref[:, :D_MODEL]) * a + _sigmoid(gg_ref[:, D_MODEL:]) * b
    x1 = x_ref[...] + jnp.dot(merged.astype(BF16), wo_ref[...], preferred_element_type=F32)
    h2 = _rms(x1, fg_ref[...]).astype(BF16)
    acc = jnp.zeros(x1.shape, F32)
    for c in range(0, FFN_HIDDEN, FFN_CHUNK):
        gt = jnp.dot(h2, wgt_ref[:, c:c + FFN_CHUNK], preferred_element_type=F32)
        up = jnp.dot(h2, wup_ref[:, c:c + FFN_CHUNK], preferred_element_type=F32)
        act = (gt * _sigmoid(gt) * up).astype(BF16)
        acc = acc + jnp.dot(act, wdn_ref[c:c + FFN_CHUNK, :], preferred_element_type=F32)
    x2 = x1 + acc
    o_ref[...] = _rms(x2, fin_ref[...]) if final_norm else x2


def _out_ffn(x, ya, ym, gg, woa, wom, wo, fg, wgt, wup, wdn, fin, final_norm):
    s = x.shape[0]
    row = lambda w: pl.BlockSpec((OUT_TILE, w), lambda i: (i, 0))
    whole = pl.BlockSpec(memory_space=pltpu.VMEM)
    return pl.pallas_call(
        functools.partial(_out_ffn_kernel, final_norm=final_norm),
        grid=(s // OUT_TILE,),
        in_specs=[row(D_MODEL), row(ATTN_WIDTH), row(MLSTM_WIDTH), row(2 * D_MODEL)] + [whole] * 8,
        out_specs=row(D_MODEL),
        out_shape=jax.ShapeDtypeStruct((s, D_MODEL), F32),
        compiler_params=pltpu.CompilerParams(dimension_semantics=("arbitrary",),
                                             vmem_limit_bytes=VMEM_LIMIT),
        name="out_ffn",
    )(x, ya, ym, gg, woa, wom, wo, fg, wgt, wup, wdn, fin)


def _alibi_slopes():
    return np.exp2(-8.0 * np.arange(1, ATTN_HEADS + 1, dtype=np.float64) / ATTN_HEADS).astype(np.float32)


def _pad_heads(w, width):
    d = w.shape[0]
    w = w.reshape(d, ATTN_HEADS, ATTN_HEAD_DIM)
    w = jnp.pad(w, ((0, 0), (0, 0), (0, width - ATTN_HEAD_DIM)))
    return w.reshape(d, ATTN_HEADS * width)


def kernel(x, mix_norm_g, w_in, conv_qk, b_igate, b_fgate, mlstm_norm_g, w_out_attn, w_out_mlstm,
           w_o, ffn_norm_g, w_ffn_gate, w_ffn_up, w_ffn_down, final_norm_g):
    batch, seq, _ = x.shape
    depth = w_in.shape[0]
    assert seq % OUT_TILE == 0 and seq // MOBA_BLOCK <= N_BLOCK_COLS
    assert seq % (KEY_GROUP * MOBA_BLOCK) == 0
    slopes = _alibi_slopes()
    alibi = np.zeros((1, ATTN_AUG_WIDTH), np.float32)
    alibi[0, AUG - 1::AUG] = slopes
    alibi = jnp.asarray(alibi)
    slope_tab = jnp.asarray(np.broadcast_to(
        slopes.reshape(ATTN_HEADS // HEADS_PER_STEP, HEADS_PER_STEP, 1),
        (ATTN_HEADS // HEADS_PER_STEP, HEADS_PER_STEP, LANES)).copy())
    a0, a1, a2 = ATTN_WIDTH, 2 * ATTN_WIDTH, 3 * ATTN_WIDTH
    m1 = a2 + 4 * MLSTM_WIDTH
    g0 = m1 + 2 * MLSTM_HEADS

    outs = []
    for bi in range(batch):
        xb = x[bi]
        for layer in range(depth):
            w = w_in[layer]
            wq = w[:, :a0].T.astype(BF16)
            wk = _pad_heads(w[:, a0:a1], AUG).astype(BF16)
            wv = _pad_heads(w[:, a1:a2], V_ROWS).T.astype(BF16)
            wm = w[:, a2:m1].astype(BF16)
            wif = jnp.pad(w[:, m1:g0], ((0, 0), (0, LANES - 2 * MLSTM_HEADS))).astype(BF16)
            wg = w[:, g0:].astype(BF16)
            q, k_aug, v_aug, kmean, qkm, vom, gates, gg = _inproj(
                xb, mix_norm_g[layer][None, :], wq, wk, wv, wm, wif, wg, alibi)
            kmean = jnp.pad(kmean.reshape(kmean.shape[0], ATTN_AUG_WIDTH),
                            ((0, N_BLOCK_COLS - kmean.shape[0]), (0, 0)))
            ya = _moba(slope_tab, q, kmean, k_aug, v_aug)
            gbias = jnp.pad(jnp.concatenate([b_igate[layer], b_fgate[layer]])[None, :],
                            ((0, 0), (0, LANES - 2 * MLSTM_HEADS))).astype(F32)
            ym = _mlstm(qkm, vom, gates, conv_qk[layer], gbias, mlstm_norm_g[layer][None, :])
            xb = _out_ffn(xb, ya, ym, gg,
                          w_out_attn[layer].astype(BF16), w_out_mlstm[layer].astype(BF16),
                          w_o[layer].astype(BF16), ffn_norm_g[layer][None, :],
                          w_ffn_gate[layer].astype(BF16), w_ffn_up[layer].astype(BF16),
                          w_ffn_down[layer].astype(BF16), final_norm_g[None, :],
                          final_norm=(layer == depth - 1))
        outs.append(xb)
    return jnp.stack(outs, axis=0)
```

```python
import functools

import numpy as np
import jax
import jax.numpy as jnp
from jax import lax
from jax.experimental import pallas as pl
from jax.experimental.pallas import tpu as pltpu
from jax.experimental.pallas import tpu_sc as plsc

D_MODEL = 1024
ATTN_HEADS = 8
ATTN_HEAD_DIM = 64
ATTN_WIDTH = ATTN_HEADS * ATTN_HEAD_DIM
MOBA_BLOCK = 256
MOBA_TOP_K = 3
MLSTM_HEADS = 4
MLSTM_HEAD_DIM = 128
MLSTM_WIDTH = MLSTM_HEADS * MLSTM_HEAD_DIM
MLSTM_CHUNK = 128
CONV_WIDTH = 4
FFN_HIDDEN = 2816
NORM_EPS = 1e-6

LANES = 128
AUG = 2 * ATTN_HEAD_DIM
ATTN_AUG_WIDTH = ATTN_HEADS * AUG
V_ROWS = 80
PART_W = 128
HEADS_PER_STEP = 2
TILE_GROUP = 4
GROUP_ROWS = TILE_GROUP * MOBA_BLOCK
SC_WINDOW = 128
FFN_CHUNK = 256
OUT_TILE = 512
VMEM_LIMIT = 56 * 1024 * 1024

F32 = jnp.float32
BF16 = jnp.bfloat16
I32 = jnp.int32
NT_DIMS = (((1,), (1,)), ((), ()))


def _rms(x, g):
    return x * lax.rsqrt(jnp.mean(x * x, axis=-1, keepdims=True) + NORM_EPS) * g


def _sigmoid(x):
    return 1.0 / (1.0 + jnp.exp(-x))


def _inproj_kernel(x_ref, g_ref, wq_ref, wk_ref, wv_ref, wm_ref, wif_ref, wg_ref, alibi_ref,
                   q_ref, k_ref, v_ref, kmean_ref, qkm_ref, vom_ref, gates_ref, gg_ref):
    blk = pl.program_id(0)
    hb = _rms(x_ref[...], g_ref[...]).astype(BF16)
    q_ref[0] = lax.dot_general(wq_ref[...], hb, NT_DIMS, preferred_element_type=F32)
    kf = jnp.dot(hb, wk_ref[...], preferred_element_type=F32)
    kmean_ref[0] = jnp.mean(kf, axis=0, keepdims=True)
    row = lax.broadcasted_iota(jnp.int32, (MOBA_BLOCK, 1), 0).astype(F32)
    start = (blk * MOBA_BLOCK).astype(F32)
    k_ref[...] = (kf + alibi_ref[0:1, :] * start + alibi_ref[1:2, :] * row).astype(BF16)
    vt = lax.dot_general(wv_ref[...], hb, NT_DIMS, preferred_element_type=F32)
    feat = lax.rem(lax.broadcasted_iota(jnp.int32, (ATTN_HEADS * V_ROWS, 1), 0), V_ROWS)
    v_ref[0] = (vt + jnp.where(feat == ATTN_HEAD_DIM, 1.0, 0.0)).astype(BF16)
    m = jnp.dot(hb, wm_ref[...], preferred_element_type=F32)
    qkm_ref[...] = m[:, :2 * MLSTM_WIDTH]
    vom_ref[...] = m[:, 2 * MLSTM_WIDTH:]
    gates_ref[...] = jnp.dot(hb, wif_ref[...], preferred_element_type=F32)
    gg_ref[...] = jnp.dot(hb, wg_ref[...], preferred_element_type=F32)


def _inproj(x, g, wq, wk, wv, wm, wif, wg, alibi):
    s = x.shape[0]
    nb = s // MOBA_BLOCK
    row = lambda w: pl.BlockSpec((MOBA_BLOCK, w), lambda i: (i, 0))
    whole = pl.BlockSpec(memory_space=pltpu.VMEM)
    return pl.pallas_call(
        _inproj_kernel,
        grid=(nb,),
        in_specs=[row(D_MODEL), whole, whole, whole, whole, whole, whole, whole, whole],
        out_specs=[pl.BlockSpec((1, ATTN_WIDTH, MOBA_BLOCK), lambda i: (i, 0, 0)),
                   row(ATTN_AUG_WIDTH),
                   pl.BlockSpec((1, ATTN_HEADS * V_ROWS, MOBA_BLOCK), lambda i: (i, 0, 0)),
                   pl.BlockSpec((1, 1, ATTN_AUG_WIDTH), lambda i: (i, 0, 0)),
                   row(2 * MLSTM_WIDTH), row(2 * MLSTM_WIDTH), row(LANES), row(2 * D_MODEL)],
        out_shape=[jax.ShapeDtypeStruct((nb, ATTN_WIDTH, MOBA_BLOCK), F32),
                   jax.ShapeDtypeStruct((s, ATTN_AUG_WIDTH), BF16),
                   jax.ShapeDtypeStruct((nb, ATTN_HEADS * V_ROWS, MOBA_BLOCK), BF16),
                   jax.ShapeDtypeStruct((nb, 1, ATTN_AUG_WIDTH), F32),
                   jax.ShapeDtypeStruct((s, 2 * MLSTM_WIDTH), F32),
                   jax.ShapeDtypeStruct((s, 2 * MLSTM_WIDTH), F32),
                   jax.ShapeDtypeStruct((s, LANES), F32),
                   jax.ShapeDtypeStruct((s, 2 * D_MODEL), F32)],
        compiler_params=pltpu.CompilerParams(dimension_semantics=("arbitrary",),
                                             vmem_limit_bytes=VMEM_LIMIT),
        name="inproj",
    )(x, g, wq, wk, wv, wm, wif, wg, alibi)


def _bucket_base_tiles(nb):
    sizes = np.array([-(-(nb - 1 - j) // TILE_GROUP) * TILE_GROUP for j in range(nb)], np.int64)
    return np.concatenate([[0], np.cumsum(sizes)[:-1]]), int(sizes.sum())


def _scores_t(k, q_rows):
    return lax.dot_general(k, q_rows.astype(BF16), NT_DIMS, preferred_element_type=F32)


def _select_kernel(base_ref, q_ref, kmean_ref, rows_ref, sidx_ref, gidx_ref, cnt_ref, run_ref,
                   *, nb, cap_rows, trash_row):
    p = pl.program_id(0)
    i = pl.program_id(1)
    tq = q_ref.shape[2]
    nbk = kmean_ref.shape[0]
    blk = lax.broadcasted_iota(jnp.int32, (nbk, tq), 0).astype(F32)
    i_f = i.astype(F32)
    qpos = lax.broadcasted_iota(jnp.int32, (1, tq), 1).astype(F32)
    earlier_q = jnp.where(lax.broadcasted_iota(jnp.int32, (tq, tq), 0)
                          < lax.broadcasted_iota(jnp.int32, (tq, tq), 1), 1.0, 0.0).astype(BF16)
    earlier_b = jnp.where(lax.broadcasted_iota(jnp.int32, (nbk, nbk), 1)
                          < lax.broadcasted_iota(jnp.int32, (nbk, nbk), 0), 1.0, 0.0).astype(BF16)
    all_q = jnp.ones((tq, tq), BF16)

    @pl.when(i == 0)
    def _():
        run_ref[...] = jnp.zeros(run_ref.shape, F32)

    for hh in range(HEADS_PER_STEP):
        head_f = (p * HEADS_PER_STEP + hh).astype(F32)
        qh = q_ref[0, hh * ATTN_HEAD_DIM:(hh + 1) * ATTN_HEAD_DIM, :]
        km = kmean_ref[:, hh * AUG:hh * AUG + ATTN_HEAD_DIM]
        gate = jnp.dot(km, qh, precision=lax.Precision.HIGHEST, preferred_element_type=F32)
        valid = blk < i_f
        g = jnp.where(valid, gate, -jnp.inf)
        sel = jnp.zeros((nbk, tq), jnp.bool_)
        for _ in range(MOBA_TOP_K):
            mx = jnp.max(g, axis=0, keepdims=True)
            idx = jnp.min(jnp.where(g == mx, blk, float(nbk)), axis=0, keepdims=True)
            pick = blk == idx
            sel = jnp.logical_or(sel, pick)
            g = jnp.where(pick, -jnp.inf, g)
        sel = jnp.logical_and(sel, valid)
        sel_f = jnp.where(sel, 1.0, 0.0)
        sel_b = sel_f.astype(BF16)
        rank = jnp.dot(sel_b, earlier_q, preferred_element_type=F32)
        tile_cnt = jnp.dot(sel_b, all_q, preferred_element_type=F32)
        ordinal = jnp.dot(earlier_b, sel_b, preferred_element_type=F32)
        run = run_ref[hh]
        slot = head_f * float(cap_rows) + base_ref[...] + run + rank
        run_ref[hh] = run + tile_cnt
        cnt_ref[hh] = (run + tile_cnt)[:, :LANES]
        srows, grows = [], []
        for r in range(MOBA_TOP_K):
            mine = jnp.logical_and(sel, ordinal == float(r))
            has = jnp.sum(jnp.where(mine, 1.0, 0.0), axis=0, keepdims=True) > 0.5
            s_r = jnp.sum(jnp.where(mine, slot, 0.0), axis=0, keepdims=True)
            i_dump = jnp.minimum(i_f, float(MOBA_TOP_K - 1))
            dump = float(trash_row) + ((head_f * MOBA_TOP_K + i_dump) * MOBA_TOP_K + r) * float(tq) + qpos
            srows.append(jnp.where(has, s_r, dump))
            grows.append(jnp.where(has, s_r, 0.0))
        pad = jnp.zeros((8 - MOBA_TOP_K, tq), F32)
        sidx_ref[hh, 0] = jnp.concatenate(srows + [pad + float(trash_row)], axis=0).astype(I32)
        gidx_ref[hh, 0] = jnp.concatenate(grows + [pad], axis=0).astype(I32)
        q_pad = jnp.concatenate([qh * (ATTN_HEAD_DIM ** -0.5), jnp.zeros((AUG - ATTN_HEAD_DIM, tq), F32)], axis=0)
        feat = lax.broadcasted_iota(jnp.int32, (AUG, tq), 0)
        ones_rows = jnp.logical_or(feat == ATTN_HEAD_DIM, feat == ATTN_HEAD_DIM + 1)
        rows_ref[hh] = jnp.where(ones_rows, 1.0, q_pad).T


def _moba_select(qt, kmean, nb):
    base_tiles, cap_tiles = _bucket_base_tiles(nb)
    nbk = kmean.shape[0]
    base = np.zeros((nbk, MOBA_BLOCK), np.float32)
    base[:nb, :] = (base_tiles * MOBA_BLOCK)[:, None]
    cap_rows = cap_tiles * MOBA_BLOCK
    trash_row = ATTN_HEADS * cap_rows
    s = nb * MOBA_BLOCK
    n_steps = ATTN_HEADS // HEADS_PER_STEP
    hp = HEADS_PER_STEP
    return pl.pallas_call(
        functools.partial(_select_kernel, nb=nb, cap_rows=cap_rows, trash_row=trash_row),
        grid=(n_steps, nb),
        in_specs=[pl.BlockSpec((nbk, MOBA_BLOCK), lambda p, i: (0, 0)),
                  pl.BlockSpec((1, hp * ATTN_HEAD_DIM, MOBA_BLOCK), lambda p, i: (i, p, 0)),
                  pl.BlockSpec((nbk, hp * AUG), lambda p, i: (0, p))],
        out_specs=[pl.BlockSpec((hp, MOBA_BLOCK, AUG), lambda p, i: (p, i, 0)),
                   pl.BlockSpec((hp, 1, 8, MOBA_BLOCK), lambda p, i: (p, i, 0, 0)),
                   pl.BlockSpec((hp, 1, 8, MOBA_BLOCK), lambda p, i: (p, i, 0, 0)),
                   pl.BlockSpec((hp, nbk, LANES), lambda p, i: (p, 0, 0))],
        out_shape=[jax.ShapeDtypeStruct((ATTN_HEADS, s, AUG), F32),
                   jax.ShapeDtypeStruct((ATTN_HEADS, nb, 8, MOBA_BLOCK), I32),
                   jax.ShapeDtypeStruct((ATTN_HEADS, nb, 8, MOBA_BLOCK), I32),
                   jax.ShapeDtypeStruct((ATTN_HEADS, nbk, LANES), F32)],
        scratch_shapes=[pltpu.VMEM((hp, nbk, MOBA_BLOCK), F32)],
        compiler_params=pltpu.CompilerParams(dimension_semantics=("arbitrary", "arbitrary"),
                                             vmem_limit_bytes=VMEM_LIMIT),
        name="moba_select",
    )(jnp.asarray(base), qt, kmean)


def _sc_scatter_rows(rows, idx, n_slots):
    n, d = rows.shape
    reps = idx.shape[0]
    blocks = n // SC_WINDOW
    mesh = plsc.VectorSubcoreMesh(core_axis_name="c", subcore_axis_name="s")

    @functools.partial(pl.kernel, out_type=jax.ShapeDtypeStruct((n_slots, d), rows.dtype), mesh=mesh,
                       scratch_types=[], name="moba_scatter")
    def scatter(x_hbm, i_hbm, o_hbm):
        def body(x_vmem, i_vmem):
            pltpu.sync_copy(x_vmem, o_hbm.at[i_vmem.at[0]])

        pltpu.emit_pipeline(
            body, grid=(reps * blocks,),
            in_specs=[pl.BlockSpec((SC_WINDOW, d), index_map=lambda b: (b % blocks, 0)),
                      pl.BlockSpec((1, SC_WINDOW), index_map=lambda b: (0, b))],
            out_specs=[], core_axis_name=("c", "s"), dimension_semantics=(pltpu.PARALLEL,),
        )(x_hbm, i_hbm)

    return scatter(rows, idx.reshape(1, reps * n))


def _sc_gather_rows(table, idx):
    m = idx.shape[0]
    d = table.shape[1]
    mesh = plsc.VectorSubcoreMesh(core_axis_name="c", subcore_axis_name="s")

    @functools.partial(pl.kernel, out_type=jax.ShapeDtypeStruct((m, d), table.dtype), mesh=mesh,
                       name="moba_gather")
    def gather(x_hbm, i_hbm, o_hbm):
        def body(i_vmem, o_vmem):
            pltpu.sync_copy(x_hbm.at[i_vmem.at[0]], o_vmem)

        pltpu.emit_pipeline(
            body, grid=(m // SC_WINDOW,),
            in_specs=[pl.BlockSpec((1, SC_WINDOW), index_map=lambda b: (0, b))],
            out_specs=[pl.BlockSpec((SC_WINDOW, d), index_map=lambda b: (b, 0))],
            core_axis_name=("c", "s"), dimension_semantics=(pltpu.PARALLEL,),
        )(i_hbm, o_hbm)

    return gather(table, idx.reshape(1, m))


def _partial_rows(acc, m):
    n = acc.shape[1]
    t = jnp.concatenate([acc, jnp.zeros((LANES - V_ROWS, n), F32)], axis=0)
    row = lax.broadcasted_iota(jnp.int32, (LANES, n), 0)
    return jnp.where(row == ATTN_HEAD_DIM + 1, m, t).T[:, :PART_W]


def _blocks_kernel(grp_ref, blk_ref, head_ref, nvalid_ref, q_ref, k_ref, vt_ref, o_ref):
    nv = nvalid_ref[pl.program_id(0)]

    def run(width):
        row_id = lax.broadcasted_iota(jnp.int32, (width, AUG), 0)
        q = jnp.where(row_id < nv, q_ref[0:width, :], 0.0)
        st = _scores_t(k_ref[...], q)
        m = jnp.max(st, axis=0, keepdims=True)
        p = jnp.exp(st - m).astype(BF16)
        acc = jnp.dot(vt_ref[0], p, preferred_element_type=F32)
        o_ref[0:width, :] = _partial_rows(acc, m)
        if width < GROUP_ROWS:
            o_ref[width:, :] = jnp.zeros((GROUP_ROWS - width, PART_W), F32)

    widths = [MOBA_BLOCK << e for e in range(TILE_GROUP.bit_length())]
    lo = 0
    for width in widths:
        pl.when(jnp.logical_and(nv > lo, nv <= width))(functools.partial(run, width))
        lo = width

    @pl.when(nv == 0)
    def _():
        o_ref[...] = jnp.zeros(o_ref.shape, F32)


def _moba_blocks(work, table, k_aug, vt_aug, n_slots):
    grp, blk, head, nvalid = work
    return pl.pallas_call(
        _blocks_kernel,
        grid_spec=pltpu.PrefetchScalarGridSpec(
            num_scalar_prefetch=4, grid=(grp.shape[0],),
            in_specs=[pl.BlockSpec((GROUP_ROWS, AUG), lambda w, g, b, h, nv: (g[w], 0)),
                      pl.BlockSpec((MOBA_BLOCK, AUG), lambda w, g, b, h, nv: (b[w], h[w])),
                      pl.BlockSpec((1, V_ROWS, MOBA_BLOCK), lambda w, g, b, h, nv: (b[w], h[w], 0))],
            out_specs=pl.BlockSpec((GROUP_ROWS, PART_W), lambda w, g, b, h, nv: (g[w], 0))),
        out_shape=jax.ShapeDtypeStruct((n_slots, PART_W), F32),
        compiler_params=pltpu.CompilerParams(dimension_semantics=("arbitrary",),
                                             vmem_limit_bytes=VMEM_LIMIT),
        name="moba_blocks",
    )(grp, blk, head, nvalid, table, k_aug, vt_aug)


def _combine_kernel(rows_ref, k_ref, vt_ref, *refs):
    part_refs, o_ref = refs[:-1], refs[-1]
    i = pl.program_id(1)
    tq = MOBA_BLOCK
    hidden = (lax.broadcasted_iota(jnp.int32, (MOBA_BLOCK, tq), 0)
              > lax.broadcasted_iota(jnp.int32, (MOBA_BLOCK, tq), 1))
    heads = range(HEADS_PER_STEP)
    sts = [jnp.where(hidden, -jnp.inf, _scores_t(k_ref[:, hh * AUG:(hh + 1) * AUG], rows_ref[hh]))
           for hh in heads]
    m_own = [jnp.max(sts[hh], axis=0, keepdims=True) for hh in heads]
    ps = [jnp.exp(sts[hh] - m_own[hh]).astype(BF16) for hh in heads]
    acc_own = [jnp.dot(vt_ref[0, hh * V_ROWS:(hh + 1) * V_ROWS, :], ps[hh], preferred_element_type=F32)
               for hh in heads]
    outs = []
    for hh in heads:
        parts = []
        for r in range(MOBA_TOP_K):
            ok = i > r
            gt = part_refs[r * HEADS_PER_STEP + hh][0, 0].T
            parts.append((jnp.where(ok, gt[ATTN_HEAD_DIM + 1:ATTN_HEAD_DIM + 2, :], -jnp.inf),
                          jnp.where(ok, gt[:V_ROWS, :], 0.0)))
        m_all = m_own[hh]
        for m_r, _ in parts:
            m_all = jnp.maximum(m_all, m_r)
        tot = jnp.exp(m_own[hh] - m_all) * acc_own[hh]
        for m_r, a_r in parts:
            tot = tot + jnp.exp(m_r - m_all) * a_r
        outs.append((tot[:ATTN_HEAD_DIM, :] / tot[ATTN_HEAD_DIM:ATTN_HEAD_DIM + 1, :]).T)
    o_ref[...] = jnp.concatenate(outs, axis=1).astype(o_ref.dtype)


def _moba_combine(rows, k_aug, vt_aug, parts, nb):
    s = nb * MOBA_BLOCK
    hp = HEADS_PER_STEP
    n_steps = ATTN_HEADS // hp

    def part_spec(r, hh):
        return pl.BlockSpec((1, 1, MOBA_BLOCK, PART_W), lambda p, i: (r, p * hp + hh, i, 0))

    return pl.pallas_call(
        _combine_kernel,
        grid=(n_steps, nb),
        in_specs=[pl.BlockSpec((hp, MOBA_BLOCK, AUG), lambda p, i: (p, i, 0)),
                  pl.BlockSpec((MOBA_BLOCK, hp * AUG), lambda p, i: (i, p)),
                  pl.BlockSpec((1, hp * V_ROWS, MOBA_BLOCK), lambda p, i: (i, p, 0))]
                 + [part_spec(r, hh) for r in range(MOBA_TOP_K) for hh in range(hp)],
        out_specs=pl.BlockSpec((MOBA_BLOCK, hp * ATTN_HEAD_DIM), lambda p, i: (i, p)),
        out_shape=jax.ShapeDtypeStruct((s, ATTN_WIDTH), BF16),
        compiler_params=pltpu.CompilerParams(dimension_semantics=("arbitrary", "arbitrary"),
                                             vmem_limit_bytes=VMEM_LIMIT),
        name="moba_combine",
    )(rows, k_aug, vt_aug, *([parts] * (MOBA_TOP_K * hp)))


def _work_list(counts, nb):
    base_tiles, cap_tiles = _bucket_base_tiles(nb)
    n = counts[:, :nb - 1].astype(I32).reshape(-1)
    groups = (n + GROUP_ROWS - 1) // GROUP_ROWS
    ends = jnp.cumsum(groups)
    n_pairs = ATTN_HEADS * nb * MOBA_BLOCK * MOBA_TOP_K
    n_work = -(-(n_pairs // GROUP_ROWS + n.shape[0]) // 8) * 8
    w = jnp.arange(n_work, dtype=I32)
    bucket = jnp.searchsorted(ends, w, side="right").astype(I32)
    live = bucket < n.shape[0]
    bucket = jnp.minimum(bucket, n.shape[0] - 1)
    g_in = w - (ends[bucket] - groups[bucket])
    head = bucket // (nb - 1)
    blk = bucket % (nb - 1)
    grp = (head * cap_tiles + jnp.asarray(base_tiles, I32)[blk]) // TILE_GROUP + g_in
    nvalid = jnp.clip(n[bucket] - g_in * GROUP_ROWS, 0, GROUP_ROWS)
    spare = (ATTN_HEADS * cap_tiles * MOBA_BLOCK + _dump_rows()) // GROUP_ROWS
    return (jnp.where(live, grp, spare), jnp.where(live, blk, 0), jnp.where(live, head, 0),
            jnp.where(live, nvalid, 0))


def _dump_rows():
    return -(-ATTN_HEADS * MOBA_TOP_K * MOBA_TOP_K * MOBA_BLOCK // GROUP_ROWS) * GROUP_ROWS


def _moba(qt, kmean, k_aug, vt_aug):
    nb = k_aug.shape[0] // MOBA_BLOCK
    _, cap_tiles = _bucket_base_tiles(nb)
    n_slots = ATTN_HEADS * cap_tiles * MOBA_BLOCK + _dump_rows() + GROUP_ROWS
    rows, sidx, gidx, counts = _moba_select(qt, kmean, nb)
    s = nb * MOBA_BLOCK
    order = lambda a: a[:, :, :MOBA_TOP_K, :].transpose(2, 0, 1, 3).reshape(MOBA_TOP_K, ATTN_HEADS * s)
    table = _sc_scatter_rows(rows.reshape(ATTN_HEADS * s, AUG), order(sidx), n_slots)
    work = _work_list(counts[:, :, 0], nb)
    parts = _moba_blocks(work, table, k_aug, vt_aug, n_slots)
    gathered = _sc_gather_rows(parts, order(gidx).reshape(-1))
    gathered = gathered.reshape(MOBA_TOP_K, ATTN_HEADS, s, PART_W)
    return _moba_combine(rows, k_aug, vt_aug, gathered, nb)


def _mlstm_kernel(qk_ref, vo_ref, gates_ref, conv_ref, gbias_ref, ng_ref, y_ref,
                  xbuf, c_ref, n_ref, m_ref):
    L = MLSTM_CHUNK
    d = MLSTM_HEAD_DIM
    halo = 8

    @pl.when(pl.program_id(0) == 0)
    def _():
        xbuf[0:halo, :] = jnp.zeros((halo, xbuf.shape[1]), F32)
        c_ref[...] = jnp.zeros(c_ref.shape, F32)
        n_ref[...] = jnp.zeros(n_ref.shape, F32)
        m_ref[...] = jnp.zeros(m_ref.shape, F32)

    xbuf[halo:halo + L, :] = qk_ref[...]
    w = conv_ref[...]
    y = xbuf[halo - CONV_WIDTH + 1:halo - CONV_WIDTH + 1 + L, :] * w[0:1, :]
    for j in range(1, CONV_WIDTH):
        off = halo - CONV_WIDTH + 1 + j
        y = y + xbuf[off:off + L, :] * w[j:j + 1, :]
    xbuf[0:halo, :] = xbuf[L:L + halo, :]
    qk = y * _sigmoid(y)

    a = gates_ref[...] + gbias_ref[...]
    lane = lax.broadcasted_iota(jnp.int32, (L, LANES), 1)
    is_f = jnp.logical_and(lane >= MLSTM_HEADS, lane < 2 * MLSTM_HEADS)
    log_f = jnp.where(is_f, jnp.minimum(a, 0.0) - jnp.log(1.0 + jnp.exp(-jnp.abs(a))), 0.0)
    t_io = lax.broadcasted_iota(jnp.int32, (L, L), 0)
    s_io = lax.broadcasted_iota(jnp.int32, (L, L), 1)
    causal = s_io <= t_io
    bcum = jnp.dot(causal.astype(F32), log_f, precision=lax.Precision.HIGHEST,
                   preferred_element_type=F32)
    colm = jnp.where(lane < MLSTM_HEADS, a, bcum)
    rowm = colm.T

    for hd in range(MLSTM_HEADS):
        hs = slice(hd * d, (hd + 1) * d)
        q = qk[:, hs]
        k = qk[:, MLSTM_WIDTH + hd * d:MLSTM_WIDTH + (hd + 1) * d] * (d ** -0.5)
        v = vo_ref[:, hs]
        og = vo_ref[:, MLSTM_WIDTH + hd * d:MLSTM_WIDTH + (hd + 1) * d]
        ic_c = colm[:, hd:hd + 1]
        bc_c = colm[:, MLSTM_HEADS + hd:MLSTM_HEADS + hd + 1]
        ic_r = rowm[hd:hd + 1, :]
        bc_r = rowm[MLSTM_HEADS + hd:MLSTM_HEADS + hd + 1, :]
        m_prev = m_ref[hd:hd + 1, 0:1]

        dmat = jnp.where(causal, bc_c - bc_r + ic_r, -jnp.inf)
        inter = bc_c + m_prev
        m_t = jnp.maximum(inter, jnp.max(dmat, axis=1, keepdims=True))
        w_intra = jnp.exp(dmat - m_t)
        w_inter = jnp.exp(inter - m_t)
        qb, kb, vb = q.astype(BF16), k.astype(BF16), v.astype(BF16)
        sc = lax.dot_general(qb, kb, NT_DIMS, preferred_element_type=F32) * w_intra
        c_st = c_ref[hd]
        n_st = n_ref[hd]
        num = (w_inter * jnp.dot(qb, c_st.astype(BF16), preferred_element_type=F32)
               + jnp.dot(sc.astype(BF16), vb, preferred_element_type=F32))
        den = (w_inter * jnp.sum(q * n_st, axis=1, keepdims=True)
               + jnp.sum(sc, axis=1, keepdims=True))
        h_out = num / jnp.maximum(jnp.abs(den), jnp.exp(-m_t))

        b_last = bc_c[L - 1:L, :]
        g_c = b_last - bc_c + ic_c
        m_new = jnp.maximum(b_last + m_prev, jnp.max(g_c, axis=0, keepdims=True))
        decay = jnp.exp(b_last + m_prev - m_new)
        kw = k * jnp.exp(g_c - m_new)
        c_ref[hd] = decay * c_st + jnp.dot(kw.T.astype(BF16), vb, preferred_element_type=F32)
        n_ref[hd] = decay * n_st + jnp.sum(kw, axis=0, keepdims=True)
        m_ref[hd:hd + 1, :] = jnp.broadcast_to(m_new, (1, LANES))

        hg = h_out * _sigmoid(og)
        mu = jnp.mean(hg, axis=1, keepdims=True)
        var = jnp.mean(jnp.square(hg - mu), axis=1, keepdims=True)
        y_ref[:, hs] = ((hg - mu) * lax.rsqrt(var + NORM_EPS) * ng_ref[:, hs]).astype(y_ref.dtype)


def _mlstm(qkm, vom, gates, conv, gbias, ng):
    s = qkm.shape[0]
    L = MLSTM_CHUNK
    row = lambda w: pl.BlockSpec((L, w), lambda t: (t, 0))
    whole = pl.BlockSpec(memory_space=pltpu.VMEM)
    return pl.pallas_call(
        _mlstm_kernel,
        grid=(s // L,),
        in_specs=[row(2 * MLSTM_WIDTH), row(2 * MLSTM_WIDTH), row(LANES), whole, whole, whole],
        out_specs=row(MLSTM_WIDTH),
        out_shape=jax.ShapeDtypeStruct((s, MLSTM_WIDTH), BF16),
        scratch_shapes=[pltpu.VMEM((L + 8, 2 * MLSTM_WIDTH), F32),
                        pltpu.VMEM((MLSTM_HEADS, MLSTM_HEAD_DIM, MLSTM_HEAD_DIM), F32),
                        pltpu.VMEM((MLSTM_HEADS, 1, MLSTM_HEAD_DIM), F32),
                        pltpu.VMEM((8, LANES), F32)],
        compiler_params=pltpu.CompilerParams(dimension_semantics=("arbitrary",),
                                             vmem_limit_bytes=VMEM_LIMIT),
        name="mlstm",
    )(qkm, vom, gates, conv, gbias, ng)


def _out_ffn_kernel(x_ref, ya_ref, ym_ref, gg_ref, woa_ref, wom_ref, wo_ref, fg_ref,
                    wgt_ref, wup_ref, wdn_ref, fin_ref, o_ref, *, final_norm):
    a = jnp.dot(ya_ref[...], woa_ref[...], preferred_element_type=F32)
    b = jnp.dot(ym_ref[...], wom_ref[...], preferred_element_type=F32)
    merged = _sigmoid(gg_ref[:, :D_MODEL]) * a + _sigmoid(gg_ref[:, D_MODEL:]) * b
    x1 = x_ref[...] + jnp.dot(merged.astype(BF16), wo_ref[...], preferred_element_type=F32)
    h2 = _rms(x1, fg_ref[...]).astype(BF16)
    acc = jnp.zeros(x1.shape, F32)
    for c in range(0, FFN_HIDDEN, FFN_CHUNK):
        gt = jnp.dot(h2, wgt_ref[:, c:c + FFN_CHUNK], preferred_element_type=F32)
        up = jnp.dot(h2, wup_ref[:, c:c + FFN_CHUNK], preferred_element_type=F32)
        act = (gt * _sigmoid(gt) * up).astype(BF16)
        acc = acc + jnp.dot(act, wdn_ref[c:c + FFN_CHUNK, :], preferred_element_type=F32)
    x2 = x1 + acc
    o_ref[...] = _rms(x2, fin_ref[...]) if final_norm else x2


def _out_ffn(x, ya, ym, gg, woa, wom, wo, fg, wgt, wup, wdn, fin, final_norm):
    s = x.shape[0]
    row = lambda w: pl.BlockSpec((OUT_TILE, w), lambda i: (i, 0))
    whole = pl.BlockSpec(memory_space=pltpu.VMEM)
    return pl.pallas_call(
        functools.partial(_out_ffn_kernel, final_norm=final_norm),
        grid=(s // OUT_TILE,),
        in_specs=[row(D_MODEL), row(ATTN_WIDTH), row(MLSTM_WIDTH), row(2 * D_MODEL)] + [whole] * 8,
        out_specs=row(D_MODEL),
        out_shape=jax.ShapeDtypeStruct((s, D_MODEL), F32),
        compiler_params=pltpu.CompilerParams(dimension_semantics=("arbitrary",),
                                             vmem_limit_bytes=VMEM_LIMIT),
        name="out_ffn",
    )(x, ya, ym, gg, woa, wom, wo, fg, wgt, wup, wdn, fin)


def _alibi_slopes():
    return np.exp2(-8.0 * np.arange(1, ATTN_HEADS + 1, dtype=np.float64) / ATTN_HEADS).astype(np.float32)


def _pad_heads(w, width):
    d = w.shape[0]
    w = w.reshape(d, ATTN_HEADS, ATTN_HEAD_DIM)
    w = jnp.pad(w, ((0, 0), (0, 0), (0, width - ATTN_HEAD_DIM)))
    return w.reshape(d, ATTN_HEADS * width)


def kernel(x, mix_norm_g, w_in, conv_qk, b_igate, b_fgate, mlstm_norm_g, w_out_attn, w_out_mlstm,
           w_o, ffn_norm_g, w_ffn_gate, w_ffn_up, w_ffn_down, final_norm_g):
    batch, seq, _ = x.shape
    depth = w_in.shape[0]
    assert seq % OUT_TILE == 0 and seq % (MOBA_BLOCK * 8) == 0
    slopes = _alibi_slopes()
    alibi = np.zeros((2, ATTN_AUG_WIDTH), np.float32)
    alibi[0, ATTN_HEAD_DIM::AUG] = slopes
    alibi[1, ATTN_HEAD_DIM + 1::AUG] = slopes
    alibi = jnp.asarray(alibi)
    a0, a1, a2 = ATTN_WIDTH, 2 * ATTN_WIDTH, 3 * ATTN_WIDTH
    m1 = a2 + 4 * MLSTM_WIDTH
    g0 = m1 + 2 * MLSTM_HEADS

    outs = []
    for bi in range(batch):
        xb = x[bi]
        for layer in range(depth):
            w = w_in[layer]
            wq = w[:, :a0].T.astype(BF16)
            wk = _pad_heads(w[:, a0:a1], AUG).astype(BF16)
            wv = _pad_heads(w[:, a1:a2], V_ROWS).T.astype(BF16)
            wm = w[:, a2:m1].astype(BF16)
            wif = jnp.pad(w[:, m1:g0], ((0, 0), (0, LANES - 2 * MLSTM_HEADS))).astype(BF16)
            wg = w[:, g0:].astype(BF16)
            qt, k_aug, vt_aug, kmean, qkm, vom, gates, gg = _inproj(
                xb, mix_norm_g[layer][None, :], wq, wk, wv, wm, wif, wg, alibi)
            nb = kmean.shape[0]
            kmean = jnp.pad(kmean.reshape(nb, ATTN_AUG_WIDTH), ((0, -nb % 8), (0, 0)))
            ya = _moba(qt, kmean, k_aug, vt_aug)
            gbias = jnp.pad(jnp.concatenate([b_igate[layer], b_fgate[layer]])[None, :],
                            ((0, 0), (0, LANES - 2 * MLSTM_HEADS))).astype(F32)
            ym = _mlstm(qkm, vom, gates, conv_qk[layer], gbias, mlstm_norm_g[layer][None, :])
            xb = _out_ffn(xb, ya, ym, gg,
                          w_out_attn[layer].astype(BF16), w_out_mlstm[layer].astype(BF16),
                          w_o[layer].astype(BF16), ffn_norm_g[layer][None, :],
                          w_ffn_gate[layer].astype(BF16), w_ffn_up[layer].astype(BF16),
                          w_ffn_down[layer].astype(BF16), final_norm_g[None, :],
                          final_norm=(layer == depth - 1))
        outs.append(xb)
    return jnp.stack(outs, axis=0)
```

```python
import functools

import numpy as np
import jax
import jax.numpy as jnp
from jax import lax
from jax.experimental import pallas as pl
from jax.experimental.pallas import tpu as pltpu

D_MODEL = 1024
ATTN_HEADS = 8
ATTN_HEAD_DIM = 64
ATTN_WIDTH = ATTN_HEADS * ATTN_HEAD_DIM
MOBA_BLOCK = 256
MOBA_TOP_K = 3
MLSTM_HEADS = 4
MLSTM_HEAD_DIM = 128
MLSTM_WIDTH = MLSTM_HEADS * MLSTM_HEAD_DIM
MLSTM_CHUNK = 128
CONV_WIDTH = 4
FFN_HIDDEN = 2816
NORM_EPS = 1e-6

LANES = 128
AUG = 2 * ATTN_HEAD_DIM
ATTN_AUG_WIDTH = ATTN_HEADS * AUG
V_ROWS = 80
N_BLOCK_COLS = AUG - ATTN_HEAD_DIM
HEADS_PER_STEP = 2
MLSTM_STEP_CHUNKS = 2
CONV_HALO = 8
KEY_GROUP = 4
NEG_BIAS = -1e9
FFN_CHUNK = 256
OUT_TILE = 512
VMEM_LIMIT = 56 * 1024 * 1024

F32 = jnp.float32
BF16 = jnp.bfloat16
NT_DIMS = (((1,), (1,)), ((), ()))


def _rms(x, g):
    return x * lax.rsqrt(jnp.mean(x * x, axis=-1, keepdims=True) + NORM_EPS) * g


def _sigmoid(x):
    return 1.0 / (1.0 + jnp.exp(-x))


def _inproj_kernel(x_ref, g_ref, wq_ref, wk_ref, wv_ref, wm_ref, wif_ref, wg_ref, alibi_ref,
                   q_ref, k_ref, v_ref, kmean_ref, qkm_ref, vom_ref, gates_ref, gg_ref):
    blk = pl.program_id(0)
    hb = _rms(x_ref[...], g_ref[...]).astype(BF16)
    q_ref[0] = lax.dot_general(wq_ref[...], hb, NT_DIMS, preferred_element_type=F32)
    kf = jnp.dot(hb, wk_ref[...], preferred_element_type=F32)
    kmean_ref[0] = jnp.mean(kf, axis=0, keepdims=True)
    lane = lax.broadcasted_iota(jnp.int32, (1, ATTN_AUG_WIDTH), 1) & (AUG - 1)
    row = lax.broadcasted_iota(jnp.int32, (MOBA_BLOCK, 1), 0).astype(F32)
    onehot = jnp.where(lane - ATTN_HEAD_DIM == blk, 1.0, 0.0)
    onehot = jnp.where(lane == AUG - 1, 0.0, onehot)
    k_ref[...] = (kf + onehot + alibi_ref[...] * row).astype(BF16)
    vt = lax.dot_general(wv_ref[...], hb, NT_DIMS, preferred_element_type=F32)
    feat = lax.rem(lax.broadcasted_iota(jnp.int32, (ATTN_HEADS * V_ROWS, 1), 0), V_ROWS)
    v_ref[0] = (vt + jnp.where(feat == ATTN_HEAD_DIM, 1.0, 0.0)).astype(BF16)
    m = jnp.dot(hb, wm_ref[...], preferred_element_type=F32)
    qkm_ref[...] = m[:, :2 * MLSTM_WIDTH]
    vom_ref[...] = m[:, 2 * MLSTM_WIDTH:]
    gates_ref[...] = jnp.dot(hb, wif_ref[...], preferred_element_type=F32)
    gg_ref[...] = jnp.dot(hb, wg_ref[...], preferred_element_type=F32)


def _inproj(x, g, wq, wk, wv, wm, wif, wg, alibi):
    s = x.shape[0]
    nb = s // MOBA_BLOCK
    row = lambda w: pl.BlockSpec((MOBA_BLOCK, w), lambda i: (i, 0))
    whole = pl.BlockSpec(memory_space=pltpu.VMEM)
    return pl.pallas_call(
        _inproj_kernel,
        grid=(nb,),
        in_specs=[row(D_MODEL), whole, whole, whole, whole, whole, whole, whole, whole],
        out_specs=[pl.BlockSpec((1, ATTN_WIDTH, MOBA_BLOCK), lambda i: (i, 0, 0)),
                   row(ATTN_AUG_WIDTH),
                   pl.BlockSpec((1, ATTN_HEADS * V_ROWS, MOBA_BLOCK), lambda i: (i, 0, 0)),
                   pl.BlockSpec((1, 1, ATTN_AUG_WIDTH), lambda i: (i, 0, 0)),
                   row(2 * MLSTM_WIDTH), row(2 * MLSTM_WIDTH), row(LANES), row(2 * D_MODEL)],
        out_shape=[jax.ShapeDtypeStruct((nb, ATTN_WIDTH, MOBA_BLOCK), F32),
                   jax.ShapeDtypeStruct((s, ATTN_AUG_WIDTH), BF16),
                   jax.ShapeDtypeStruct((nb, ATTN_HEADS * V_ROWS, MOBA_BLOCK), BF16),
                   jax.ShapeDtypeStruct((nb, 1, ATTN_AUG_WIDTH), F32),
                   jax.ShapeDtypeStruct((s, 2 * MLSTM_WIDTH), F32),
                   jax.ShapeDtypeStruct((s, 2 * MLSTM_WIDTH), F32),
                   jax.ShapeDtypeStruct((s, LANES), F32),
                   jax.ShapeDtypeStruct((s, 2 * D_MODEL), F32)],
        compiler_params=pltpu.CompilerParams(dimension_semantics=("arbitrary",),
                                             vmem_limit_bytes=VMEM_LIMIT),
        name="inproj",
    )(x, g, wq, wk, wv, wm, wif, wg, alibi)


def _col_max(st):
    rows, n = st.shape
    fan = 8
    while rows > fan * 8 and rows % fan == 0:
        st = jnp.max(st.reshape(fan, rows // fan, n), axis=0)
        rows //= fan
    return jnp.max(st, axis=0, keepdims=True)


def _moba_kernel(slope_ref, q_ref, kmean_ref, k_ref, vt_ref, o_ref, s_ref):
    i = pl.program_id(1)
    tq = q_ref.shape[2]
    nb = N_BLOCK_COLS
    group_rows = KEY_GROUP * MOBA_BLOCK
    blk = lax.broadcasted_iota(jnp.int32, (nb, tq), 0).astype(F32)
    i_f = i.astype(F32)
    head_lanes = [slice(hh * AUG, (hh + 1) * AUG) for hh in range(HEADS_PER_STEP)]
    head_rows = [slice(hh * V_ROWS, (hh + 1) * V_ROWS) for hh in range(HEADS_PER_STEP)]
    q_augs = []
    for hh in range(HEADS_PER_STEP):
        slope = slope_ref[0, hh:hh + 1, 0:1]
        qh = q_ref[0, hh * ATTN_HEAD_DIM:(hh + 1) * ATTN_HEAD_DIM, :]
        km = kmean_ref[:, hh * AUG:hh * AUG + ATTN_HEAD_DIM]
        gate = jnp.dot(km, qh, precision=lax.Precision.HIGHEST, preferred_element_type=F32)
        valid = blk < i_f
        g = jnp.where(valid, gate, -jnp.inf)
        sel = jnp.zeros((nb, tq), jnp.bool_)
        for _ in range(MOBA_TOP_K):
            mx = jnp.max(g, axis=0, keepdims=True)
            idx = jnp.min(jnp.where(g == mx, blk, float(nb)), axis=0, keepdims=True)
            pick = blk == idx
            sel = jnp.logical_or(sel, pick)
            g = jnp.where(pick, -jnp.inf, g)
        bias = jnp.where(sel, (blk - i_f) * (slope * float(MOBA_BLOCK)), NEG_BIAS)
        bias = jnp.where(valid, bias, 0.0)
        bias = jnp.where(blk == float(nb - 1), 1.0, bias)
        q_augs.append(jnp.concatenate([qh * (ATTN_HEAD_DIM ** -0.5), bias], axis=0).astype(BF16))

    heads = range(HEADS_PER_STEP)
    n_past = i // KEY_GROUP
    own_first = n_past * KEY_GROUP
    hidden = (lax.broadcasted_iota(jnp.int32, (group_rows, tq), 0)
              - lax.broadcasted_iota(jnp.int32, (group_rows, tq), 1)) > (i - own_first) * MOBA_BLOCK

    def score(first, slot, causal):
        start = pl.multiple_of(first * MOBA_BLOCK, group_rows)
        col_max = []
        for hh in heads:
            st = jnp.dot(k_ref[pl.ds(start, group_rows), head_lanes[hh]], q_augs[hh],
                         preferred_element_type=F32)
            if causal:
                st = jnp.where(hidden, -jnp.inf, st)
            s_ref[slot, hh] = st
            col_max.append(_col_max(st))
        return col_max

    def consume(first, slot, m_prev, m_cur, accs):
        ps = [jnp.exp(s_ref[slot, hh] - m_cur[hh]).astype(BF16) for hh in heads]
        accs = [jnp.exp(m_prev[hh] - m_cur[hh]) * accs[hh] for hh in heads]
        for b in range(KEY_GROUP):
            for hh in heads:
                accs[hh] = accs[hh] + jnp.dot(vt_ref[first + b, head_rows[hh], :],
                                              ps[hh][b * MOBA_BLOCK:(b + 1) * MOBA_BLOCK, :],
                                              preferred_element_type=F32)
        return accs

    def step(first, slot, carry):
        pend, m_prev, m_cur, accs = carry
        cm = score(first, slot, causal=False)
        accs = consume(pend, 1 - slot, m_prev, m_cur, accs)
        return (jnp.asarray(first, jnp.int32), m_cur,
                [jnp.maximum(m_cur[hh], cm[hh]) for hh in heads], accs)

    zeros = [jnp.zeros((V_ROWS, tq), F32) for _ in heads]

    def start_even(_):
        cm = score(own_first, 0, causal=True)
        return jnp.asarray(own_first, jnp.int32), cm, cm, zeros

    def start_odd(_):
        cm = score(own_first, 1, causal=True)
        return step(0, 0, (own_first, cm, cm, zeros))

    odd = n_past % 2
    carry = lax.cond(odd == 1, start_odd, start_even, 0)

    def pair(first_group, carry):
        g = first_group * KEY_GROUP
        return step(g + KEY_GROUP, 0, step(g, 1, carry))

    n_pairs = n_past // 2
    odd_pair = n_pairs % 2
    carry = lax.cond(odd_pair == 1, functools.partial(pair, odd), lambda c: c, carry)
    done = odd + 2 * odd_pair

    def quad(u, carry):
        return pair(done + 4 * u + 2, pair(done + 4 * u, carry))

    pend, m_prev, m_cur, accs = lax.fori_loop(0, n_pairs // 2, quad, carry)
    accs = consume(pend, 0, m_prev, m_cur, accs)
    outs = []
    for hh in heads:
        ot = accs[hh][:ATTN_HEAD_DIM, :] / accs[hh][ATTN_HEAD_DIM:ATTN_HEAD_DIM + 1, :]
        outs.append(ot.T)
    o_ref[...] = jnp.concatenate(outs, axis=1).astype(o_ref.dtype)


def _moba(slope_tab, qt, kmean, k_aug, vt_aug):
    s = k_aug.shape[0]
    nb = s // MOBA_BLOCK
    n_steps = ATTN_HEADS // HEADS_PER_STEP
    return pl.pallas_call(
        _moba_kernel,
        grid=(n_steps, nb),
        in_specs=[pl.BlockSpec((1, HEADS_PER_STEP, LANES), lambda p, i: (p, 0, 0)),
                  pl.BlockSpec((1, HEADS_PER_STEP * ATTN_HEAD_DIM, MOBA_BLOCK), lambda p, i: (i, p, 0)),
                  pl.BlockSpec((N_BLOCK_COLS, HEADS_PER_STEP * AUG), lambda p, i: (0, p)),
                  pl.BlockSpec((s, HEADS_PER_STEP * AUG), lambda p, i: (0, p)),
                  pl.BlockSpec((nb, HEADS_PER_STEP * V_ROWS, MOBA_BLOCK), lambda p, i: (0, p, 0))],
        out_specs=pl.BlockSpec((MOBA_BLOCK, HEADS_PER_STEP * ATTN_HEAD_DIM), lambda p, i: (i, p)),
        out_shape=jax.ShapeDtypeStruct((s, ATTN_WIDTH), BF16),
        scratch_shapes=[pltpu.VMEM((2, HEADS_PER_STEP, KEY_GROUP * MOBA_BLOCK, MOBA_BLOCK), F32)],
        compiler_params=pltpu.CompilerParams(dimension_semantics=("arbitrary", "arbitrary"),
                                             vmem_limit_bytes=VMEM_LIMIT),
        name="moba",
    )(slope_tab, qt, kmean, k_aug, vt_aug)


def _mlstm_kernel(qk_ref, vo_ref, gates_ref, conv_ref, gbias_ref, ng_ref, y_ref,
                  xbuf, c_ref, n_ref, m_ref):
    L = MLSTM_CHUNK
    d = MLSTM_HEAD_DIM
    rows = qk_ref.shape[0]
    halo = xbuf.shape[0] - rows
    chunks = range(rows // L)
    heads = range(MLSTM_HEADS)

    @pl.when(pl.program_id(0) == 0)
    def _():
        xbuf[0:halo, :] = jnp.zeros((halo, xbuf.shape[1]), F32)
        c_ref[...] = jnp.zeros(c_ref.shape, F32)
        n_ref[...] = jnp.zeros(n_ref.shape, F32)
        m_ref[...] = jnp.zeros(m_ref.shape, F32)

    xbuf[halo:, :] = qk_ref[...]
    w = conv_ref[...]
    y = qk_ref[...] * w[CONV_WIDTH - 1:CONV_WIDTH, :]
    for j in range(CONV_WIDTH - 1):
        off = halo - CONV_WIDTH + 1 + j
        y = y + xbuf[off:off + rows, :] * w[j:j + 1, :]
    xbuf[0:halo, :] = xbuf[rows:rows + halo, :]
    qk = y * _sigmoid(y)

    a = gates_ref[...] + gbias_ref[...]
    lane = lax.broadcasted_iota(jnp.int32, (rows, LANES), 1)
    is_f = jnp.logical_and(lane >= MLSTM_HEADS, lane < 2 * MLSTM_HEADS)
    log_f = jnp.where(is_f, jnp.minimum(a, 0.0) - jnp.log(1.0 + jnp.exp(-jnp.abs(a))), 0.0)
    t_io = lax.broadcasted_iota(jnp.int32, (rows, rows), 0)
    s_io = lax.broadcasted_iota(jnp.int32, (rows, rows), 1)
    same_chunk = (t_io // L) == (s_io // L)
    tri = jnp.where(jnp.logical_and(s_io <= t_io, same_chunk), 1.0, 0.0)
    bcum = jnp.dot(tri, log_f, precision=lax.Precision.HIGHEST,
                   preferred_element_type=F32)
    colm = jnp.where(lane < MLSTM_HEADS, a, bcum)
    rowm = colm.T
    causal = (lax.broadcasted_iota(jnp.int32, (L, L), 1) <= lax.broadcasted_iota(jnp.int32, (L, L), 0))

    pairs = [(cc, hd) for cc in chunks for hd in heads]
    rs = lambda cc: slice(cc * L, (cc + 1) * L)
    q = {(cc, hd): qk[rs(cc), hd * d:(hd + 1) * d] for cc, hd in pairs}
    k = {(cc, hd): qk[rs(cc), MLSTM_WIDTH + hd * d:MLSTM_WIDTH + (hd + 1) * d] * (d ** -0.5) for cc, hd in pairs}
    qb = {p: q[p].astype(BF16) for p in pairs}
    kb = {p: k[p].astype(BF16) for p in pairs}
    vb = {(cc, hd): vo_ref[rs(cc), hd * d:(hd + 1) * d].astype(BF16) for cc, hd in pairs}
    ic_c = {(cc, hd): colm[rs(cc), hd:hd + 1] for cc, hd in pairs}
    bc_c = {(cc, hd): colm[rs(cc), MLSTM_HEADS + hd:MLSTM_HEADS + hd + 1] for cc, hd in pairs}
    ic_r = {(cc, hd): rowm[hd:hd + 1, rs(cc)] for cc, hd in pairs}
    bc_r = {(cc, hd): rowm[MLSTM_HEADS + hd:MLSTM_HEADS + hd + 1, rs(cc)] for cc, hd in pairs}
    dmat = {p: jnp.where(causal, bc_c[p] - bc_r[p] + ic_r[p], -jnp.inf) for p in pairs}
    a_t = {p: jnp.max(dmat[p], axis=1, keepdims=True) for p in pairs}
    sc = {p: lax.dot_general(qb[p], kb[p], NT_DIMS, preferred_element_type=F32) * jnp.exp(dmat[p] - a_t[p])
          for p in pairs}
    sv = {p: jnp.dot(sc[p].astype(BF16), vb[p], preferred_element_type=F32) for p in pairs}
    ssum = {p: jnp.sum(sc[p], axis=1, keepdims=True) for p in pairs}
    b_last = {p: bc_c[p][L - 1:L, :] for p in pairs}
    g_c = {p: b_last[p] - bc_c[p] + ic_c[p] for p in pairs}
    gmax = {p: jnp.max(g_c[p], axis=0, keepdims=True) for p in pairs}
    kw = {p: k[p] * jnp.exp(g_c[p] - gmax[p]) for p in pairs}
    upd = {p: jnp.dot(kw[p].T.astype(BF16), vb[p], preferred_element_type=F32) for p in pairs}
    ksum = {p: jnp.sum(kw[p], axis=0, keepdims=True) for p in pairs}
    h_out = {}
    for hd in heads:
        c_st, n_st, m_prev = c_ref[hd], n_ref[hd], m_ref[hd:hd + 1, 0:1]
        for cc in chunks:
            p = (cc, hd)
            inter = bc_c[p] + m_prev
            m_t = jnp.maximum(inter, a_t[p])
            r_t = jnp.exp(a_t[p] - m_t)
            w_inter = jnp.exp(inter - m_t)
            num = w_inter * jnp.dot(qb[p], c_st.astype(BF16), preferred_element_type=F32) + r_t * sv[p]
            den = w_inter * jnp.sum(q[p] * n_st, axis=1, keepdims=True) + r_t * ssum[p]
            h_out[p] = num / jnp.maximum(jnp.abs(den), jnp.exp(-m_t))
            m_new = jnp.maximum(b_last[p] + m_prev, gmax[p])
            decay = jnp.exp(b_last[p] + m_prev - m_new)
            gain = jnp.exp(gmax[p] - m_new)
            c_st = decay * c_st + gain * upd[p]
            n_st = decay * n_st + gain * ksum[p]
            m_prev = m_new
        c_ref[hd], n_ref[hd] = c_st, n_st
        m_ref[hd:hd + 1, :] = jnp.broadcast_to(m_prev, (1, LANES))
    for cc, hd in pairs:
        hs = slice(hd * d, (hd + 1) * d)
        og = vo_ref[rs(cc), MLSTM_WIDTH + hd * d:MLSTM_WIDTH + (hd + 1) * d]
        hg = h_out[(cc, hd)] * _sigmoid(og)
        mu = jnp.mean(hg, axis=1, keepdims=True)
        var = jnp.mean(jnp.square(hg - mu), axis=1, keepdims=True)
        y_ref[rs(cc), hs] = ((hg - mu) * lax.rsqrt(var + NORM_EPS) * ng_ref[:, hs]).astype(y_ref.dtype)


def _mlstm(qkm, vom, gates, conv, gbias, ng):
    s = qkm.shape[0]
    rows = MLSTM_STEP_CHUNKS * MLSTM_CHUNK
    row = lambda w: pl.BlockSpec((rows, w), lambda t: (t, 0))
    whole = pl.BlockSpec(memory_space=pltpu.VMEM)
    return pl.pallas_call(
        _mlstm_kernel,
        grid=(s // rows,),
        in_specs=[row(2 * MLSTM_WIDTH), row(2 * MLSTM_WIDTH), row(LANES), whole, whole, whole],
        out_specs=row(MLSTM_WIDTH),
        out_shape=jax.ShapeDtypeStruct((s, MLSTM_WIDTH), BF16),
        scratch_shapes=[pltpu.VMEM((rows + CONV_HALO, 2 * MLSTM_WIDTH), F32),
                        pltpu.VMEM((MLSTM_HEADS, MLSTM_HEAD_DIM, MLSTM_HEAD_DIM), F32),
                        pltpu.VMEM((MLSTM_HEADS, 1, MLSTM_HEAD_DIM), F32),
                        pltpu.VMEM((8, LANES), F32)],
        compiler_params=pltpu.CompilerParams(dimension_semantics=("arbitrary",),
                                             vmem_limit_bytes=VMEM_LIMIT),
        name="mlstm",
    )(qkm, vom, gates, conv, gbias, ng)


def _out_ffn_kernel(x_ref, ya_ref, ym_ref, gg_ref, woa_ref, wom_ref, wo_ref, fg_ref,
                    wgt_ref, wup_ref, wdn_ref, fin_ref, o_ref, *, final_norm):
    a = jnp.dot(ya_ref[...], woa_ref[...], preferred_element_type=F32)
    b = jnp.dot(ym_ref[...], wom_ref[...], preferred_element_type=F32)
    merged = _sigmoid(gg_ref[:, :D_MODEL]) * a + _sigmoid(gg_ref[:, D_MODEL:]) * b
    x1 = x_ref[...] + jnp.dot(merged.astype(BF16), wo_ref[...], preferred_element_type=F32)
    h2 = _rms(x1, fg_ref[...]).astype(BF16)
    acc = jnp.zeros(x1.shape, F32)
    for c in range(0, FFN_HIDDEN, FFN_CHUNK):
        gt = jnp.dot(h2, wgt_ref[:, c:c + FFN_CHUNK], preferred_element_type=F32)
        up = jnp.dot(h2, wup_ref[:, c:c + FFN_CHUNK], preferred_element_type=F32)
        act = (gt * _sigmoid(gt) * up).astype(BF16)
        acc = acc + jnp.dot(act, wdn_ref[c:c + FFN_CHUNK, :], preferred_element_type=F32)
    x2 = x1 + acc
    o_ref[...] = _rms(x2, fin_ref[...]) if final_norm else x2


def _out_ffn(x, ya, ym, gg, woa, wom, wo, fg, wgt, wup, wdn, fin, final_norm):
    s = x.shape[0]
    row = lambda w: pl.BlockSpec((OUT_TILE, w), lambda i: (i, 0))
    whole = pl.BlockSpec(memory_space=pltpu.VMEM)
    return pl.pallas_call(
        functools.partial(_out_ffn_kernel, final_norm=final_norm),
        grid=(s // OUT_TILE,),
        in_specs=[row(D_MODEL), row(ATTN_WIDTH), row(MLSTM_WIDTH), row(2 * D_MODEL)] + [whole] * 8,
        out_specs=row(D_MODEL),
        out_shape=jax.ShapeDtypeStruct((s, D_MODEL), F32),
        compiler_params=pltpu.CompilerParams(dimension_semantics=("arbitrary",),
                                             vmem_limit_bytes=VMEM_LIMIT),
        name="out_ffn",
    )(x, ya, ym, gg, woa, wom, wo, fg, wgt, wup, wdn, fin)


def _alibi_slopes():
    return np.exp2(-8.0 * np.arange(1, ATTN_HEADS + 1, dtype=np.float64) / ATTN_HEADS).astype(np.float32)


def _pad_heads(w, width):
    d = w.shape[0]
    w = w.reshape(d, ATTN_HEADS, ATTN_HEAD_DIM)
    w = jnp.pad(w, ((0, 0), (0, 0), (0, width - ATTN_HEAD_DIM)))
    return w.reshape(d, ATTN_HEADS * width)


def kernel(x, mix_norm_g, w_in, conv_qk, b_igate, b_fgate, mlstm_norm_g, w_out_attn, w_out_mlstm,
           w_o, ffn_norm_g, w_ffn_gate, w_ffn_up, w_ffn_down, final_norm_g):
    batch, seq, _ = x.shape
    depth = w_in.shape[0]
    assert seq % OUT_TILE == 0 and seq // MOBA_BLOCK <= N_BLOCK_COLS
    assert seq % (KEY_GROUP * MOBA_BLOCK) == 0
    slopes = _alibi_slopes()
    alibi = np.zeros((1, ATTN_AUG_WIDTH), np.float32)
    alibi[0, AUG - 1::AUG] = slopes
    alibi = jnp.asarray(alibi)
    slope_tab = jnp.asarray(np.broadcast_to(
        slopes.reshape(ATTN_HEADS // HEADS_PER_STEP, HEADS_PER_STEP, 1),
        (ATTN_HEADS // HEADS_PER_STEP, HEADS_PER_STEP, LANES)).copy())
    a0, a1, a2 = ATTN_WIDTH, 2 * ATTN_WIDTH, 3 * ATTN_WIDTH
    m1 = a2 + 4 * MLSTM_WIDTH
    g0 = m1 + 2 * MLSTM_HEADS

    outs = []
    for bi in range(batch):
        xb = x[bi]
        for layer in range(depth):
            w = w_in[layer]
            wq = w[:, :a0].T.astype(BF16)
            wk = _pad_heads(w[:, a0:a1], AUG).astype(BF16)
            wv = _pad_heads(w[:, a1:a2], V_ROWS).T.astype(BF16)
            wm = w[:, a2:m1].astype(BF16)
            wif = jnp.pad(w[:, m1:g0], ((0, 0), (0, LANES - 2 * MLSTM_HEADS))).astype(BF16)
            wg = w[:, g0:].astype(BF16)
            q, k_aug, v_aug, kmean, qkm, vom, gates, gg = _inproj(
                xb, mix_norm_g[layer][None, :], wq, wk, wv, wm, wif, wg, alibi)
            kmean = jnp.pad(kmean.reshape(kmean.shape[0], ATTN_AUG_WIDTH),
                            ((0, N_BLOCK_COLS - kmean.shape[0]), (0, 0)))
            ya = _moba(slope_tab, q, kmean, k_aug, v_aug)
            gbias = jnp.pad(jnp.concatenate([b_igate[layer], b_fgate[layer]])[None, :],
                            ((0, 0), (0, LANES - 2 * MLSTM_HEADS))).astype(F32)
            ym = _mlstm(qkm, vom, gates, conv_qk[layer], gbias, mlstm_norm_g[layer][None, :])
            xb = _out_ffn(xb, ya, ym, gg,
                          w_out_attn[layer].astype(BF16), w_out_mlstm[layer].astype(BF16),
                          w_o[layer].astype(BF16), ffn_norm_g[layer][None, :],
                          w_ffn_gate[layer].astype(BF16), w_ffn_up[layer].astype(BF16),
                          w_ffn_down[layer].astype(BF16), final_norm_g[None, :],
                          final_norm=(layer == depth - 1))
        outs.append(xb)
    return jnp.stack(outs, axis=0)
```

```python
import functools

import numpy as np
import jax
import jax.numpy as jnp
from jax import lax
from jax.experimental import pallas as pl
from jax.experimental.pallas import tpu as pltpu

D_MODEL = 1024
ATTN_HEADS = 8
ATTN_HEAD_DIM = 64
ATTN_WIDTH = ATTN_HEADS * ATTN_HEAD_DIM
MOBA_BLOCK = 256
MOBA_TOP_K = 3
MLSTM_HEADS = 4
MLSTM_HEAD_DIM = 128
MLSTM_WIDTH = MLSTM_HEADS * MLSTM_HEAD_DIM
MLSTM_CHUNK = 128
CONV_WIDTH = 4
FFN_HIDDEN = 2816
NORM_EPS = 1e-6

LANES = 128
AUG = 2 * ATTN_HEAD_DIM
ATTN_AUG_WIDTH = ATTN_HEADS * AUG
V_ROWS = 80
N_BLOCK_COLS = AUG - ATTN_HEAD_DIM
HEADS_PER_STEP = 2
MLSTM_STEP_CHUNKS = 2
CONV_HALO = 8
Q_TILES = 2
KEY_GROUP = 4
NEG_BIAS = -1e9
FFN_CHUNK = 256
OUT_TILE = 512
VMEM_LIMIT = 56 * 1024 * 1024

F32 = jnp.float32
BF16 = jnp.bfloat16
NT_DIMS = (((1,), (1,)), ((), ()))


def _rms(x, g):
    return x * lax.rsqrt(jnp.mean(x * x, axis=-1, keepdims=True) + NORM_EPS) * g


def _sigmoid(x):
    return 1.0 / (1.0 + jnp.exp(-x))


def _inproj_kernel(x_ref, g_ref, wq_ref, wk_ref, wv_ref, wm_ref, wif_ref, wg_ref, alibi_ref,
                   q_ref, k_ref, v_ref, kmean_ref, qkm_ref, vom_ref, gates_ref, gg_ref):
    blk = pl.program_id(0)
    hb = _rms(x_ref[...], g_ref[...]).astype(BF16)
    q_ref[0] = lax.dot_general(wq_ref[...], hb, NT_DIMS, preferred_element_type=F32)
    kf = jnp.dot(hb, wk_ref[...], preferred_element_type=F32)
    kmean_ref[0] = jnp.mean(kf, axis=0, keepdims=True)
    lane = lax.broadcasted_iota(jnp.int32, (1, ATTN_AUG_WIDTH), 1) & (AUG - 1)
    row = lax.broadcasted_iota(jnp.int32, (MOBA_BLOCK, 1), 0).astype(F32)
    onehot = jnp.where(lane - ATTN_HEAD_DIM == blk, 1.0, 0.0)
    onehot = jnp.where(lane == AUG - 1, 0.0, onehot)
    k_ref[...] = (kf + onehot + alibi_ref[...] * row).astype(BF16)
    vt = lax.dot_general(wv_ref[...], hb, NT_DIMS, preferred_element_type=F32)
    feat = lax.rem(lax.broadcasted_iota(jnp.int32, (ATTN_HEADS * V_ROWS, 1), 0), V_ROWS)
    v_ref[0] = (vt + jnp.where(feat == ATTN_HEAD_DIM, 1.0, 0.0)).astype(BF16)
    m = jnp.dot(hb, wm_ref[...], preferred_element_type=F32)
    qkm_ref[...] = m[:, :2 * MLSTM_WIDTH]
    vom_ref[...] = m[:, 2 * MLSTM_WIDTH:]
    gates_ref[...] = jnp.dot(hb, wif_ref[...], preferred_element_type=F32)
    gg_ref[...] = jnp.dot(hb, wg_ref[...], preferred_element_type=F32)


def _inproj(x, g, wq, wk, wv, wm, wif, wg, alibi):
    s = x.shape[0]
    nb = s // MOBA_BLOCK
    row = lambda w: pl.BlockSpec((MOBA_BLOCK, w), lambda i: (i, 0))
    whole = pl.BlockSpec(memory_space=pltpu.VMEM)
    return pl.pallas_call(
        _inproj_kernel,
        grid=(nb,),
        in_specs=[row(D_MODEL), whole, whole, whole, whole, whole, whole, whole, whole],
        out_specs=[pl.BlockSpec((1, ATTN_WIDTH, MOBA_BLOCK), lambda i: (i, 0, 0)),
                   row(ATTN_AUG_WIDTH),
                   pl.BlockSpec((1, ATTN_HEADS * V_ROWS, MOBA_BLOCK), lambda i: (i, 0, 0)),
                   pl.BlockSpec((1, 1, ATTN_AUG_WIDTH), lambda i: (i, 0, 0)),
                   row(2 * MLSTM_WIDTH), row(2 * MLSTM_WIDTH), row(LANES), row(2 * D_MODEL)],
        out_shape=[jax.ShapeDtypeStruct((nb, ATTN_WIDTH, MOBA_BLOCK), F32),
                   jax.ShapeDtypeStruct((s, ATTN_AUG_WIDTH), BF16),
                   jax.ShapeDtypeStruct((nb, ATTN_HEADS * V_ROWS, MOBA_BLOCK), BF16),
                   jax.ShapeDtypeStruct((nb, 1, ATTN_AUG_WIDTH), F32),
                   jax.ShapeDtypeStruct((s, 2 * MLSTM_WIDTH), F32),
                   jax.ShapeDtypeStruct((s, 2 * MLSTM_WIDTH), F32),
                   jax.ShapeDtypeStruct((s, LANES), F32),
                   jax.ShapeDtypeStruct((s, 2 * D_MODEL), F32)],
        compiler_params=pltpu.CompilerParams(dimension_semantics=("arbitrary",),
                                             vmem_limit_bytes=VMEM_LIMIT),
        name="inproj",
    )(x, g, wq, wk, wv, wm, wif, wg, alibi)


def _col_max(st):
    rows, n = st.shape
    fan = 8
    while rows > fan * 8 and rows % fan == 0:
        st = jnp.max(st.reshape(fan, rows // fan, n), axis=0)
        rows //= fan
    return jnp.max(st, axis=0, keepdims=True)


def _moba_kernel(slope_ref, q_ref, kmean_ref, k_ref, vt_ref, o_ref, s_ref):
    i = pl.program_id(1) * Q_TILES
    tq = Q_TILES * MOBA_BLOCK
    nb = N_BLOCK_COLS
    group_rows = KEY_GROUP * MOBA_BLOCK
    blk = lax.broadcasted_iota(jnp.int32, (nb, tq), 0).astype(F32)
    i_f = (i + lax.broadcasted_iota(jnp.int32, (1, tq), 1) // MOBA_BLOCK).astype(F32)
    head_lanes = [slice(hh * AUG, (hh + 1) * AUG) for hh in range(HEADS_PER_STEP)]
    head_rows = [slice(hh * V_ROWS, (hh + 1) * V_ROWS) for hh in range(HEADS_PER_STEP)]
    q_augs = []
    for hh in range(HEADS_PER_STEP):
        slope = slope_ref[0, hh:hh + 1, 0:1]
        qh = jnp.concatenate([q_ref[t, hh * ATTN_HEAD_DIM:(hh + 1) * ATTN_HEAD_DIM, :]
                              for t in range(Q_TILES)], axis=1)
        km = kmean_ref[:, hh * AUG:hh * AUG + ATTN_HEAD_DIM]
        gate = jnp.dot(km, qh, precision=lax.Precision.HIGHEST, preferred_element_type=F32)
        valid = blk < i_f
        g = jnp.where(valid, gate, -jnp.inf)
        sel = jnp.zeros((nb, tq), jnp.bool_)
        for _ in range(MOBA_TOP_K):
            mx = jnp.max(g, axis=0, keepdims=True)
            idx = jnp.min(jnp.where(g == mx, blk, float(nb)), axis=0, keepdims=True)
            pick = blk == idx
            sel = jnp.logical_or(sel, pick)
            g = jnp.where(pick, -jnp.inf, g)
        bias = jnp.where(sel, (blk - i_f) * (slope * float(MOBA_BLOCK)), NEG_BIAS)
        bias = jnp.where(valid, bias, 0.0)
        bias = jnp.where(blk == float(nb - 1), 1.0, bias)
        q_augs.append(jnp.concatenate([qh * (ATTN_HEAD_DIM ** -0.5), bias], axis=0).astype(BF16))

    heads = range(HEADS_PER_STEP)
    n_past = i // KEY_GROUP
    own_first = n_past * KEY_GROUP
    assert KEY_GROUP % Q_TILES == 0
    hidden = (lax.broadcasted_iota(jnp.int32, (group_rows, tq), 0)
              - lax.broadcasted_iota(jnp.int32, (group_rows, tq), 1)) > (i - own_first) * MOBA_BLOCK

    def score(first, slot, causal):
        start = pl.multiple_of(first * MOBA_BLOCK, group_rows)
        col_max = []
        for hh in heads:
            st = jnp.dot(k_ref[pl.ds(start, group_rows), head_lanes[hh]], q_augs[hh],
                         preferred_element_type=F32)
            if causal:
                st = jnp.where(hidden, -jnp.inf, st)
            s_ref[slot, hh] = st
            col_max.append(_col_max(st))
        return col_max

    def consume(first, slot, m_prev, m_cur, accs):
        ps = [jnp.exp(s_ref[slot, hh] - m_cur[hh]).astype(BF16) for hh in heads]
        accs = [jnp.exp(m_prev[hh] - m_cur[hh]) * accs[hh] for hh in heads]
        for b in range(KEY_GROUP):
            for hh in heads:
                accs[hh] = accs[hh] + jnp.dot(vt_ref[first + b, head_rows[hh], :],
                                              ps[hh][b * MOBA_BLOCK:(b + 1) * MOBA_BLOCK, :],
                                              preferred_element_type=F32)
        return accs

    def step(first, slot, carry):
        pend, m_prev, m_cur, accs = carry
        cm = score(first, slot, causal=False)
        accs = consume(pend, 1 - slot, m_prev, m_cur, accs)
        return (jnp.asarray(first, jnp.int32), m_cur,
                [jnp.maximum(m_cur[hh], cm[hh]) for hh in heads], accs)

    zeros = [jnp.zeros((V_ROWS, tq), F32) for _ in heads]

    def start_even(_):
        cm = score(own_first, 0, causal=True)
        return jnp.asarray(own_first, jnp.int32), cm, cm, zeros

    def start_odd(_):
        cm = score(own_first, 1, causal=True)
        return step(0, 0, (own_first, cm, cm, zeros))

    odd = n_past % 2
    carry = lax.cond(odd == 1, start_odd, start_even, 0)

    def pair(first_group, carry):
        g = first_group * KEY_GROUP
        return step(g + KEY_GROUP, 0, step(g, 1, carry))

    n_pairs = n_past // 2
    odd_pair = n_pairs % 2
    carry = lax.cond(odd_pair == 1, functools.partial(pair, odd), lambda c: c, carry)
    done = odd + 2 * odd_pair

    def quad(u, carry):
        return pair(done + 4 * u + 2, pair(done + 4 * u, carry))

    pend, m_prev, m_cur, accs = lax.fori_loop(0, n_pairs // 2, quad, carry)
    accs = consume(pend, 0, m_prev, m_cur, accs)
    outs = []
    for hh in heads:
        ot = accs[hh][:ATTN_HEAD_DIM, :] / accs[hh][ATTN_HEAD_DIM:ATTN_HEAD_DIM + 1, :]
        outs.append(ot.T)
    o_ref[...] = jnp.concatenate(outs, axis=1).astype(o_ref.dtype)


def _moba(slope_tab, qt, kmean, k_aug, vt_aug):
    s = k_aug.shape[0]
    nb = s // MOBA_BLOCK
    n_steps = ATTN_HEADS // HEADS_PER_STEP
    return pl.pallas_call(
        _moba_kernel,
        grid=(n_steps, nb // Q_TILES),
        in_specs=[pl.BlockSpec((1, HEADS_PER_STEP, LANES), lambda p, i: (p, 0, 0)),
                  pl.BlockSpec((Q_TILES, HEADS_PER_STEP * ATTN_HEAD_DIM, MOBA_BLOCK), lambda p, i: (i, p, 0)),
                  pl.BlockSpec((N_BLOCK_COLS, HEADS_PER_STEP * AUG), lambda p, i: (0, p)),
                  pl.BlockSpec((s, HEADS_PER_STEP * AUG), lambda p, i: (0, p)),
                  pl.BlockSpec((nb, HEADS_PER_STEP * V_ROWS, MOBA_BLOCK), lambda p, i: (0, p, 0))],
        out_specs=pl.BlockSpec((Q_TILES * MOBA_BLOCK, HEADS_PER_STEP * ATTN_HEAD_DIM), lambda p, i: (i, p)),
        out_shape=jax.ShapeDtypeStruct((s, ATTN_WIDTH), BF16),
        scratch_shapes=[pltpu.VMEM((2, HEADS_PER_STEP, KEY_GROUP * MOBA_BLOCK, Q_TILES * MOBA_BLOCK), F32)],
        compiler_params=pltpu.CompilerParams(dimension_semantics=("arbitrary", "arbitrary"),
                                             vmem_limit_bytes=VMEM_LIMIT),
        name="moba",
    )(slope_tab, qt, kmean, k_aug, vt_aug)


def _mlstm_kernel(qk_ref, vo_ref, gates_ref, conv_ref, gbias_ref, ng_ref, y_ref,
                  xbuf, c_ref, n_ref, m_ref):
    L = MLSTM_CHUNK
    d = MLSTM_HEAD_DIM
    rows = qk_ref.shape[0]
    halo = xbuf.shape[0] - rows
    chunks = range(rows // L)
    heads = range(MLSTM_HEADS)

    @pl.when(pl.program_id(0) == 0)
    def _():
        xbuf[0:halo, :] = jnp.zeros((halo, xbuf.shape[1]), F32)
        c_ref[...] = jnp.zeros(c_ref.shape, F32)
        n_ref[...] = jnp.zeros(n_ref.shape, F32)
        m_ref[...] = jnp.zeros(m_ref.shape, F32)

    xbuf[halo:, :] = qk_ref[...]
    w = conv_ref[...]
    y = qk_ref[...] * w[CONV_WIDTH - 1:CONV_WIDTH, :]
    for j in range(CONV_WIDTH - 1):
        off = halo - CONV_WIDTH + 1 + j
        y = y + xbuf[off:off + rows, :] * w[j:j + 1, :]
    xbuf[0:halo, :] = xbuf[rows:rows + halo, :]
    qk = y * _sigmoid(y)

    a = gates_ref[...] + gbias_ref[...]
    lane = lax.broadcasted_iota(jnp.int32, (rows, LANES), 1)
    is_f = jnp.logical_and(lane >= MLSTM_HEADS, lane < 2 * MLSTM_HEADS)
    log_f = jnp.where(is_f, jnp.minimum(a, 0.0) - jnp.log(1.0 + jnp.exp(-jnp.abs(a))), 0.0)
    t_io = lax.broadcasted_iota(jnp.int32, (rows, rows), 0)
    s_io = lax.broadcasted_iota(jnp.int32, (rows, rows), 1)
    same_chunk = (t_io // L) == (s_io // L)
    tri = jnp.where(jnp.logical_and(s_io <= t_io, same_chunk), 1.0, 0.0)
    bcum = jnp.dot(tri, log_f, precision=lax.Precision.HIGHEST,
                   preferred_element_type=F32)
    colm = jnp.where(lane < MLSTM_HEADS, a, bcum)
    rowm = colm.T
    causal = (lax.broadcasted_iota(jnp.int32, (L, L), 1) <= lax.broadcasted_iota(jnp.int32, (L, L), 0))

    pairs = [(cc, hd) for cc in chunks for hd in heads]
    rs = lambda cc: slice(cc * L, (cc + 1) * L)
    q = {(cc, hd): qk[rs(cc), hd * d:(hd + 1) * d] for cc, hd in pairs}
    k = {(cc, hd): qk[rs(cc), MLSTM_WIDTH + hd * d:MLSTM_WIDTH + (hd + 1) * d] * (d ** -0.5) for cc, hd in pairs}
    qb = {p: q[p].astype(BF16) for p in pairs}
    kb = {p: k[p].astype(BF16) for p in pairs}
    vb = {(cc, hd): vo_ref[rs(cc), hd * d:(hd + 1) * d].astype(BF16) for cc, hd in pairs}
    ic_c = {(cc, hd): colm[rs(cc), hd:hd + 1] for cc, hd in pairs}
    bc_c = {(cc, hd): colm[rs(cc), MLSTM_HEADS + hd:MLSTM_HEADS + hd + 1] for cc, hd in pairs}
    ic_r = {(cc, hd): rowm[hd:hd + 1, rs(cc)] for cc, hd in pairs}
    bc_r = {(cc, hd): rowm[MLSTM_HEADS + hd:MLSTM_HEADS + hd + 1, rs(cc)] for cc, hd in pairs}
    dmat = {p: jnp.where(causal, bc_c[p] - bc_r[p] + ic_r[p], -jnp.inf) for p in pairs}
    a_t = {p: jnp.max(dmat[p], axis=1, keepdims=True) for p in pairs}
    sc = {p: lax.dot_general(qb[p], kb[p], NT_DIMS, preferred_element_type=F32) * jnp.exp(dmat[p] - a_t[p])
          for p in pairs}
    sv = {p: jnp.dot(sc[p].astype(BF16), vb[p], preferred_element_type=F32) for p in pairs}
    ssum = {p: jnp.sum(sc[p], axis=1, keepdims=True) for p in pairs}
    b_last = {p: bc_c[p][L - 1:L, :] for p in pairs}
    g_c = {p: b_last[p] - bc_c[p] + ic_c[p] for p in pairs}
    gmax = {p: jnp.max(g_c[p], axis=0, keepdims=True) for p in pairs}
    kw = {p: k[p] * jnp.exp(g_c[p] - gmax[p]) for p in pairs}
    upd = {p: jnp.dot(kw[p].T.astype(BF16), vb[p], preferred_element_type=F32) for p in pairs}
    ksum = {p: jnp.sum(kw[p], axis=0, keepdims=True) for p in pairs}
    h_out = {}
    for hd in heads:
        c_st, n_st, m_prev = c_ref[hd], n_ref[hd], m_ref[hd:hd + 1, 0:1]
        for cc in chunks:
            p = (cc, hd)
            inter = bc_c[p] + m_prev
            m_t = jnp.maximum(inter, a_t[p])
            r_t = jnp.exp(a_t[p] - m_t)
            w_inter = jnp.exp(inter - m_t)
            num = w_inter * jnp.dot(qb[p], c_st.astype(BF16), preferred_element_type=F32) + r_t * sv[p]
            den = w_inter * jnp.sum(q[p] * n_st, axis=1, keepdims=True) + r_t * ssum[p]
            h_out[p] = num / jnp.maximum(jnp.abs(den), jnp.exp(-m_t))
            m_new = jnp.maximum(b_last[p] + m_prev, gmax[p])
            decay = jnp.exp(b_last[p] + m_prev - m_new)
            gain = jnp.exp(gmax[p] - m_new)
            c_st = decay * c_st + gain * upd[p]
            n_st = decay * n_st + gain * ksum[p]
            m_prev = m_new
        c_ref[hd], n_ref[hd] = c_st, n_st
        m_ref[hd:hd + 1, :] = jnp.broadcast_to(m_prev, (1, LANES))
    for cc, hd in pairs:
        hs = slice(hd * d, (hd + 1) * d)
        og = vo_ref[rs(cc), MLSTM_WIDTH + hd * d:MLSTM_WIDTH + (hd + 1) * d]
        hg = h_out[(cc, hd)] * _sigmoid(og)
        mu = jnp.mean(hg, axis=1, keepdims=True)
        var = jnp.mean(jnp.square(hg - mu), axis=1, keepdims=True)
        y_ref[rs(cc), hs] = ((hg - mu) * lax.rsqrt(var + NORM_EPS) * ng_ref[:, hs]).astype(y_ref.dtype)


def _mlstm(qkm, vom, gates, conv, gbias, ng):
    s = qkm.shape[0]
    rows = MLSTM_STEP_CHUNKS * MLSTM_CHUNK
    row = lambda w: pl.BlockSpec((rows, w), lambda t: (t, 0))
    whole = pl.BlockSpec(memory_space=pltpu.VMEM)
    return pl.pallas_call(
        _mlstm_kernel,
        grid=(s // rows,),
        in_specs=[row(2 * MLSTM_WIDTH), row(2 * MLSTM_WIDTH), row(LANES), whole, whole, whole],
        out_specs=row(MLSTM_WIDTH),
        out_shape=jax.ShapeDtypeStruct((s, MLSTM_WIDTH), BF16),
        scratch_shapes=[pltpu.VMEM((rows + CONV_HALO, 2 * MLSTM_WIDTH), F32),
                        pltpu.VMEM((MLSTM_HEADS, MLSTM_HEAD_DIM, MLSTM_HEAD_DIM), F32),
                        pltpu.VMEM((MLSTM_HEADS, 1, MLSTM_HEAD_DIM), F32),
                        pltpu.VMEM((8, LANES), F32)],
        compiler_params=pltpu.CompilerParams(dimension_semantics=("arbitrary",),
                                             vmem_limit_bytes=VMEM_LIMIT),
        name="mlstm",
    )(qkm, vom, gates, conv, gbias, ng)


def _out_ffn_kernel(x_ref, ya_ref, ym_ref, gg_ref, woa_ref, wom_ref, wo_ref, fg_ref,
                    wgt_ref, wup_ref, wdn_ref, fin_ref, o_ref, *, final_norm):
    a = jnp.dot(ya_ref[...], woa_ref[...], preferred_element_type=F32)
    b = jnp.dot(ym_ref[...], wom_ref[...], preferred_element_type=F32)
    merged = _sigmoid(gg_ref[:, :D_MODEL]) * a + _sigmoid(gg_ref[:, D_MODEL:]) * b
    x1 = x_ref[...] + jnp.dot(merged.astype(BF16), wo_ref[...], preferred_element_type=F32)
    h2 = _rms(x1, fg_ref[...]).astype(BF16)
    acc = jnp.zeros(x1.shape, F32)
    for c in range(0, FFN_HIDDEN, FFN_CHUNK):
        gt = jnp.dot(h2, wgt_ref[:, c:c + FFN_CHUNK], preferred_element_type=F32)
        up = jnp.dot(h2, wup_ref[:, c:c + FFN_CHUNK], preferred_element_type=F32)
        act = (gt * _sigmoid(gt) * up).astype(BF16)
        acc = acc + jnp.dot(act, wdn_ref[c:c + FFN_CHUNK, :], preferred_element_type=F32)
    x2 = x1 + acc
    o_ref[...] = _rms(x2, fin_ref[...]) if final_norm else x2


def _out_ffn(x, ya, ym, gg, woa, wom, wo, fg, wgt, wup, wdn, fin, final_norm):
    s = x.shape[0]
    row = lambda w: pl.BlockSpec((OUT_TILE, w), lambda i: (i, 0))
    whole = pl.BlockSpec(memory_space=pltpu.VMEM)
    return pl.pallas_call(
        functools.partial(_out_ffn_kernel, final_norm=final_norm),
        grid=(s // OUT_TILE,),
        in_specs=[row(D_MODEL), row(ATTN_WIDTH), row(MLSTM_WIDTH), row(2 * D_MODEL)] + [whole] * 8,
        out_specs=row(D_MODEL),
        out_shape=jax.ShapeDtypeStruct((s, D_MODEL), F32),
        compiler_params=pltpu.CompilerParams(dimension_semantics=("arbitrary",),
                                             vmem_limit_bytes=VMEM_LIMIT),
        name="out_ffn",
    )(x, ya, ym, gg, woa, wom, wo, fg, wgt, wup, wdn, fin)


def _alibi_slopes():
    return np.exp2(-8.0 * np.arange(1, ATTN_HEADS + 1, dtype=np.float64) / ATTN_HEADS).astype(np.float32)


def _pad_heads(w, width):
    d = w.shape[0]
    w = w.reshape(d, ATTN_HEADS, ATTN_HEAD_DIM)
    w = jnp.pad(w, ((0, 0), (0, 0), (0, width - ATTN_HEAD_DIM)))
    return w.reshape(d, ATTN_HEADS * width)


def kernel(x, mix_norm_g, w_in, conv_qk, b_igate, b_fgate, mlstm_norm_g, w_out_attn, w_out_mlstm,
           w_o, ffn_norm_g, w_ffn_gate, w_ffn_up, w_ffn_down, final_norm_g):
    batch, seq, _ = x.shape
    depth = w_in.shape[0]
    assert seq % OUT_TILE == 0 and seq // MOBA_BLOCK <= N_BLOCK_COLS
    assert seq % (KEY_GROUP * MOBA_BLOCK) == 0
    slopes = _alibi_slopes()
    alibi = np.zeros((1, ATTN_AUG_WIDTH), np.float32)
    alibi[0, AUG - 1::AUG] = slopes
    alibi = jnp.asarray(alibi)
    slope_tab = jnp.asarray(np.broadcast_to(
        slopes.reshape(ATTN_HEADS // HEADS_PER_STEP, HEADS_PER_STEP, 1),
        (ATTN_HEADS // HEADS_PER_STEP, HEADS_PER_STEP, LANES)).copy())
    a0, a1, a2 = ATTN_WIDTH, 2 * ATTN_WIDTH, 3 * ATTN_WIDTH
    m1 = a2 + 4 * MLSTM_WIDTH
    g0 = m1 + 2 * MLSTM_HEADS

    outs = []
    for bi in range(batch):
        xb = x[bi]
        for layer in range(depth):
            w = w_in[layer]
            wq = w[:, :a0].T.astype(BF16)
            wk = _pad_heads(w[:, a0:a1], AUG).astype(BF16)
            wv = _pad_heads(w[:, a1:a2], V_ROWS).T.astype(BF16)
            wm = w[:, a2:m1].astype(BF16)
            wif = jnp.pad(w[:, m1:g0], ((0, 0), (0, LANES - 2 * MLSTM_HEADS))).astype(BF16)
            wg = w[:, g0:].astype(BF16)
            q, k_aug, v_aug, kmean, qkm, vom, gates, gg = _inproj(
                xb, mix_norm_g[layer][None, :], wq, wk, wv, wm, wif, wg, alibi)
            kmean = jnp.pad(kmean.reshape(kmean.shape[0], ATTN_AUG_WIDTH),
                            ((0, N_BLOCK_COLS - kmean.shape[0]), (0, 0)))
            ya = _moba(slope_tab, q, kmean, k_aug, v_aug)
            gbias = jnp.pad(jnp.concatenate([b_igate[layer], b_fgate[layer]])[None, :],
                            ((0, 0), (0, LANES - 2 * MLSTM_HEADS))).astype(F32)
            ym = _mlstm(qkm, vom, gates, conv_qk[layer], gbias, mlstm_norm_g[layer][None, :])
            xb = _out_ffn(xb, ya, ym, gg,
                          w_out_attn[layer].astype(BF16), w_out_mlstm[layer].astype(BF16),
                          w_o[layer].astype(BF16), ffn_norm_g[layer][None, :],
                          w_ffn_gate[layer].astype(BF16), w_ffn_up[layer].astype(BF16),
                          w_ffn_down[layer].astype(BF16), final_norm_g[None, :],
                          final_norm=(layer == depth - 1))
        outs.append(xb)
    return outs[0][None] if batch == 1 else jnp.stack(outs, axis=0)
```

```python
import functools

import numpy as np
import jax
import jax.numpy as jnp
from jax import lax
from jax.experimental import pallas as pl
from jax.experimental.pallas import tpu as pltpu

D_MODEL = 1024
ATTN_HEADS = 8
ATTN_HEAD_DIM = 64
ATTN_WIDTH = ATTN_HEADS * ATTN_HEAD_DIM
MOBA_BLOCK = 256
MOBA_TOP_K = 3
MLSTM_HEADS = 4
MLSTM_HEAD_DIM = 128
MLSTM_WIDTH = MLSTM_HEADS * MLSTM_HEAD_DIM
MLSTM_CHUNK = 128
CONV_WIDTH = 4
FFN_HIDDEN = 2816
NORM_EPS = 1e-6

LANES = 128
AUG = 2 * ATTN_HEAD_DIM
ATTN_AUG_WIDTH = ATTN_HEADS * AUG
V_ROWS = 80
N_BLOCK_COLS = AUG - ATTN_HEAD_DIM
HEADS_PER_STEP = 2
MLSTM_STEP_CHUNKS = 2
CONV_HALO = 8
Q_TILES = 2
KEY_GROUP = 2
NEG_BIAS = -1e9
FFN_CHUNK = 256
OUT_TILE = 512
VMEM_LIMIT = 56 * 1024 * 1024

F32 = jnp.float32
BF16 = jnp.bfloat16
NT_DIMS = (((1,), (1,)), ((), ()))


def _rms(x, g):
    return x * lax.rsqrt(jnp.mean(x * x, axis=-1, keepdims=True) + NORM_EPS) * g


def _sigmoid(x):
    return 1.0 / (1.0 + jnp.exp(-x))


def _inproj_kernel(x_ref, g_ref, wq_ref, wk_ref, wv_ref, wm_ref, wif_ref, wg_ref, alibi_ref,
                   q_ref, k_ref, v_ref, kmean_ref, qkm_ref, vom_ref, gates_ref, gg_ref):
    blk = pl.program_id(0)
    hb = _rms(x_ref[...], g_ref[...]).astype(BF16)
    q_ref[0] = lax.dot_general(wq_ref[...], hb, NT_DIMS, preferred_element_type=F32)
    kf = jnp.dot(hb, wk_ref[...], preferred_element_type=F32)
    kmean_ref[0] = jnp.mean(kf, axis=0, keepdims=True)
    lane = lax.broadcasted_iota(jnp.int32, (1, ATTN_AUG_WIDTH), 1) & (AUG - 1)
    row = lax.broadcasted_iota(jnp.int32, (MOBA_BLOCK, 1), 0).astype(F32)
    onehot = jnp.where(lane - ATTN_HEAD_DIM == blk, 1.0, 0.0)
    onehot = jnp.where(lane == AUG - 1, 0.0, onehot)
    k_ref[...] = (kf + onehot + alibi_ref[...] * row).astype(BF16)
    vt = lax.dot_general(wv_ref[...], hb, NT_DIMS, preferred_element_type=F32)
    feat = lax.rem(lax.broadcasted_iota(jnp.int32, (ATTN_HEADS * V_ROWS, 1), 0), V_ROWS)
    v_ref[0] = (vt + jnp.where(feat == ATTN_HEAD_DIM, 1.0, 0.0)).astype(BF16)
    m = jnp.dot(hb, wm_ref[...], preferred_element_type=F32)
    qkm_ref[...] = m[:, :2 * MLSTM_WIDTH]
    vom_ref[...] = m[:, 2 * MLSTM_WIDTH:]
    gates_ref[...] = jnp.dot(hb, wif_ref[...], preferred_element_type=F32)
    gg_ref[...] = jnp.dot(hb, wg_ref[...], preferred_element_type=F32)


def _inproj(x, g, wq, wk, wv, wm, wif, wg, alibi):
    s = x.shape[0]
    nb = s // MOBA_BLOCK
    row = lambda w: pl.BlockSpec((MOBA_BLOCK, w), lambda i: (i, 0))
    whole = pl.BlockSpec(memory_space=pltpu.VMEM)
    return pl.pallas_call(
        _inproj_kernel,
        grid=(nb,),
        in_specs=[row(D_MODEL), whole, whole, whole, whole, whole, whole, whole, whole],
        out_specs=[pl.BlockSpec((1, ATTN_WIDTH, MOBA_BLOCK), lambda i: (i, 0, 0)),
                   row(ATTN_AUG_WIDTH),
                   pl.BlockSpec((1, ATTN_HEADS * V_ROWS, MOBA_BLOCK), lambda i: (i, 0, 0)),
                   pl.BlockSpec((1, 1, ATTN_AUG_WIDTH), lambda i: (i, 0, 0)),
                   row(2 * MLSTM_WIDTH), row(2 * MLSTM_WIDTH), row(LANES), row(2 * D_MODEL)],
        out_shape=[jax.ShapeDtypeStruct((nb, ATTN_WIDTH, MOBA_BLOCK), F32),
                   jax.ShapeDtypeStruct((s, ATTN_AUG_WIDTH), BF16),
                   jax.ShapeDtypeStruct((nb, ATTN_HEADS * V_ROWS, MOBA_BLOCK), BF16),
                   jax.ShapeDtypeStruct((nb, 1, ATTN_AUG_WIDTH), F32),
                   jax.ShapeDtypeStruct((s, 2 * MLSTM_WIDTH), F32),
                   jax.ShapeDtypeStruct((s, 2 * MLSTM_WIDTH), F32),
                   jax.ShapeDtypeStruct((s, LANES), F32),
                   jax.ShapeDtypeStruct((s, 2 * D_MODEL), F32)],
        compiler_params=pltpu.CompilerParams(dimension_semantics=("arbitrary",),
                                             vmem_limit_bytes=VMEM_LIMIT),
        name="inproj",
    )(x, g, wq, wk, wv, wm, wif, wg, alibi)


def _col_max(st):
    rows, n = st.shape
    fan = 8
    while rows > fan * 8 and rows % fan == 0:
        st = jnp.max(st.reshape(fan, rows // fan, n), axis=0)
        rows //= fan
    return jnp.max(st, axis=0, keepdims=True)


def _moba_kernel(slope_ref, q_ref, qn_ref, kmean_ref, k_ref, vt_ref, o_ref, s_ref, qa_ref, cm_ref):
    step_id = pl.program_id(1)
    n_steps = pl.num_programs(1)
    i = step_id * Q_TILES
    tq = Q_TILES * MOBA_BLOCK
    nb = N_BLOCK_COLS
    group_rows = KEY_GROUP * MOBA_BLOCK
    assert KEY_GROUP % Q_TILES == 0
    heads = range(HEADS_PER_STEP)
    head_lanes = [slice(hh * AUG, (hh + 1) * AUG) for hh in heads]
    head_rows = [slice(hh * V_ROWS, (hh + 1) * V_ROWS) for hh in heads]
    own_slot = 2

    def prepare(qsrc_ref, tile0):
        blk = lax.broadcasted_iota(jnp.int32, (nb, tq), 0).astype(F32)
        i_f = (tile0 + lax.broadcasted_iota(jnp.int32, (1, tq), 1) // MOBA_BLOCK).astype(F32)
        for hh in heads:
            slope = slope_ref[0, hh:hh + 1, 0:1]
            qh = jnp.concatenate([qsrc_ref[t, hh * ATTN_HEAD_DIM:(hh + 1) * ATTN_HEAD_DIM, :]
                                  for t in range(Q_TILES)], axis=1)
            km = kmean_ref[:, hh * AUG:hh * AUG + ATTN_HEAD_DIM]
            gate = jnp.dot(km, qh, precision=lax.Precision.HIGHEST, preferred_element_type=F32)
            valid = blk < i_f
            g = jnp.where(valid, gate, -jnp.inf)
            sel = jnp.zeros((nb, tq), jnp.bool_)
            for _ in range(MOBA_TOP_K):
                mx = jnp.max(g, axis=0, keepdims=True)
                idx = jnp.min(jnp.where(g == mx, blk, float(nb)), axis=0, keepdims=True)
                pick = blk == idx
                sel = jnp.logical_or(sel, pick)
                g = jnp.where(pick, -jnp.inf, g)
            bias = jnp.where(sel, (blk - i_f) * (slope * float(MOBA_BLOCK)), NEG_BIAS)
            bias = jnp.where(valid, bias, 0.0)
            bias = jnp.where(blk == float(nb - 1), 1.0, bias)
            qa_ref[hh] = jnp.concatenate([qh * (ATTN_HEAD_DIM ** -0.5), bias], axis=0).astype(BF16)
        first = (tile0 // KEY_GROUP) * KEY_GROUP
        hidden = (lax.broadcasted_iota(jnp.int32, (group_rows, tq), 0)
                  - lax.broadcasted_iota(jnp.int32, (group_rows, tq), 1)) > (tile0 - first) * MOBA_BLOCK
        start = pl.multiple_of(first * MOBA_BLOCK, group_rows)
        for hh in heads:
            st = jnp.dot(k_ref[pl.ds(start, group_rows), head_lanes[hh]], qa_ref[hh], preferred_element_type=F32)
            st = jnp.where(hidden, -jnp.inf, st)
            s_ref[own_slot, hh] = st
            cm_ref[hh] = jnp.broadcast_to(_col_max(st), (8, tq))

    def score(first, slot):
        start = pl.multiple_of(first * MOBA_BLOCK, group_rows)
        col_max = []
        for hh in heads:
            st = jnp.dot(k_ref[pl.ds(start, group_rows), head_lanes[hh]], qa_ref[hh], preferred_element_type=F32)
            s_ref[slot, hh] = st
            col_max.append(_col_max(st))
        return col_max

    def consume(first, slot, m_prev, m_cur, accs):
        ps = [jnp.exp(s_ref[slot, hh] - m_cur[hh]).astype(BF16) for hh in heads]
        accs = [jnp.exp(m_prev[hh] - m_cur[hh]) * accs[hh] for hh in heads]
        for b in range(KEY_GROUP):
            for hh in heads:
                accs[hh] = accs[hh] + jnp.dot(vt_ref[first + b, head_rows[hh], :],
                                              ps[hh][b * MOBA_BLOCK:(b + 1) * MOBA_BLOCK, :],
                                              preferred_element_type=F32)
        return accs

    def step(first, slot, src, carry):
        pend, m_prev, m_cur, accs = carry
        cm = score(first, slot)
        accs = consume(pend, src, m_prev, m_cur, accs)
        return (jnp.asarray(first, jnp.int32), m_cur,
                [jnp.maximum(m_cur[hh], cm[hh]) for hh in heads], accs)

    def pair(first_group, carry, src=0):
        g = first_group * KEY_GROUP
        return step(g + KEY_GROUP, 0, 1, step(g, 1, src, carry))

    @pl.when(step_id == 0)
    def _():
        prepare(q_ref, i)

    n_past = i // KEY_GROUP
    own_first = n_past * KEY_GROUP
    cm0 = [cm_ref[hh, 0:1, :] for hh in heads]
    init = (jnp.asarray(own_first, jnp.int32), cm0, cm0, [jnp.zeros((V_ROWS, tq), F32) for _ in heads])

    def start_none(c):
        for hh in heads:
            s_ref[0, hh] = s_ref[own_slot, hh]
        return c

    def start_odd(c):
        return step(0, 0, own_slot, c)

    def start_even(c):
        return pair(0, c, src=own_slot)

    odd = n_past % 2
    carry = lax.cond(n_past == 0, start_none,
                     lambda c: lax.cond(odd == 1, start_odd, start_even, c), init)
    done = jnp.where(n_past == 0, 0, 2 - odd)
    n_pairs = (n_past - done) // 2
    odd_pair = n_pairs % 2
    carry = lax.cond(odd_pair == 1, functools.partial(pair, done), lambda c: c, carry)
    done = done + 2 * odd_pair

    def quad(u, carry):
        return pair(done + 4 * u + 2, pair(done + 4 * u, carry))

    pend, m_prev, m_cur, accs = lax.fori_loop(0, n_pairs // 2, quad, carry)
    accs = consume(pend, 0, m_prev, m_cur, accs)
    prepare(qn_ref, jnp.minimum(step_id + 1, n_steps - 1) * Q_TILES)
    outs = []
    for hh in heads:
        ot = accs[hh][:ATTN_HEAD_DIM, :] / accs[hh][ATTN_HEAD_DIM:ATTN_HEAD_DIM + 1, :]
        outs.append(ot.T)
    o_ref[...] = jnp.concatenate(outs, axis=1).astype(o_ref.dtype)


def _moba(slope_tab, qt, kmean, k_aug, vt_aug):
    s = k_aug.shape[0]
    nb = s // MOBA_BLOCK
    n_pairs = ATTN_HEADS // HEADS_PER_STEP
    n_steps = nb // Q_TILES
    tq = Q_TILES * MOBA_BLOCK
    q_block = (Q_TILES, HEADS_PER_STEP * ATTN_HEAD_DIM, MOBA_BLOCK)
    return pl.pallas_call(
        _moba_kernel,
        grid=(n_pairs, n_steps),
        in_specs=[pl.BlockSpec((1, HEADS_PER_STEP, LANES), lambda p, i: (p, 0, 0)),
                  pl.BlockSpec(q_block, lambda p, i: (i, p, 0)),
                  pl.BlockSpec(q_block, lambda p, i: (jnp.minimum(i + 1, n_steps - 1), p, 0)),
                  pl.BlockSpec((N_BLOCK_COLS, HEADS_PER_STEP * AUG), lambda p, i: (0, p)),
                  pl.BlockSpec((s, HEADS_PER_STEP * AUG), lambda p, i: (0, p)),
                  pl.BlockSpec((nb, HEADS_PER_STEP * V_ROWS, MOBA_BLOCK), lambda p, i: (0, p, 0))],
        out_specs=pl.BlockSpec((tq, HEADS_PER_STEP * ATTN_HEAD_DIM), lambda p, i: (i, p)),
        out_shape=jax.ShapeDtypeStruct((s, ATTN_WIDTH), BF16),
        scratch_shapes=[pltpu.VMEM((3, HEADS_PER_STEP, KEY_GROUP * MOBA_BLOCK, tq), F32),
                        pltpu.VMEM((HEADS_PER_STEP, AUG, tq), BF16),
                        pltpu.VMEM((HEADS_PER_STEP, 8, tq), F32)],
        compiler_params=pltpu.CompilerParams(dimension_semantics=("arbitrary", "arbitrary"),
                                             vmem_limit_bytes=VMEM_LIMIT),
        name="moba",
    )(slope_tab, qt, qt, kmean, k_aug, vt_aug)


def _mlstm_kernel(qk_ref, vo_ref, gates_ref, conv_ref, gbias_ref, ng_ref, y_ref,
                  xbuf, c_ref, n_ref, m_ref):
    L = MLSTM_CHUNK
    d = MLSTM_HEAD_DIM
    rows = qk_ref.shape[0]
    halo = xbuf.shape[0] - rows
    chunks = range(rows // L)
    heads = range(MLSTM_HEADS)

    @pl.when(pl.program_id(0) == 0)
    def _():
        xbuf[0:halo, :] = jnp.zeros((halo, xbuf.shape[1]), F32)
        c_ref[...] = jnp.zeros(c_ref.shape, F32)
        n_ref[...] = jnp.zeros(n_ref.shape, F32)
        m_ref[...] = jnp.zeros(m_ref.shape, F32)

    xbuf[halo:, :] = qk_ref[...]
    w = conv_ref[...]
    y = qk_ref[...] * w[CONV_WIDTH - 1:CONV_WIDTH, :]
    for j in range(CONV_WIDTH - 1):
        off = halo - CONV_WIDTH + 1 + j
        y = y + xbuf[off:off + rows, :] * w[j:j + 1, :]
    xbuf[0:halo, :] = xbuf[rows:rows + halo, :]
    qk = y * _sigmoid(y)

    a = gates_ref[...] + gbias_ref[...]
    lane = lax.broadcasted_iota(jnp.int32, (rows, LANES), 1)
    is_f = jnp.logical_and(lane >= MLSTM_HEADS, lane < 2 * MLSTM_HEADS)
    log_f = jnp.where(is_f, jnp.minimum(a, 0.0) - jnp.log(1.0 + jnp.exp(-jnp.abs(a))), 0.0)
    t_io = lax.broadcasted_iota(jnp.int32, (rows, rows), 0)
    s_io = lax.broadcasted_iota(jnp.int32, (rows, rows), 1)
    same_chunk = (t_io // L) == (s_io // L)
    tri = jnp.where(jnp.logical_and(s_io <= t_io, same_chunk), 1.0, 0.0)
    bcum = jnp.dot(tri, log_f, precision=lax.Precision.HIGHEST,
                   preferred_element_type=F32)
    colm = jnp.where(lane < MLSTM_HEADS, a, bcum)
    rowm = colm.T
    causal = (lax.broadcasted_iota(jnp.int32, (L, L), 1) <= lax.broadcasted_iota(jnp.int32, (L, L), 0))

    pairs = [(cc, hd) for cc in chunks for hd in heads]
    rs = lambda cc: slice(cc * L, (cc + 1) * L)
    q = {(cc, hd): qk[rs(cc), hd * d:(hd + 1) * d] for cc, hd in pairs}
    k = {(cc, hd): qk[rs(cc), MLSTM_WIDTH + hd * d:MLSTM_WIDTH + (hd + 1) * d] * (d ** -0.5) for cc, hd in pairs}
    qb = {p: q[p].astype(BF16) for p in pairs}
    kb = {p: k[p].astype(BF16) for p in pairs}
    vb = {(cc, hd): vo_ref[rs(cc), hd * d:(hd + 1) * d].astype(BF16) for cc, hd in pairs}
    ic_c = {(cc, hd): colm[rs(cc), hd:hd + 1] for cc, hd in pairs}
    bc_c = {(cc, hd): colm[rs(cc), MLSTM_HEADS + hd:MLSTM_HEADS + hd + 1] for cc, hd in pairs}
    ic_r = {(cc, hd): rowm[hd:hd + 1, rs(cc)] for cc, hd in pairs}
    bc_r = {(cc, hd): rowm[MLSTM_HEADS + hd:MLSTM_HEADS + hd + 1, rs(cc)] for cc, hd in pairs}
    dmat = {p: jnp.where(causal, bc_c[p] - bc_r[p] + ic_r[p], -jnp.inf) for p in pairs}
    a_t = {p: jnp.max(dmat[p], axis=1, keepdims=True) for p in pairs}
    sc = {p: lax.dot_general(qb[p], kb[p], NT_DIMS, preferred_element_type=F32) * jnp.exp(dmat[p] - a_t[p])
          for p in pairs}
    sv = {p: jnp.dot(sc[p].astype(BF16), vb[p], preferred_element_type=F32) for p in pairs}
    ssum = {p: jnp.sum(sc[p], axis=1, keepdims=True) for p in pairs}
    b_last = {p: bc_c[p][L - 1:L, :] for p in pairs}
    g_c = {p: b_last[p] - bc_c[p] + ic_c[p] for p in pairs}
    gmax = {p: jnp.max(g_c[p], axis=0, keepdims=True) for p in pairs}
    kw = {p: k[p] * jnp.exp(g_c[p] - gmax[p]) for p in pairs}
    upd = {p: jnp.dot(kw[p].T.astype(BF16), vb[p], preferred_element_type=F32) for p in pairs}
    ksum = {p: jnp.sum(kw[p], axis=0, keepdims=True) for p in pairs}
    h_out = {}
    for hd in heads:
        c_st, n_st, m_prev = c_ref[hd], n_ref[hd], m_ref[hd:hd + 1, 0:1]
        for cc in chunks:
            p = (cc, hd)
            inter = bc_c[p] + m_prev
            m_t = jnp.maximum(inter, a_t[p])
            r_t = jnp.exp(a_t[p] - m_t)
            w_inter = jnp.exp(inter - m_t)
            num = w_inter * jnp.dot(qb[p], c_st.astype(BF16), preferred_element_type=F32) + r_t * sv[p]
            den = w_inter * jnp.sum(q[p] * n_st, axis=1, keepdims=True) + r_t * ssum[p]
            h_out[p] = num / jnp.maximum(jnp.abs(den), jnp.exp(-m_t))
            m_new = jnp.maximum(b_last[p] + m_prev, gmax[p])
            decay = jnp.exp(b_last[p] + m_prev - m_new)
            gain = jnp.exp(gmax[p] - m_new)
            c_st = decay * c_st + gain * upd[p]
            n_st = decay * n_st + gain * ksum[p]
            m_prev = m_new
        c_ref[hd], n_ref[hd] = c_st, n_st
        m_ref[hd:hd + 1, :] = jnp.broadcast_to(m_prev, (1, LANES))
    for cc, hd in pairs:
        hs = slice(hd * d, (hd + 1) * d)
        og = vo_ref[rs(cc), MLSTM_WIDTH + hd * d:MLSTM_WIDTH + (hd + 1) * d]
        hg = h_out[(cc, hd)] * _sigmoid(og)
        mu = jnp.mean(hg, axis=1, keepdims=True)
        var = jnp.mean(jnp.square(hg - mu), axis=1, keepdims=True)
        y_ref[rs(cc), hs] = ((hg - mu) * lax.rsqrt(var + NORM_EPS) * ng_ref[:, hs]).astype(y_ref.dtype)


def _mlstm(qkm, vom, gates, conv, gbias, ng):
    s = qkm.shape[0]
    rows = MLSTM_STEP_CHUNKS * MLSTM_CHUNK
    row = lambda w: pl.BlockSpec((rows, w), lambda t: (t, 0))
    whole = pl.BlockSpec(memory_space=pltpu.VMEM)
    return pl.pallas_call(
        _mlstm_kernel,
        grid=(s // rows,),
        in_specs=[row(2 * MLSTM_WIDTH), row(2 * MLSTM_WIDTH), row(LANES), whole, whole, whole],
        out_specs=row(MLSTM_WIDTH),
        out_shape=jax.ShapeDtypeStruct((s, MLSTM_WIDTH), BF16),
        scratch_shapes=[pltpu.VMEM((rows + CONV_HALO, 2 * MLSTM_WIDTH), F32),
                        pltpu.VMEM((MLSTM_HEADS, MLSTM_HEAD_DIM, MLSTM_HEAD_DIM), F32),
                        pltpu.VMEM((MLSTM_HEADS, 1, MLSTM_HEAD_DIM), F32),
                        pltpu.VMEM((8, LANES), F32)],
        compiler_params=pltpu.CompilerParams(dimension_semantics=("arbitrary",),
                                             vmem_limit_bytes=VMEM_LIMIT),
        name="mlstm",
    )(qkm, vom, gates, conv, gbias, ng)


def _out_ffn_kernel(x_ref, ya_ref, ym_ref, gg_ref, woa_ref, wom_ref, wo_ref, fg_ref,
                    wgt_ref, wup_ref, wdn_ref, fin_ref, o_ref, *, final_norm):
    a = jnp.dot(ya_ref[...], woa_ref[...], preferred_element_type=F32)
    b = jnp.dot(ym_ref[...], wom_ref[...], preferred_element_type=F32)
    merged = _sigmoid(gg_ref[:, :D_MODEL]) * a + _sigmoid(gg_ref[:, D_MODEL:]) * b
    x1 = x_ref[...] + jnp.dot(merged.astype(BF16), wo_ref[...], preferred_element_type=F32)
    h2 = _rms(x1, fg_ref[...]).astype(BF16)
    acc = jnp.zeros(x1.shape, F32)
    for c in range(0, FFN_HIDDEN, FFN_CHUNK):
        gt = jnp.dot(h2, wgt_ref[:, c:c + FFN_CHUNK], preferred_element_type=F32)
        up = jnp.dot(h2, wup_ref[:, c:c + FFN_CHUNK], preferred_element_type=F32)
        act = (gt * _sigmoid(gt) * up).astype(BF16)
        acc = acc + jnp.dot(act, wdn_ref[c:c + FFN_CHUNK, :], preferred_element_type=F32)
    x2 = x1 + acc
    o_ref[...] = _rms(x2, fin_ref[...]) if final_norm else x2


def _out_ffn(x, ya, ym, gg, woa, wom, wo, fg, wgt, wup, wdn, fin, final_norm):
    s = x.shape[0]
    row = lambda w: pl.BlockSpec((OUT_TILE, w), lambda i: (i, 0))
    whole = pl.BlockSpec(memory_space=pltpu.VMEM)
    return pl.pallas_call(
        functools.partial(_out_ffn_kernel, final_norm=final_norm),
        grid=(s // OUT_TILE,),
        in_specs=[row(D_MODEL), row(ATTN_WIDTH), row(MLSTM_WIDTH), row(2 * D_MODEL)] + [whole] * 8,
        out_specs=row(D_MODEL),
        out_shape=jax.ShapeDtypeStruct((s, D_MODEL), F32),
        compiler_params=pltpu.CompilerParams(dimension_semantics=("arbitrary",),
                                             vmem_limit_bytes=VMEM_LIMIT),
        name="out_ffn",
    )(x, ya, ym, gg, woa, wom, wo, fg, wgt, wup, wdn, fin)


def _alibi_slopes():
    return np.exp2(-8.0 * np.arange(1, ATTN_HEADS + 1, dtype=np.float64) / ATTN_HEADS).astype(np.float32)


def _pad_heads(w, width):
    d = w.shape[0]
    w = w.reshape(d, ATTN_HEADS, ATTN_HEAD_DIM)
    w = jnp.pad(w, ((0, 0), (0, 0), (0, width - ATTN_HEAD_DIM)))
    return w.reshape(d, ATTN_HEADS * width)


def kernel(x, mix_norm_g, w_in, conv_qk, b_igate, b_fgate, mlstm_norm_g, w_out_attn, w_out_mlstm,
           w_o, ffn_norm_g, w_ffn_gate, w_ffn_up, w_ffn_down, final_norm_g):
    batch, seq, _ = x.shape
    depth = w_in.shape[0]
    assert seq % OUT_TILE == 0 and seq // MOBA_BLOCK <= N_BLOCK_COLS
    assert seq % (KEY_GROUP * MOBA_BLOCK) == 0
    slopes = _alibi_slopes()
    alibi = np.zeros((1, ATTN_AUG_WIDTH), np.float32)
    alibi[0, AUG - 1::AUG] = slopes
    alibi = jnp.asarray(alibi)
    slope_tab = jnp.asarray(np.broadcast_to(
        slopes.reshape(ATTN_HEADS // HEADS_PER_STEP, HEADS_PER_STEP, 1),
        (ATTN_HEADS // HEADS_PER_STEP, HEADS_PER_STEP, LANES)).copy())
    a0, a1, a2 = ATTN_WIDTH, 2 * ATTN_WIDTH, 3 * ATTN_WIDTH
    m1 = a2 + 4 * MLSTM_WIDTH
    g0 = m1 + 2 * MLSTM_HEADS

    outs = []
    for bi in range(batch):
        xb = x[bi]
        for layer in range(depth):
            w = w_in[layer]
            wq = w[:, :a0].T.astype(BF16)
            wk = _pad_heads(w[:, a0:a1], AUG).astype(BF16)
            wv = _pad_heads(w[:, a1:a2], V_ROWS).T.astype(BF16)
            wm = w[:, a2:m1].astype(BF16)
            wif = jnp.pad(w[:, m1:g0], ((0, 0), (0, LANES - 2 * MLSTM_HEADS))).astype(BF16)
            wg = w[:, g0:].astype(BF16)
            q, k_aug, v_aug, kmean, qkm, vom, gates, gg = _inproj(
                xb, mix_norm_g[layer][None, :], wq, wk, wv, wm, wif, wg, alibi)
            kmean = jnp.pad(kmean.reshape(kmean.shape[0], ATTN_AUG_WIDTH),
                            ((0, N_BLOCK_COLS - kmean.shape[0]), (0, 0)))
            ya = _moba(slope_tab, q, kmean, k_aug, v_aug)
            gbias = jnp.pad(jnp.concatenate([b_igate[layer], b_fgate[layer]])[None, :],
                            ((0, 0), (0, LANES - 2 * MLSTM_HEADS))).astype(F32)
            ym = _mlstm(qkm, vom, gates, conv_qk[layer], gbias, mlstm_norm_g[layer][None, :])
            xb = _out_ffn(xb, ya, ym, gg,
                          w_out_attn[layer].astype(BF16), w_out_mlstm[layer].astype(BF16),
                          w_o[layer].astype(BF16), ffn_norm_g[layer][None, :],
                          w_ffn_gate[layer].astype(BF16), w_ffn_up[layer].astype(BF16),
                          w_ffn_down[layer].astype(BF16), final_norm_g[None, :],
                          final_norm=(layer == depth - 1))
        outs.append(xb)
    return outs[0][None] if batch == 1 else jnp.stack(outs, axis=0)
```

```python
import functools

import numpy as np
import jax
import jax.numpy as jnp
from jax import lax
from jax.experimental import pallas as pl
from jax.experimental.pallas import tpu as pltpu

D_MODEL = 1024
ATTN_HEADS = 8
ATTN_HEAD_DIM = 64
ATTN_WIDTH = ATTN_HEADS * ATTN_HEAD_DIM
MOBA_BLOCK = 256
MOBA_TOP_K = 3
MLSTM_HEADS = 4
MLSTM_HEAD_DIM = 128
MLSTM_WIDTH = MLSTM_HEADS * MLSTM_HEAD_DIM
MLSTM_CHUNK = 128
CONV_WIDTH = 4
FFN_HIDDEN = 2816
NORM_EPS = 1e-6

LANES = 128
AUG = 2 * ATTN_HEAD_DIM
ATTN_AUG_WIDTH = ATTN_HEADS * AUG
V_ROWS = 80
N_BLOCK_COLS = AUG - ATTN_HEAD_DIM
HEADS_PER_STEP = 2
MLSTM_STEP_CHUNKS = 2
CONV_HALO = 8
Q_TILES = 2
KEY_GROUP = 2
NEGLIGIBLE_LOG = 50.0
NORM_SLACK = 1.02
NEG_BIAS = -1e9
FFN_CHUNK = 256
OUT_TILE = 512
VMEM_LIMIT = 56 * 1024 * 1024

F32 = jnp.float32
BF16 = jnp.bfloat16
NT_DIMS = (((1,), (1,)), ((), ()))


def _rms(x, g):
    return x * lax.rsqrt(jnp.mean(x * x, axis=-1, keepdims=True) + NORM_EPS) * g


def _sigmoid(x):
    return 1.0 / (1.0 + jnp.exp(-x))


def _inproj_kernel(x_ref, g_ref, wq_ref, wk_ref, wv_ref, wm_ref, wif_ref, wg_ref, alibi_ref,
                   q_ref, k_ref, v_ref, kmean_ref, qkm_ref, vom_ref, gates_ref, gg_ref):
    blk = pl.program_id(0)
    hb = _rms(x_ref[...], g_ref[...]).astype(BF16)
    q_ref[0] = lax.dot_general(wq_ref[...], hb, NT_DIMS, preferred_element_type=F32)
    kf = jnp.dot(hb, wk_ref[...], preferred_element_type=F32)
    ksq = kf * kf
    norm2 = [jnp.max(jnp.sum(ksq[:, h * AUG:(h + 1) * AUG], axis=1, keepdims=True), axis=0, keepdims=True)
             for h in range(ATTN_HEADS)]
    kmean_ref[0] = jnp.concatenate(
        [jnp.mean(kf, axis=0, keepdims=True),
         jnp.concatenate([jnp.broadcast_to(n2, (1, AUG)) for n2 in norm2], axis=1)], axis=0)
    lane = lax.broadcasted_iota(jnp.int32, (1, ATTN_AUG_WIDTH), 1) & (AUG - 1)
    row = lax.broadcasted_iota(jnp.int32, (MOBA_BLOCK, 1), 0).astype(F32)
    onehot = jnp.where(lane - ATTN_HEAD_DIM == blk, 1.0, 0.0)
    onehot = jnp.where(lane == AUG - 1, 0.0, onehot)
    k_ref[...] = (kf + onehot + alibi_ref[...] * row).astype(BF16)
    vt = lax.dot_general(wv_ref[...], hb, NT_DIMS, preferred_element_type=F32)
    feat = lax.rem(lax.broadcasted_iota(jnp.int32, (ATTN_HEADS * V_ROWS, 1), 0), V_ROWS)
    v_ref[0] = (vt + jnp.where(feat == ATTN_HEAD_DIM, 1.0, 0.0)).astype(BF16)
    m = jnp.dot(hb, wm_ref[...], preferred_element_type=F32)
    qkm_ref[...] = m[:, :2 * MLSTM_WIDTH]
    vom_ref[...] = m[:, 2 * MLSTM_WIDTH:]
    gates_ref[...] = jnp.dot(hb, wif_ref[...], preferred_element_type=F32)
    gg_ref[...] = jnp.dot(hb, wg_ref[...], preferred_element_type=F32)


def _inproj(x, g, wq, wk, wv, wm, wif, wg, alibi):
    s = x.shape[0]
    nb = s // MOBA_BLOCK
    row = lambda w: pl.BlockSpec((MOBA_BLOCK, w), lambda i: (i, 0))
    whole = pl.BlockSpec(memory_space=pltpu.VMEM)
    return pl.pallas_call(
        _inproj_kernel,
        grid=(nb,),
        in_specs=[row(D_MODEL), whole, whole, whole, whole, whole, whole, whole, whole],
        out_specs=[pl.BlockSpec((1, ATTN_WIDTH, MOBA_BLOCK), lambda i: (i, 0, 0)),
                   row(ATTN_AUG_WIDTH),
                   pl.BlockSpec((1, ATTN_HEADS * V_ROWS, MOBA_BLOCK), lambda i: (i, 0, 0)),
                   pl.BlockSpec((1, 2, ATTN_AUG_WIDTH), lambda i: (i, 0, 0)),
                   row(2 * MLSTM_WIDTH), row(2 * MLSTM_WIDTH), row(LANES), row(2 * D_MODEL)],
        out_shape=[jax.ShapeDtypeStruct((nb, ATTN_WIDTH, MOBA_BLOCK), F32),
                   jax.ShapeDtypeStruct((s, ATTN_AUG_WIDTH), BF16),
                   jax.ShapeDtypeStruct((nb, ATTN_HEADS * V_ROWS, MOBA_BLOCK), BF16),
                   jax.ShapeDtypeStruct((nb, 2, ATTN_AUG_WIDTH), F32),
                   jax.ShapeDtypeStruct((s, 2 * MLSTM_WIDTH), F32),
                   jax.ShapeDtypeStruct((s, 2 * MLSTM_WIDTH), F32),
                   jax.ShapeDtypeStruct((s, LANES), F32),
                   jax.ShapeDtypeStruct((s, 2 * D_MODEL), F32)],
        compiler_params=pltpu.CompilerParams(dimension_semantics=("arbitrary",),
                                             vmem_limit_bytes=VMEM_LIMIT),
        name="inproj",
    )(x, g, wq, wk, wv, wm, wif, wg, alibi)


def _col_max(st):
    rows, n = st.shape
    fan = 8
    while rows > fan * 8 and rows % fan == 0:
        st = jnp.max(st.reshape(fan, rows // fan, n), axis=0)
        rows //= fan
    return jnp.max(st, axis=0, keepdims=True)


def _moba_kernel(slope_ref, q_ref, qn_ref, kmean_ref, kn2_ref, k_ref, vt_ref, o_ref,
                 s_ref, qa_ref, cm_ref, reach_ref):
    step_id = pl.program_id(1)
    n_steps = pl.num_programs(1)
    i = step_id * Q_TILES
    tq = Q_TILES * MOBA_BLOCK
    nb = N_BLOCK_COLS
    group_rows = KEY_GROUP * MOBA_BLOCK
    assert KEY_GROUP % Q_TILES == 0
    heads = range(HEADS_PER_STEP)
    head_lanes = [slice(hh * AUG, (hh + 1) * AUG) for hh in heads]
    head_rows = [slice(hh * V_ROWS, (hh + 1) * V_ROWS) for hh in heads]
    own_slot = 2

    def prepare(qsrc_ref, tile0):
        blk = lax.broadcasted_iota(jnp.int32, (nb, tq), 0).astype(F32)
        i_f = (tile0 + lax.broadcasted_iota(jnp.int32, (1, tq), 1) // MOBA_BLOCK).astype(F32)
        reach = jnp.zeros((1, 1), F32)
        for hh in heads:
            slope = slope_ref[0, hh:hh + 1, 0:1]
            qh = jnp.concatenate([qsrc_ref[t, hh * ATTN_HEAD_DIM:(hh + 1) * ATTN_HEAD_DIM, :]
                                  for t in range(Q_TILES)], axis=1)
            km = kmean_ref[:, hh * AUG:hh * AUG + ATTN_HEAD_DIM]
            gate = jnp.dot(km, qh, precision=lax.Precision.HIGHEST, preferred_element_type=F32)
            valid = blk < i_f
            g = jnp.where(valid, gate, -jnp.inf)
            sel = jnp.zeros((nb, tq), jnp.bool_)
            for _ in range(MOBA_TOP_K):
                mx = jnp.max(g, axis=0, keepdims=True)
                idx = jnp.min(jnp.where(g == mx, blk, float(nb)), axis=0, keepdims=True)
                pick = blk == idx
                sel = jnp.logical_or(sel, pick)
                g = jnp.where(pick, -jnp.inf, g)
            bias = jnp.where(sel, (blk - i_f) * (slope * float(MOBA_BLOCK)), NEG_BIAS)
            bias = jnp.where(valid, bias, 0.0)
            bias = jnp.where(blk == float(nb - 1), 1.0, bias)
            qa_ref[hh] = jnp.concatenate([qh * (ATTN_HEAD_DIM ** -0.5), bias], axis=0).astype(BF16)
            q_norm = jnp.sqrt(jnp.max(jnp.sum(qh * qh, axis=0, keepdims=True), axis=1, keepdims=True))
            k_norm = jnp.sqrt(jnp.max(kn2_ref[:, hh * AUG:hh * AUG + 1], axis=0, keepdims=True))
            spread = 2.0 * NORM_SLACK * q_norm * k_norm * (ATTN_HEAD_DIM ** -0.5)
            reach = jnp.maximum(reach, (NEGLIGIBLE_LOG + spread) / (slope * float(MOBA_BLOCK)) + 1.0)
        reach = jnp.ceil(reach)
        reach = jnp.where(reach < float(nb), reach, float(nb))
        reach_ref[...] = jnp.broadcast_to(reach, reach_ref.shape).astype(jnp.int32)
        first = (tile0 // KEY_GROUP) * KEY_GROUP
        hidden = (lax.broadcasted_iota(jnp.int32, (group_rows, tq), 0)
                  - lax.broadcasted_iota(jnp.int32, (group_rows, tq), 1)) > (tile0 - first) * MOBA_BLOCK
        start = pl.multiple_of(first * MOBA_BLOCK, group_rows)
        for hh in heads:
            st = jnp.dot(k_ref[pl.ds(start, group_rows), head_lanes[hh]], qa_ref[hh], preferred_element_type=F32)
            st = jnp.where(hidden, -jnp.inf, st)
            s_ref[own_slot, hh] = st
            cm_ref[hh] = jnp.broadcast_to(_col_max(st), (8, tq))

    def score(first, slot):
        start = pl.multiple_of(first * MOBA_BLOCK, group_rows)
        col_max = []
        for hh in heads:
            st = jnp.dot(k_ref[pl.ds(start, group_rows), head_lanes[hh]], qa_ref[hh], preferred_element_type=F32)
            s_ref[slot, hh] = st
            col_max.append(_col_max(st))
        return col_max

    def consume(first, slot, m_prev, m_cur, accs):
        ps = [jnp.exp(s_ref[slot, hh] - m_cur[hh]).astype(BF16) for hh in heads]
        accs = [jnp.exp(m_prev[hh] - m_cur[hh]) * accs[hh] for hh in heads]
        for b in range(KEY_GROUP):
            for hh in heads:
                accs[hh] = accs[hh] + jnp.dot(vt_ref[first + b, head_rows[hh], :],
                                              ps[hh][b * MOBA_BLOCK:(b + 1) * MOBA_BLOCK, :],
                                              preferred_element_type=F32)
        return accs

    def step(first, slot, src, carry):
        pend, m_prev, m_cur, accs = carry
        cm = score(first, slot)
        accs = consume(pend, src, m_prev, m_cur, accs)
        return (jnp.asarray(first, jnp.int32), m_cur,
                [jnp.maximum(m_cur[hh], cm[hh]) for hh in heads], accs)

    def pair(first_group, carry, src=0):
        g = first_group * KEY_GROUP
        return step(g + KEY_GROUP, 0, 1, step(g, 1, src, carry))

    @pl.when(step_id == 0)
    def _():
        prepare(q_ref, i)

    own_first = (i // KEY_GROUP) * KEY_GROUP
    first_block = jnp.maximum(i - jnp.max(reach_ref[...]) + 1, 0)
    first_group = first_block // KEY_GROUP
    n_past = i // KEY_GROUP - first_group
    cm0 = [cm_ref[hh, 0:1, :] for hh in heads]
    init = (jnp.asarray(own_first, jnp.int32), cm0, cm0, [jnp.zeros((V_ROWS, tq), F32) for _ in heads])

    def start_none(c):
        for hh in heads:
            s_ref[0, hh] = s_ref[own_slot, hh]
        return c

    def start_odd(c):
        return step(first_group * KEY_GROUP, 0, own_slot, c)

    def start_even(c):
        return pair(first_group, c, src=own_slot)

    odd = n_past % 2
    carry = lax.cond(n_past == 0, start_none,
                     lambda c: lax.cond(odd == 1, start_odd, start_even, c), init)
    done = first_group + jnp.where(n_past == 0, 0, 2 - odd)
    n_pairs = (i // KEY_GROUP - done) // 2
    odd_pair = n_pairs % 2
    carry = lax.cond(odd_pair == 1, functools.partial(pair, done), lambda c: c, carry)
    done = done + 2 * odd_pair

    def quad(u, carry):
        return pair(done + 4 * u + 2, pair(done + 4 * u, carry))

    pend, m_prev, m_cur, accs = lax.fori_loop(0, n_pairs // 2, quad, carry)
    accs = consume(pend, 0, m_prev, m_cur, accs)
    prepare(qn_ref, jnp.minimum(step_id + 1, n_steps - 1) * Q_TILES)
    outs = []
    for hh in heads:
        ot = accs[hh][:ATTN_HEAD_DIM, :] / accs[hh][ATTN_HEAD_DIM:ATTN_HEAD_DIM + 1, :]
        outs.append(ot.T)
    o_ref[...] = jnp.concatenate(outs, axis=1).astype(o_ref.dtype)


def _moba(slope_tab, qt, kmean, knorm2, k_aug, vt_aug):
    s = k_aug.shape[0]
    nb = s // MOBA_BLOCK
    n_pairs = ATTN_HEADS // HEADS_PER_STEP
    n_steps = nb // Q_TILES
    tq = Q_TILES * MOBA_BLOCK
    q_block = (Q_TILES, HEADS_PER_STEP * ATTN_HEAD_DIM, MOBA_BLOCK)
    return pl.pallas_call(
        _moba_kernel,
        grid=(n_pairs, n_steps),
        in_specs=[pl.BlockSpec((1, HEADS_PER_STEP, LANES), lambda p, i: (p, 0, 0)),
                  pl.BlockSpec(q_block, lambda p, i: (i, p, 0)),
                  pl.BlockSpec(q_block, lambda p, i: (jnp.minimum(i + 1, n_steps - 1), p, 0)),
                  pl.BlockSpec((N_BLOCK_COLS, HEADS_PER_STEP * AUG), lambda p, i: (0, p)),
                  pl.BlockSpec((N_BLOCK_COLS, HEADS_PER_STEP * AUG), lambda p, i: (0, p)),
                  pl.BlockSpec((s, HEADS_PER_STEP * AUG), lambda p, i: (0, p)),
                  pl.BlockSpec((nb, HEADS_PER_STEP * V_ROWS, MOBA_BLOCK), lambda p, i: (0, p, 0))],
        out_specs=pl.BlockSpec((tq, HEADS_PER_STEP * ATTN_HEAD_DIM), lambda p, i: (i, p)),
        out_shape=jax.ShapeDtypeStruct((s, ATTN_WIDTH), BF16),
        scratch_shapes=[pltpu.VMEM((3, HEADS_PER_STEP, KEY_GROUP * MOBA_BLOCK, tq), F32),
                        pltpu.VMEM((HEADS_PER_STEP, AUG, tq), BF16),
                        pltpu.VMEM((HEADS_PER_STEP, 8, tq), F32),
                        pltpu.VMEM((8, LANES), jnp.int32)],
        compiler_params=pltpu.CompilerParams(dimension_semantics=("arbitrary", "arbitrary"),
                                             vmem_limit_bytes=VMEM_LIMIT),
        name="moba",
    )(slope_tab, qt, qt, kmean, knorm2, k_aug, vt_aug)


def _mlstm_kernel(qk_ref, vo_ref, gates_ref, conv_ref, gbias_ref, ng_ref, y_ref,
                  xbuf, c_ref, n_ref, m_ref):
    L = MLSTM_CHUNK
    d = MLSTM_HEAD_DIM
    rows = qk_ref.shape[0]
    halo = xbuf.shape[0] - rows
    chunks = range(rows // L)
    heads = range(MLSTM_HEADS)

    @pl.when(pl.program_id(0) == 0)
    def _():
        xbuf[0:halo, :] = jnp.zeros((halo, xbuf.shape[1]), F32)
        c_ref[...] = jnp.zeros(c_ref.shape, F32)
        n_ref[...] = jnp.zeros(n_ref.shape, F32)
        m_ref[...] = jnp.zeros(m_ref.shape, F32)

    xbuf[halo:, :] = qk_ref[...]
    w = conv_ref[...]
    y = qk_ref[...] * w[CONV_WIDTH - 1:CONV_WIDTH, :]
    for j in range(CONV_WIDTH - 1):
        off = halo - CONV_WIDTH + 1 + j
        y = y + xbuf[off:off + rows, :] * w[j:j + 1, :]
    xbuf[0:halo, :] = xbuf[rows:rows + halo, :]
    qk = y * _sigmoid(y)

    a = gates_ref[...] + gbias_ref[...]
    lane = lax.broadcasted_iota(jnp.int32, (rows, LANES), 1)
    is_f = jnp.logical_and(lane >= MLSTM_HEADS, lane < 2 * MLSTM_HEADS)
    log_f = jnp.where(is_f, jnp.minimum(a, 0.0) - jnp.log(1.0 + jnp.exp(-jnp.abs(a))), 0.0)
    t_io = lax.broadcasted_iota(jnp.int32, (rows, rows), 0)
    s_io = lax.broadcasted_iota(jnp.int32, (rows, rows), 1)
    same_chunk = (t_io // L) == (s_io // L)
    tri = jnp.where(jnp.logical_and(s_io <= t_io, same_chunk), 1.0, 0.0)
    bcum = jnp.dot(tri, log_f, precision=lax.Precision.HIGHEST,
                   preferred_element_type=F32)
    colm = jnp.where(lane < MLSTM_HEADS, a, bcum)
    rowm = colm.T
    causal = (lax.broadcasted_iota(jnp.int32, (L, L), 1) <= lax.broadcasted_iota(jnp.int32, (L, L), 0))

    pairs = [(cc, hd) for cc in chunks for hd in heads]
    rs = lambda cc: slice(cc * L, (cc + 1) * L)
    q = {(cc, hd): qk[rs(cc), hd * d:(hd + 1) * d] for cc, hd in pairs}
    k = {(cc, hd): qk[rs(cc), MLSTM_WIDTH + hd * d:MLSTM_WIDTH + (hd + 1) * d] * (d ** -0.5) for cc, hd in pairs}
    qb = {p: q[p].astype(BF16) for p in pairs}
    kb = {p: k[p].astype(BF16) for p in pairs}
    vb = {(cc, hd): vo_ref[rs(cc), hd * d:(hd + 1) * d].astype(BF16) for cc, hd in pairs}
    ic_c = {(cc, hd): colm[rs(cc), hd:hd + 1] for cc, hd in pairs}
    bc_c = {(cc, hd): colm[rs(cc), MLSTM_HEADS + hd:MLSTM_HEADS + hd + 1] for cc, hd in pairs}
    ic_r = {(cc, hd): rowm[hd:hd + 1, rs(cc)] for cc, hd in pairs}
    bc_r = {(cc, hd): rowm[MLSTM_HEADS + hd:MLSTM_HEADS + hd + 1, rs(cc)] for cc, hd in pairs}
    dmat = {p: jnp.where(causal, bc_c[p] - bc_r[p] + ic_r[p], -jnp.inf) for p in pairs}
    a_t = {p: jnp.max(dmat[p], axis=1, keepdims=True) for p in pairs}
    sc = {p: lax.dot_general(qb[p], kb[p], NT_DIMS, preferred_element_type=F32) * jnp.exp(dmat[p] - a_t[p])
          for p in pairs}
    sv = {p: jnp.dot(sc[p].astype(BF16), vb[p], preferred_element_type=F32) for p in pairs}
    ssum = {p: jnp.sum(sc[p], axis=1, keepdims=True) for p in pairs}
    b_last = {p: bc_c[p][L - 1:L, :] for p in pairs}
    g_c = {p: b_last[p] - bc_c[p] + ic_c[p] for p in pairs}
    gmax = {p: jnp.max(g_c[p], axis=0, keepdims=True) for p in pairs}
    kw = {p: k[p] * jnp.exp(g_c[p] - gmax[p]) for p in pairs}
    upd = {p: jnp.dot(kw[p].T.astype(BF16), vb[p], preferred_element_type=F32) for p in pairs}
    ksum = {p: jnp.sum(kw[p], axis=0, keepdims=True) for p in pairs}
    h_out = {}
    for hd in heads:
        c_st, n_st, m_prev = c_ref[hd], n_ref[hd], m_ref[hd:hd + 1, 0:1]
        for cc in chunks:
            p = (cc, hd)
            inter = bc_c[p] + m_prev
            m_t = jnp.maximum(inter, a_t[p])
            r_t = jnp.exp(a_t[p] - m_t)
            w_inter = jnp.exp(inter - m_t)
            num = w_inter * jnp.dot(qb[p], c_st.astype(BF16), preferred_element_type=F32) + r_t * sv[p]
            den = w_inter * jnp.sum(q[p] * n_st, axis=1, keepdims=True) + r_t * ssum[p]
            h_out[p] = num / jnp.maximum(jnp.abs(den), jnp.exp(-m_t))
            m_new = jnp.maximum(b_last[p] + m_prev, gmax[p])
            decay = jnp.exp(b_last[p] + m_prev - m_new)
            gain = jnp.exp(gmax[p] - m_new)
            c_st = decay * c_st + gain * upd[p]
            n_st = decay * n_st + gain * ksum[p]
            m_prev = m_new
        c_ref[hd], n_ref[hd] = c_st, n_st
        m_ref[hd:hd + 1, :] = jnp.broadcast_to(m_prev, (1, LANES))
    for cc, hd in pairs:
        hs = slice(hd * d, (hd + 1) * d)
        og = vo_ref[rs(cc), MLSTM_WIDTH + hd * d:MLSTM_WIDTH + (hd + 1) * d]
        hg = h_out[(cc, hd)] * _sigmoid(og)
        mu = jnp.mean(hg, axis=1, keepdims=True)
        var = jnp.mean(jnp.square(hg - mu), axis=1, keepdims=True)
        y_ref[rs(cc), hs] = ((hg - mu) * lax.rsqrt(var + NORM_EPS) * ng_ref[:, hs]).astype(y_ref.dtype)


def _mlstm(qkm, vom, gates, conv, gbias, ng):
    s = qkm.shape[0]
    rows = MLSTM_STEP_CHUNKS * MLSTM_CHUNK
    row = lambda w: pl.BlockSpec((rows, w), lambda t: (t, 0))
    whole = pl.BlockSpec(memory_space=pltpu.VMEM)
    return pl.pallas_call(
        _mlstm_kernel,
        grid=(s // rows,),
        in_specs=[row(2 * MLSTM_WIDTH), row(2 * MLSTM_WIDTH), row(LANES), whole, whole, whole],
        out_specs=row(MLSTM_WIDTH),
        out_shape=jax.ShapeDtypeStruct((s, MLSTM_WIDTH), BF16),
        scratch_shapes=[pltpu.VMEM((rows + CONV_HALO, 2 * MLSTM_WIDTH), F32),
                        pltpu.VMEM((MLSTM_HEADS, MLSTM_HEAD_DIM, MLSTM_HEAD_DIM), F32),
                        pltpu.VMEM((MLSTM_HEADS, 1, MLSTM_HEAD_DIM), F32),
                        pltpu.VMEM((8, LANES), F32)],
        compiler_params=pltpu.CompilerParams(dimension_semantics=("arbitrary",),
                                             vmem_limit_bytes=VMEM_LIMIT),
        name="mlstm",
    )(qkm, vom, gates, conv, gbias, ng)


def _out_ffn_kernel(x_ref, ya_ref, ym_ref, gg_ref, woa_ref, wom_ref, wo_ref, fg_ref,
                    wgt_ref, wup_ref, wdn_ref, fin_ref, o_ref, *, final_norm):
    a = jnp.dot(ya_ref[...], woa_ref[...], preferred_element_type=F32)
    b = jnp.dot(ym_ref[...], wom_ref[...], preferred_element_type=F32)
    merged = _sigmoid(gg_ref[:, :D_MODEL]) * a + _sigmoid(gg_ref[:, D_MODEL:]) * b
    x1 = x_ref[...] + jnp.dot(merged.astype(BF16), wo_ref[...], preferred_element_type=F32)
    h2 = _rms(x1, fg_ref[...]).astype(BF16)
    acc = jnp.zeros(x1.shape, F32)
    for c in range(0, FFN_HIDDEN, FFN_CHUNK):
        gt = jnp.dot(h2, wgt_ref[:, c:c + FFN_CHUNK], preferred_element_type=F32)
        up = jnp.dot(h2, wup_ref[:, c:c + FFN_CHUNK], preferred_element_type=F32)
        act = (gt * _sigmoid(gt) * up).astype(BF16)
        acc = acc + jnp.dot(act, wdn_ref[c:c + FFN_CHUNK, :], preferred_element_type=F32)
    x2 = x1 + acc
    o_ref[...] = _rms(x2, fin_ref[...]) if final_norm else x2


def _out_ffn(x, ya, ym, gg, woa, wom, wo, fg, wgt, wup, wdn, fin, final_norm):
    s = x.shape[0]
    row = lambda w: pl.BlockSpec((OUT_TILE, w), lambda i: (i, 0))
    whole = pl.BlockSpec(memory_space=pltpu.VMEM)
    return pl.pallas_call(
        functools.partial(_out_ffn_kernel, final_norm=final_norm),
        grid=(s // OUT_TILE,),
        in_specs=[row(D_MODEL), row(ATTN_WIDTH), row(MLSTM_WIDTH), row(2 * D_MODEL)] + [whole] * 8,
        out_specs=row(D_MODEL),
        out_shape=jax.ShapeDtypeStruct((s, D_MODEL), F32),
        compiler_params=pltpu.CompilerParams(dimension_semantics=("arbitrary",),
                                             vmem_limit_bytes=VMEM_LIMIT),
        name="out_ffn",
    )(x, ya, ym, gg, woa, wom, wo, fg, wgt, wup, wdn, fin)


def _alibi_slopes():
    return np.exp2(-8.0 * np.arange(1, ATTN_HEADS + 1, dtype=np.float64) / ATTN_HEADS).astype(np.float32)


def _pad_heads(w, width):
    d = w.shape[0]
    w = w.reshape(d, ATTN_HEADS, ATTN_HEAD_DIM)
    w = jnp.pad(w, ((0, 0), (0, 0), (0, width - ATTN_HEAD_DIM)))
    return w.reshape(d, ATTN_HEADS * width)


def kernel(x, mix_norm_g, w_in, conv_qk, b_igate, b_fgate, mlstm_norm_g, w_out_attn, w_out_mlstm,
           w_o, ffn_norm_g, w_ffn_gate, w_ffn_up, w_ffn_down, final_norm_g):
    batch, seq, _ = x.shape
    depth = w_in.shape[0]
    assert seq % OUT_TILE == 0 and seq // MOBA_BLOCK <= N_BLOCK_COLS
    assert seq % (KEY_GROUP * MOBA_BLOCK) == 0
    slopes = _alibi_slopes()
    alibi = np.zeros((1, ATTN_AUG_WIDTH), np.float32)
    alibi[0, AUG - 1::AUG] = slopes
    alibi = jnp.asarray(alibi)
    slope_tab = jnp.asarray(np.broadcast_to(
        slopes.reshape(ATTN_HEADS // HEADS_PER_STEP, HEADS_PER_STEP, 1),
        (ATTN_HEADS // HEADS_PER_STEP, HEADS_PER_STEP, LANES)).copy())
    a0, a1, a2 = ATTN_WIDTH, 2 * ATTN_WIDTH, 3 * ATTN_WIDTH
    m1 = a2 + 4 * MLSTM_WIDTH
    g0 = m1 + 2 * MLSTM_HEADS

    outs = []
    for bi in range(batch):
        xb = x[bi]
        for layer in range(depth):
            w = w_in[layer]
            wq = w[:, :a0].T.astype(BF16)
            wk = _pad_heads(w[:, a0:a1], AUG).astype(BF16)
            wv = _pad_heads(w[:, a1:a2], V_ROWS).T.astype(BF16)
            wm = w[:, a2:m1].astype(BF16)
            wif = jnp.pad(w[:, m1:g0], ((0, 0), (0, LANES - 2 * MLSTM_HEADS))).astype(BF16)
            wg = w[:, g0:].astype(BF16)
            q, k_aug, v_aug, kstat, qkm, vom, gates, gg = _inproj(
                xb, mix_norm_g[layer][None, :], wq, wk, wv, wm, wif, wg, alibi)
            pad_blocks = ((0, N_BLOCK_COLS - kstat.shape[0]), (0, 0))
            ya = _moba(slope_tab, q, jnp.pad(kstat[:, 0, :], pad_blocks), jnp.pad(kstat[:, 1, :], pad_blocks),
                       k_aug, v_aug)
            gbias = jnp.pad(jnp.concatenate([b_igate[layer], b_fgate[layer]])[None, :],
                            ((0, 0), (0, LANES - 2 * MLSTM_HEADS))).astype(F32)
            ym = _mlstm(qkm, vom, gates, conv_qk[layer], gbias, mlstm_norm_g[layer][None, :])
            xb = _out_ffn(xb, ya, ym, gg,
                          w_out_attn[layer].astype(BF16), w_out_mlstm[layer].astype(BF16),
                          w_o[layer].astype(BF16), ffn_norm_g[layer][None, :],
                          w_ffn_gate[layer].astype(BF16), w_ffn_up[layer].astype(BF16),
                          w_ffn_down[layer].astype(BF16), final_norm_g[None, :],
                          final_norm=(layer == depth - 1))
        outs.append(xb)
    return outs[0][None] if batch == 1 else jnp.stack(outs, axis=0)
```

```python
import functools

import numpy as np
import jax
import jax.numpy as jnp
from jax import lax
from jax.experimental import pallas as pl
from jax.experimental.pallas import tpu as pltpu

D_MODEL = 1024
ATTN_HEADS = 8
ATTN_HEAD_DIM = 64
ATTN_WIDTH = ATTN_HEADS * ATTN_HEAD_DIM
MOBA_BLOCK = 256
MOBA_TOP_K = 3
MLSTM_HEADS = 4
MLSTM_HEAD_DIM = 128
MLSTM_WIDTH = MLSTM_HEADS * MLSTM_HEAD_DIM
MLSTM_CHUNK = 128
CONV_WIDTH = 4
FFN_HIDDEN = 2816
NORM_EPS = 1e-6

LANES = 128
AUG = 2 * ATTN_HEAD_DIM
ATTN_AUG_WIDTH = ATTN_HEADS * AUG
V_ROWS = 80
N_BLOCK_COLS = AUG - ATTN_HEAD_DIM
HEADS_PER_STEP = 2
CONV_HALO = 8
Q_TILES = 2
KEY_GROUP = 2
NEGLIGIBLE_LOG = 50.0
NORM_SLACK = 1.02
NEG_BIAS = -1e9
FFN_CHUNK = 256
OUT_TILE = 512
VMEM_LIMIT = 56 * 1024 * 1024

F32 = jnp.float32
BF16 = jnp.bfloat16
NT_DIMS = (((1,), (1,)), ((), ()))


def _rms(x, g):
    return x * lax.rsqrt(jnp.mean(x * x, axis=-1, keepdims=True) + NORM_EPS) * g


def _sigmoid(x):
    return 1.0 / (1.0 + jnp.exp(-x))


def _inproj_kernel(x_ref, g_ref, wq_ref, wk_ref, wv_ref, wm_ref, wvo_ref, wif_ref, wg_ref, alibi_ref,
                   q_ref, k_ref, v_ref, kmean_ref, qkm_ref, vom_ref, gates_ref, gg_ref):
    blk = pl.program_id(0)
    hb = _rms(x_ref[...], g_ref[...]).astype(BF16)
    q_ref[0] = lax.dot_general(wq_ref[...], hb, NT_DIMS, preferred_element_type=F32)
    kf = jnp.dot(hb, wk_ref[...], preferred_element_type=F32)
    ksq = kf * kf
    norm2 = [jnp.max(jnp.sum(ksq[:, h * AUG:(h + 1) * AUG], axis=1, keepdims=True), axis=0, keepdims=True)
             for h in range(ATTN_HEADS)]
    kmean_ref[0] = jnp.concatenate(
        [jnp.mean(kf, axis=0, keepdims=True),
         jnp.concatenate([jnp.broadcast_to(n2, (1, AUG)) for n2 in norm2], axis=1)], axis=0)
    lane = lax.broadcasted_iota(jnp.int32, (1, ATTN_AUG_WIDTH), 1) & (AUG - 1)
    row = lax.broadcasted_iota(jnp.int32, (MOBA_BLOCK, 1), 0).astype(F32)
    onehot = jnp.where(lane - ATTN_HEAD_DIM == blk, 1.0, 0.0)
    onehot = jnp.where(lane == AUG - 1, 0.0, onehot)
    k_ref[...] = (kf + onehot + alibi_ref[...] * row).astype(BF16)
    vt = lax.dot_general(wv_ref[...], hb, NT_DIMS, preferred_element_type=F32)
    feat = lax.rem(lax.broadcasted_iota(jnp.int32, (ATTN_HEADS * V_ROWS, 1), 0), V_ROWS)
    v_ref[0] = (vt + jnp.where(feat == ATTN_HEAD_DIM, 1.0, 0.0)).astype(BF16)
    qkm_ref[...] = jnp.dot(hb, wm_ref[...], preferred_element_type=F32)
    vom_ref[0] = lax.dot_general(wvo_ref[...], hb, NT_DIMS, preferred_element_type=F32)
    gates_ref[...] = jnp.dot(hb, wif_ref[...], preferred_element_type=F32)
    gg_ref[...] = jnp.dot(hb, wg_ref[...], preferred_element_type=F32)


def _inproj(x, g, wq, wk, wv, wm, wvo, wif, wg, alibi):
    s = x.shape[0]
    nb = s // MOBA_BLOCK
    row = lambda w: pl.BlockSpec((MOBA_BLOCK, w), lambda i: (i, 0))
    whole = pl.BlockSpec(memory_space=pltpu.VMEM)
    return pl.pallas_call(
        _inproj_kernel,
        grid=(nb,),
        in_specs=[row(D_MODEL)] + [whole] * 9,
        out_specs=[pl.BlockSpec((1, ATTN_WIDTH, MOBA_BLOCK), lambda i: (i, 0, 0)),
                   row(ATTN_AUG_WIDTH),
                   pl.BlockSpec((1, ATTN_HEADS * V_ROWS, MOBA_BLOCK), lambda i: (i, 0, 0)),
                   pl.BlockSpec((1, 2, ATTN_AUG_WIDTH), lambda i: (i, 0, 0)),
                   row(2 * MLSTM_WIDTH), pl.BlockSpec((1, 2 * MLSTM_WIDTH, MOBA_BLOCK), lambda i: (i, 0, 0)),
                   row(LANES), row(2 * D_MODEL)],
        out_shape=[jax.ShapeDtypeStruct((nb, ATTN_WIDTH, MOBA_BLOCK), F32),
                   jax.ShapeDtypeStruct((s, ATTN_AUG_WIDTH), BF16),
                   jax.ShapeDtypeStruct((nb, ATTN_HEADS * V_ROWS, MOBA_BLOCK), BF16),
                   jax.ShapeDtypeStruct((nb, 2, ATTN_AUG_WIDTH), F32),
                   jax.ShapeDtypeStruct((s, 2 * MLSTM_WIDTH), F32),
                   jax.ShapeDtypeStruct((nb, 2 * MLSTM_WIDTH, MOBA_BLOCK), F32),
                   jax.ShapeDtypeStruct((s, LANES), F32),
                   jax.ShapeDtypeStruct((s, 2 * D_MODEL), F32)],
        compiler_params=pltpu.CompilerParams(dimension_semantics=("arbitrary",),
                                             vmem_limit_bytes=VMEM_LIMIT),
        name="inproj",
    )(x, g, wq, wk, wv, wm, wvo, wif, wg, alibi)


def _col_max(st):
    rows, n = st.shape
    fan = 8
    while rows > fan * 8 and rows % fan == 0:
        st = jnp.max(st.reshape(fan, rows // fan, n), axis=0)
        rows //= fan
    return jnp.max(st, axis=0, keepdims=True)


def _moba_kernel(slope_ref, q_ref, qn_ref, kmean_ref, kn2_ref, k_ref, vt_ref, o_ref,
                 s_ref, qa_ref, cm_ref, reach_ref):
    step_id = pl.program_id(1)
    n_steps = pl.num_programs(1)
    i = step_id * Q_TILES
    tq = Q_TILES * MOBA_BLOCK
    nb = N_BLOCK_COLS
    group_rows = KEY_GROUP * MOBA_BLOCK
    assert KEY_GROUP % Q_TILES == 0
    heads = range(HEADS_PER_STEP)
    head_lanes = [slice(hh * AUG, (hh + 1) * AUG) for hh in heads]
    head_rows = [slice(hh * V_ROWS, (hh + 1) * V_ROWS) for hh in heads]
    own_slot = 2

    def prepare(qsrc_ref, tile0):
        blk = lax.broadcasted_iota(jnp.int32, (nb, tq), 0).astype(F32)
        i_f = (tile0 + lax.broadcasted_iota(jnp.int32, (1, tq), 1) // MOBA_BLOCK).astype(F32)
        reach = jnp.zeros((1, 1), F32)
        qk_cap, slopes = [], []
        for hh in heads:
            slope = slope_ref[0, hh:hh + 1, 0:1]
            qh = jnp.concatenate([qsrc_ref[t, hh * ATTN_HEAD_DIM:(hh + 1) * ATTN_HEAD_DIM, :]
                                  for t in range(Q_TILES)], axis=1)
            km = kmean_ref[:, hh * AUG:hh * AUG + ATTN_HEAD_DIM]
            gate = jnp.dot(km, qh, precision=lax.Precision.HIGHEST, preferred_element_type=F32)
            valid = blk < i_f
            g = jnp.where(valid, gate, -jnp.inf)
            sel = jnp.zeros((nb, tq), jnp.bool_)
            for _ in range(MOBA_TOP_K):
                mx = jnp.max(g, axis=0, keepdims=True)
                idx = jnp.min(jnp.where(g == mx, blk, float(nb)), axis=0, keepdims=True)
                pick = blk == idx
                sel = jnp.logical_or(sel, pick)
                g = jnp.where(pick, -jnp.inf, g)
            bias = jnp.where(sel, (blk - i_f) * (slope * float(MOBA_BLOCK)), NEG_BIAS)
            bias = jnp.where(valid, bias, 0.0)
            bias = jnp.where(blk == float(nb - 1), 1.0, bias)
            qa_ref[hh] = jnp.concatenate([qh * (ATTN_HEAD_DIM ** -0.5), bias], axis=0).astype(BF16)
            k_norm = jnp.sqrt(jnp.max(kn2_ref[:, hh * AUG:hh * AUG + 1], axis=0, keepdims=True))
            qk_cap.append(NORM_SLACK * (ATTN_HEAD_DIM ** -0.5) * k_norm
                          * jnp.sqrt(jnp.sum(qh * qh, axis=0, keepdims=True)))
            slopes.append(slope)
        first = (tile0 // KEY_GROUP) * KEY_GROUP
        hidden = (lax.broadcasted_iota(jnp.int32, (group_rows, tq), 0)
                  - lax.broadcasted_iota(jnp.int32, (group_rows, tq), 1)) > (tile0 - first) * MOBA_BLOCK
        start = pl.multiple_of(first * MOBA_BLOCK, group_rows)
        for hh in heads:
            st = jnp.dot(k_ref[pl.ds(start, group_rows), head_lanes[hh]], qa_ref[hh], preferred_element_type=F32)
            st = jnp.where(hidden, -jnp.inf, st)
            s_ref[own_slot, hh] = st
            own_max = _col_max(st)
            cm_ref[hh] = jnp.broadcast_to(own_max, (8, tq))
            spread = jnp.max(qk_cap[hh] - own_max, axis=1, keepdims=True)
            reach = jnp.maximum(reach, (NEGLIGIBLE_LOG + spread) / (slopes[hh] * float(MOBA_BLOCK)) + 1.0)
        reach = jnp.ceil(reach)
        reach = jnp.where(reach < float(nb), reach, float(nb))
        reach_ref[...] = jnp.broadcast_to(reach, reach_ref.shape).astype(jnp.int32)

    def score(first, slot):
        start = pl.multiple_of(first * MOBA_BLOCK, group_rows)
        col_max = []
        for hh in heads:
            st = jnp.dot(k_ref[pl.ds(start, group_rows), head_lanes[hh]], qa_ref[hh], preferred_element_type=F32)
            s_ref[slot, hh] = st
            col_max.append(_col_max(st))
        return col_max

    def consume(first, slot, m_prev, m_cur, accs):
        ps = [jnp.exp(s_ref[slot, hh] - m_cur[hh]).astype(BF16) for hh in heads]
        accs = [jnp.exp(m_prev[hh] - m_cur[hh]) * accs[hh] for hh in heads]
        for b in range(KEY_GROUP):
            for hh in heads:
                accs[hh] = accs[hh] + jnp.dot(vt_ref[first + b, head_rows[hh], :],
                                              ps[hh][b * MOBA_BLOCK:(b + 1) * MOBA_BLOCK, :],
                                              preferred_element_type=F32)
        return accs

    def step(first, slot, src, carry):
        pend, m_prev, m_cur, accs = carry
        cm = score(first, slot)
        accs = consume(pend, src, m_prev, m_cur, accs)
        return (jnp.asarray(first, jnp.int32), m_cur,
                [jnp.maximum(m_cur[hh], cm[hh]) for hh in heads], accs)

    def pair(first_group, carry, src=0):
        g = first_group * KEY_GROUP
        return step(g + KEY_GROUP, 0, 1, step(g, 1, src, carry))

    @pl.when(step_id == 0)
    def _():
        prepare(q_ref, i)

    own_first = (i // KEY_GROUP) * KEY_GROUP
    first_block = jnp.maximum(i - jnp.max(reach_ref[...]) + 1, 0)
    first_group = first_block // KEY_GROUP
    n_past = i // KEY_GROUP - first_group
    cm0 = [cm_ref[hh, 0:1, :] for hh in heads]
    init = (jnp.asarray(own_first, jnp.int32), cm0, cm0, [jnp.zeros((V_ROWS, tq), F32) for _ in heads])

    def start_none(c):
        for hh in heads:
            s_ref[0, hh] = s_ref[own_slot, hh]
        return c

    def start_odd(c):
        return step(first_group * KEY_GROUP, 0, own_slot, c)

    def start_even(c):
        return pair(first_group, c, src=own_slot)

    odd = n_past % 2
    carry = lax.cond(n_past == 0, start_none,
                     lambda c: lax.cond(odd == 1, start_odd, start_even, c), init)
    done = first_group + jnp.where(n_past == 0, 0, 2 - odd)
    n_pairs = (i // KEY_GROUP - done) // 2
    odd_pair = n_pairs % 2
    carry = lax.cond(odd_pair == 1, functools.partial(pair, done), lambda c: c, carry)
    done = done + 2 * odd_pair

    def quad(u, carry):
        return pair(done + 4 * u + 2, pair(done + 4 * u, carry))

    pend, m_prev, m_cur, accs = lax.fori_loop(0, n_pairs // 2, quad, carry)
    accs = consume(pend, 0, m_prev, m_cur, accs)
    prepare(qn_ref, jnp.minimum(step_id + 1, n_steps - 1) * Q_TILES)
    outs = []
    for hh in heads:
        ot = accs[hh][:ATTN_HEAD_DIM, :] / accs[hh][ATTN_HEAD_DIM:ATTN_HEAD_DIM + 1, :]
        outs.append(ot.T)
    o_ref[...] = jnp.concatenate(outs, axis=1).astype(o_ref.dtype)


def _moba(slope_tab, qt, kmean, knorm2, k_aug, vt_aug):
    s = k_aug.shape[0]
    nb = s // MOBA_BLOCK
    n_pairs = ATTN_HEADS // HEADS_PER_STEP
    n_steps = nb // Q_TILES
    tq = Q_TILES * MOBA_BLOCK
    q_block = (Q_TILES, HEADS_PER_STEP * ATTN_HEAD_DIM, MOBA_BLOCK)
    return pl.pallas_call(
        _moba_kernel,
        grid=(n_pairs, n_steps),
        in_specs=[pl.BlockSpec((1, HEADS_PER_STEP, LANES), lambda p, i: (p, 0, 0)),
                  pl.BlockSpec(q_block, lambda p, i: (i, p, 0)),
                  pl.BlockSpec(q_block, lambda p, i: (jnp.minimum(i + 1, n_steps - 1), p, 0)),
                  pl.BlockSpec((N_BLOCK_COLS, HEADS_PER_STEP * AUG), lambda p, i: (0, p)),
                  pl.BlockSpec((N_BLOCK_COLS, HEADS_PER_STEP * AUG), lambda p, i: (0, p)),
                  pl.BlockSpec((s, HEADS_PER_STEP * AUG), lambda p, i: (0, p)),
                  pl.BlockSpec((nb, HEADS_PER_STEP * V_ROWS, MOBA_BLOCK), lambda p, i: (0, p, 0))],
        out_specs=pl.BlockSpec((tq, HEADS_PER_STEP * ATTN_HEAD_DIM), lambda p, i: (i, p)),
        out_shape=jax.ShapeDtypeStruct((s, ATTN_WIDTH), BF16),
        scratch_shapes=[pltpu.VMEM((3, HEADS_PER_STEP, KEY_GROUP * MOBA_BLOCK, tq), F32),
                        pltpu.VMEM((HEADS_PER_STEP, AUG, tq), BF16),
                        pltpu.VMEM((HEADS_PER_STEP, 8, tq), F32),
                        pltpu.VMEM((8, LANES), jnp.int32)],
        compiler_params=pltpu.CompilerParams(dimension_semantics=("arbitrary", "arbitrary"),
                                             vmem_limit_bytes=VMEM_LIMIT),
        name="moba",
    )(slope_tab, qt, qt, kmean, knorm2, k_aug, vt_aug)


def _mlstm_kernel(qk_ref, vot_ref, gates_ref, conv_ref, gbias_ref, ngt_ref, y_ref,
                  xbuf, ct_ref, n_ref, m_ref):
    L = MLSTM_CHUNK
    d = MLSTM_HEAD_DIM
    rows = qk_ref.shape[0]
    halo = xbuf.shape[0] - rows
    chunks = range(rows // L)
    heads = range(MLSTM_HEADS)

    @pl.when(pl.program_id(0) == 0)
    def _():
        xbuf[0:halo, :] = jnp.zeros((halo, xbuf.shape[1]), F32)
        ct_ref[...] = jnp.zeros(ct_ref.shape, F32)
        n_ref[...] = jnp.zeros(n_ref.shape, F32)
        m_ref[...] = jnp.zeros(m_ref.shape, F32)

    xbuf[halo:, :] = qk_ref[...]
    w = conv_ref[...]
    y = qk_ref[...] * w[CONV_WIDTH - 1:CONV_WIDTH, :]
    for j in range(CONV_WIDTH - 1):
        off = halo - CONV_WIDTH + 1 + j
        y = y + xbuf[off:off + rows, :] * w[j:j + 1, :]
    xbuf[0:halo, :] = xbuf[rows:rows + halo, :]
    qk = y * _sigmoid(y)

    a = gates_ref[...] + gbias_ref[...]
    lane = lax.broadcasted_iota(jnp.int32, (rows, LANES), 1)
    is_f = jnp.logical_and(lane >= MLSTM_HEADS, lane < 2 * MLSTM_HEADS)
    log_f = jnp.where(is_f, jnp.minimum(a, 0.0) - jnp.log(1.0 + jnp.exp(-jnp.abs(a))), 0.0)
    t_io = lax.broadcasted_iota(jnp.int32, (rows, rows), 0)
    s_io = lax.broadcasted_iota(jnp.int32, (rows, rows), 1)
    same_chunk = (t_io // L) == (s_io // L)
    tri = jnp.where(jnp.logical_and(s_io <= t_io, same_chunk), 1.0, 0.0)
    bcum = jnp.dot(tri, log_f, precision=lax.Precision.HIGHEST,
                   preferred_element_type=F32)
    colm = jnp.where(lane < MLSTM_HEADS, a, bcum)
    rowm = colm.T
    causal_t = (lax.broadcasted_iota(jnp.int32, (L, L), 0) <= lax.broadcasted_iota(jnp.int32, (L, L), 1))

    pairs = [(cc, hd) for cc in chunks for hd in heads]
    rs = lambda cc: slice(cc * L, (cc + 1) * L)
    fs = lambda hd: slice(hd * d, (hd + 1) * d)
    qtb = {(cc, hd): qk[rs(cc), fs(hd)].T.astype(BF16) for cc, hd in pairs}
    kb = {(cc, hd): (qk[rs(cc), MLSTM_WIDTH + hd * d:MLSTM_WIDTH + (hd + 1) * d] * (d ** -0.5)).astype(BF16)
          for cc, hd in pairs}
    vt = {(cc, hd): vot_ref[0, fs(hd), rs(cc)] for cc, hd in pairs}
    ic_r = {(cc, hd): rowm[hd:hd + 1, rs(cc)] for cc, hd in pairs}
    bc_r = {(cc, hd): rowm[MLSTM_HEADS + hd:MLSTM_HEADS + hd + 1, rs(cc)] for cc, hd in pairs}
    e_c = {(cc, hd): colm[rs(cc), hd:hd + 1] - colm[rs(cc), MLSTM_HEADS + hd:MLSTM_HEADS + hd + 1]
           for cc, hd in pairs}
    dmat_t = {p: jnp.where(causal_t, bc_r[p] + e_c[p], -jnp.inf) for p in pairs}
    a_row = {p: jnp.max(dmat_t[p], axis=0, keepdims=True) for p in pairs}
    sc_t = {p: jnp.dot(kb[p], qtb[p], preferred_element_type=F32) * jnp.exp(dmat_t[p] - a_row[p]) for p in pairs}
    sv_t = {p: jnp.dot(vt[p].astype(BF16), sc_t[p].astype(BF16), preferred_element_type=F32) for p in pairs}
    ssum = {p: jnp.sum(sc_t[p], axis=0, keepdims=True) for p in pairs}
    b_last = {p: bc_r[p][:, L - 1:L] for p in pairs}
    g_r = {p: b_last[p] - bc_r[p] + ic_r[p] for p in pairs}
    gmax = {p: jnp.max(g_r[p], axis=1, keepdims=True) for p in pairs}
    w_row = {p: jnp.exp(g_r[p] - gmax[p]) for p in pairs}
    upd_t = {p: jnp.dot((vt[p] * w_row[p]).astype(BF16), kb[p], preferred_element_type=F32) for p in pairs}
    ksum = {p: jnp.dot(jnp.broadcast_to(w_row[p], (8, L)).astype(BF16), kb[p],
                       preferred_element_type=F32)[0:1, :] for p in pairs}
    h_t = {}
    for hd in heads:
        ct_st, n_st, m_prev = ct_ref[hd], n_ref[hd], m_ref[hd:hd + 1, 0:1]
        for cc in chunks:
            p = (cc, hd)
            inter = bc_r[p] + m_prev
            m_row = jnp.maximum(inter, a_row[p])
            r_row = jnp.exp(a_row[p] - m_row)
            w_inter = jnp.exp(inter - m_row)
            qc_t = jnp.dot(ct_st.astype(BF16), qtb[p], preferred_element_type=F32)
            qn = jnp.dot(jnp.broadcast_to(n_st, (8, d)).astype(BF16), qtb[p],
                         preferred_element_type=F32)[0:1, :]
            den = w_inter * qn + r_row * ssum[p]
            scale = 1.0 / jnp.maximum(jnp.abs(den), jnp.exp(-m_row))
            h_t[p] = (w_inter * scale) * qc_t + (r_row * scale) * sv_t[p]
            m_new = jnp.maximum(b_last[p] + m_prev, gmax[p])
            decay = jnp.exp(b_last[p] + m_prev - m_new)
            gain = jnp.exp(gmax[p] - m_new)
            ct_st = decay * ct_st + gain * upd_t[p]
            n_st = decay * n_st + gain * ksum[p]
            m_prev = m_new
        ct_ref[hd], n_ref[hd] = ct_st, n_st
        m_ref[hd:hd + 1, :] = jnp.broadcast_to(m_prev, (1, LANES))
    for cc, hd in pairs:
        og_t = vot_ref[0, MLSTM_WIDTH + hd * d:MLSTM_WIDTH + (hd + 1) * d, rs(cc)]
        hg = h_t[(cc, hd)] * _sigmoid(og_t)
        mu = jnp.mean(hg, axis=0, keepdims=True)
        var = jnp.mean(jnp.square(hg - mu), axis=0, keepdims=True)
        yn = (hg - mu) * lax.rsqrt(var + NORM_EPS) * ngt_ref[fs(hd), :]
        y_ref[rs(cc), fs(hd)] = yn.T.astype(y_ref.dtype)


def _mlstm(qkm, vot, gates, conv, gbias, ngt):
    s = qkm.shape[0]
    rows = MOBA_BLOCK
    row = lambda w: pl.BlockSpec((rows, w), lambda t: (t, 0))
    whole = pl.BlockSpec(memory_space=pltpu.VMEM)
    return pl.pallas_call(
        _mlstm_kernel,
        grid=(s // rows,),
        in_specs=[row(2 * MLSTM_WIDTH), pl.BlockSpec((1, 2 * MLSTM_WIDTH, rows), lambda t: (t, 0, 0)),
                  row(LANES), whole, whole, whole],
        out_specs=row(MLSTM_WIDTH),
        out_shape=jax.ShapeDtypeStruct((s, MLSTM_WIDTH), BF16),
        scratch_shapes=[pltpu.VMEM((rows + CONV_HALO, 2 * MLSTM_WIDTH), F32),
                        pltpu.VMEM((MLSTM_HEADS, MLSTM_HEAD_DIM, MLSTM_HEAD_DIM), F32),
                        pltpu.VMEM((MLSTM_HEADS, 1, MLSTM_HEAD_DIM), F32),
                        pltpu.VMEM((8, LANES), F32)],
        compiler_params=pltpu.CompilerParams(dimension_semantics=("arbitrary",),
                                             vmem_limit_bytes=VMEM_LIMIT),
        name="mlstm",
    )(qkm, vot, gates, conv, gbias, ngt)


def _out_ffn_kernel(x_ref, ya_ref, ym_ref, gg_ref, woa_ref, wom_ref, wo_ref, fg_ref,
                    wgt_ref, wup_ref, wdn_ref, fin_ref, o_ref, *, final_norm):
    a = jnp.dot(ya_ref[...], woa_ref[...], preferred_element_type=F32)
    b = jnp.dot(ym_ref[...], wom_ref[...], preferred_element_type=F32)
    merged = _sigmoid(gg_ref[:, :D_MODEL]) * a + _sigmoid(gg_ref[:, D_MODEL:]) * b
    x1 = x_ref[...] + jnp.dot(merged.astype(BF16), wo_ref[...], preferred_element_type=F32)
    h2 = _rms(x1, fg_ref[...]).astype(BF16)
    acc = jnp.zeros(x1.shape, F32)
    for c in range(0, FFN_HIDDEN, FFN_CHUNK):
        gt = jnp.dot(h2, wgt_ref[:, c:c + FFN_CHUNK], preferred_element_type=F32)
        up = jnp.dot(h2, wup_ref[:, c:c + FFN_CHUNK], preferred_element_type=F32)
        act = (gt * _sigmoid(gt) * up).astype(BF16)
        acc = acc + jnp.dot(act, wdn_ref[c:c + FFN_CHUNK, :], preferred_element_type=F32)
    x2 = x1 + acc
    o_ref[...] = _rms(x2, fin_ref[...]) if final_norm else x2


def _out_ffn(x, ya, ym, gg, woa, wom, wo, fg, wgt, wup, wdn, fin, final_norm):
    s = x.shape[0]
    row = lambda w: pl.BlockSpec((OUT_TILE, w), lambda i: (i, 0))
    whole = pl.BlockSpec(memory_space=pltpu.VMEM)
    return pl.pallas_call(
        functools.partial(_out_ffn_kernel, final_norm=final_norm),
        grid=(s // OUT_TILE,),
        in_specs=[row(D_MODEL), row(ATTN_WIDTH), row(MLSTM_WIDTH), row(2 * D_MODEL)] + [whole] * 8,
        out_specs=row(D_MODEL),
        out_shape=jax.ShapeDtypeStruct((s, D_MODEL), F32),
        compiler_params=pltpu.CompilerParams(dimension_semantics=("arbitrary",),
                                             vmem_limit_bytes=VMEM_LIMIT),
        name="out_ffn",
    )(x, ya, ym, gg, woa, wom, wo, fg, wgt, wup, wdn, fin)


def _alibi_slopes():
    return np.exp2(-8.0 * np.arange(1, ATTN_HEADS + 1, dtype=np.float64) / ATTN_HEADS).astype(np.float32)


def _pad_heads(w, width):
    d = w.shape[0]
    w = w.reshape(d, ATTN_HEADS, ATTN_HEAD_DIM)
    w = jnp.pad(w, ((0, 0), (0, 0), (0, width - ATTN_HEAD_DIM)))
    return w.reshape(d, ATTN_HEADS * width)


def kernel(x, mix_norm_g, w_in, conv_qk, b_igate, b_fgate, mlstm_norm_g, w_out_attn, w_out_mlstm,
           w_o, ffn_norm_g, w_ffn_gate, w_ffn_up, w_ffn_down, final_norm_g):
    batch, seq, _ = x.shape
    depth = w_in.shape[0]
    assert seq % OUT_TILE == 0 and seq // MOBA_BLOCK <= N_BLOCK_COLS
    assert seq % (KEY_GROUP * MOBA_BLOCK) == 0
    slopes = _alibi_slopes()
    alibi = np.zeros((1, ATTN_AUG_WIDTH), np.float32)
    alibi[0, AUG - 1::AUG] = slopes
    alibi = jnp.asarray(alibi)
    slope_tab = jnp.asarray(np.broadcast_to(
        slopes.reshape(ATTN_HEADS // HEADS_PER_STEP, HEADS_PER_STEP, 1),
        (ATTN_HEADS // HEADS_PER_STEP, HEADS_PER_STEP, LANES)).copy())
    a0, a1, a2 = ATTN_WIDTH, 2 * ATTN_WIDTH, 3 * ATTN_WIDTH
    m1 = a2 + 4 * MLSTM_WIDTH
    g0 = m1 + 2 * MLSTM_HEADS

    outs = []
    for bi in range(batch):
        xb = x[bi]
        for layer in range(depth):
            w = w_in[layer]
            wq = w[:, :a0].T.astype(BF16)
            wk = _pad_heads(w[:, a0:a1], AUG).astype(BF16)
            wv = _pad_heads(w[:, a1:a2], V_ROWS).T.astype(BF16)
            wm = w[:, a2:a2 + 2 * MLSTM_WIDTH].astype(BF16)
            wvo = w[:, a2 + 2 * MLSTM_WIDTH:m1].T.astype(BF16)
            wif = jnp.pad(w[:, m1:g0], ((0, 0), (0, LANES - 2 * MLSTM_HEADS))).astype(BF16)
            wg = w[:, g0:].astype(BF16)
            q, k_aug, v_aug, kstat, qkm, vom, gates, gg = _inproj(
                xb, mix_norm_g[layer][None, :], wq, wk, wv, wm, wvo, wif, wg, alibi)
            pad_blocks = ((0, N_BLOCK_COLS - kstat.shape[0]), (0, 0))
            ya = _moba(slope_tab, q, jnp.pad(kstat[:, 0, :], pad_blocks), jnp.pad(kstat[:, 1, :], pad_blocks),
                       k_aug, v_aug)
            gbias = jnp.pad(jnp.concatenate([b_igate[layer], b_fgate[layer]])[None, :],
                            ((0, 0), (0, LANES - 2 * MLSTM_HEADS))).astype(F32)
            ngt = jnp.broadcast_to(mlstm_norm_g[layer][:, None], (MLSTM_WIDTH, MLSTM_CHUNK)).astype(F32)
            ym = _mlstm(qkm, vom, gates, conv_qk[layer], gbias, ngt)
            xb = _out_ffn(xb, ya, ym, gg,
                          w_out_attn[layer].astype(BF16), w_out_mlstm[layer].astype(BF16),
                          w_o[layer].astype(BF16), ffn_norm_g[layer][None, :],
                          w_ffn_gate[layer].astype(BF16), w_ffn_up[layer].astype(BF16),
                          w_ffn_down[layer].astype(BF16), final_norm_g[None, :],
                          final_norm=(layer == depth - 1))
        outs.append(xb)
    return outs[0][None] if batch == 1 else jnp.stack(outs, axis=0)
```

```python
import functools

import numpy as np
import jax
import jax.numpy as jnp
from jax import lax
from jax.experimental import pallas as pl
from jax.experimental.pallas import tpu as pltpu

D_MODEL = 1024
ATTN_HEADS = 8
ATTN_HEAD_DIM = 64
ATTN_WIDTH = ATTN_HEADS * ATTN_HEAD_DIM
MOBA_BLOCK = 256
MOBA_TOP_K = 3
MLSTM_HEADS = 4
MLSTM_HEAD_DIM = 128
MLSTM_WIDTH = MLSTM_HEADS * MLSTM_HEAD_DIM
MLSTM_CHUNK = 128
CONV_WIDTH = 4
FFN_HIDDEN = 2816
NORM_EPS = 1e-6

LANES = 128
AUG = 2 * ATTN_HEAD_DIM
ATTN_AUG_WIDTH = ATTN_HEADS * AUG
V_ROWS = 80
N_BLOCK_COLS = AUG - ATTN_HEAD_DIM
HEADS_PER_STEP = 2
CONV_HALO = 8
Q_TILES = 2
KEY_GROUP = 2
NEGLIGIBLE_LOG = 50.0
NORM_SLACK = 1.02
NEG_BIAS = -1e9
FFN_CHUNK = 256
OUT_TILE = 512
VMEM_LIMIT = 56 * 1024 * 1024

F32 = jnp.float32
BF16 = jnp.bfloat16
NT_DIMS = (((1,), (1,)), ((), ()))


def _rms(x, g):
    return x * lax.rsqrt(jnp.mean(x * x, axis=-1, keepdims=True) + NORM_EPS) * g


def _sigmoid(x):
    return 1.0 / (1.0 + jnp.exp(-x))


def _inproj_kernel(x_ref, g_ref, wq_ref, wk_ref, wv_ref, wm_ref, wvo_ref, wif_ref, wg_ref, alibi_ref,
                   q_ref, k_ref, v_ref, kmean_ref, qkm_ref, vom_ref, gates_ref, gg_ref):
    blk = pl.program_id(0)
    hb = _rms(x_ref[...], g_ref[...]).astype(BF16)
    q_ref[0] = lax.dot_general(wq_ref[...], hb, NT_DIMS, preferred_element_type=F32)
    kc = jnp.dot(hb, wk_ref[...], preferred_element_type=F32)
    gap = jnp.zeros((MOBA_BLOCK, AUG - ATTN_HEAD_DIM), F32)
    kf = jnp.concatenate([piece for h in range(ATTN_HEADS)
                          for piece in (kc[:, h * ATTN_HEAD_DIM:(h + 1) * ATTN_HEAD_DIM], gap)], axis=1)
    ksq = kf * kf
    norm2 = [jnp.max(jnp.sum(ksq[:, h * AUG:(h + 1) * AUG], axis=1, keepdims=True), axis=0, keepdims=True)
             for h in range(ATTN_HEADS)]
    kmean_ref[0] = jnp.concatenate(
        [jnp.mean(kf, axis=0, keepdims=True),
         jnp.concatenate([jnp.broadcast_to(n2, (1, AUG)) for n2 in norm2], axis=1)], axis=0)
    lane = lax.broadcasted_iota(jnp.int32, (1, ATTN_AUG_WIDTH), 1) & (AUG - 1)
    row = lax.broadcasted_iota(jnp.int32, (MOBA_BLOCK, 1), 0).astype(F32)
    onehot = jnp.where(lane - ATTN_HEAD_DIM == blk, 1.0, 0.0)
    onehot = jnp.where(lane == AUG - 1, 0.0, onehot)
    k_ref[...] = (kf + onehot + alibi_ref[...] * row).astype(BF16)
    vt = lax.dot_general(wv_ref[...], hb, NT_DIMS, preferred_element_type=F32)
    feat = lax.rem(lax.broadcasted_iota(jnp.int32, (ATTN_HEADS * V_ROWS, 1), 0), V_ROWS)
    v_ref[0] = (vt + jnp.where(feat == ATTN_HEAD_DIM, 1.0, 0.0)).astype(BF16)
    qkm_ref[...] = jnp.dot(hb, wm_ref[...], preferred_element_type=F32)
    vom_ref[0] = lax.dot_general(wvo_ref[...], hb, NT_DIMS, preferred_element_type=F32)
    gates_ref[...] = jnp.dot(hb, wif_ref[...], preferred_element_type=F32)
    gg_ref[...] = jnp.dot(hb, wg_ref[...], preferred_element_type=F32)


def _inproj(x, g, wq, wk, wv, wm, wvo, wif, wg, alibi):
    s = x.shape[0]
    nb = s // MOBA_BLOCK
    row = lambda w: pl.BlockSpec((MOBA_BLOCK, w), lambda i: (i, 0))
    whole = pl.BlockSpec(memory_space=pltpu.VMEM)
    return pl.pallas_call(
        _inproj_kernel,
        grid=(nb,),
        in_specs=[row(D_MODEL)] + [whole] * 9,
        out_specs=[pl.BlockSpec((1, ATTN_WIDTH, MOBA_BLOCK), lambda i: (i, 0, 0)),
                   row(ATTN_AUG_WIDTH),
                   pl.BlockSpec((1, ATTN_HEADS * V_ROWS, MOBA_BLOCK), lambda i: (i, 0, 0)),
                   pl.BlockSpec((1, 2, ATTN_AUG_WIDTH), lambda i: (i, 0, 0)),
                   row(2 * MLSTM_WIDTH), pl.BlockSpec((1, 2 * MLSTM_WIDTH, MOBA_BLOCK), lambda i: (i, 0, 0)),
                   row(LANES), row(2 * D_MODEL)],
        out_shape=[jax.ShapeDtypeStruct((nb, ATTN_WIDTH, MOBA_BLOCK), F32),
                   jax.ShapeDtypeStruct((s, ATTN_AUG_WIDTH), BF16),
                   jax.ShapeDtypeStruct((nb, ATTN_HEADS * V_ROWS, MOBA_BLOCK), BF16),
                   jax.ShapeDtypeStruct((nb, 2, ATTN_AUG_WIDTH), F32),
                   jax.ShapeDtypeStruct((s, 2 * MLSTM_WIDTH), F32),
                   jax.ShapeDtypeStruct((nb, 2 * MLSTM_WIDTH, MOBA_BLOCK), F32),
                   jax.ShapeDtypeStruct((s, LANES), F32),
                   jax.ShapeDtypeStruct((s, 2 * D_MODEL), F32)],
        compiler_params=pltpu.CompilerParams(dimension_semantics=("arbitrary",),
                                             vmem_limit_bytes=VMEM_LIMIT),
        name="inproj",
    )(x, g, wq, wk, wv, wm, wvo, wif, wg, alibi)


def _col_max(st):
    rows, n = st.shape
    fan = 8
    while rows > fan * 8 and rows % fan == 0:
        st = jnp.max(st.reshape(fan, rows // fan, n), axis=0)
        rows //= fan
    return jnp.max(st, axis=0, keepdims=True)


def _moba_kernel(slope_ref, q_ref, qn_ref, kmean_ref, kn2_ref, k_ref, vt_ref, o_ref,
                 s_ref, qa_ref, cm_ref, reach_ref):
    step_id = pl.program_id(1)
    n_steps = pl.num_programs(1)
    i = step_id * Q_TILES
    tq = Q_TILES * MOBA_BLOCK
    nb = N_BLOCK_COLS
    group_rows = KEY_GROUP * MOBA_BLOCK
    assert KEY_GROUP % Q_TILES == 0
    heads = range(HEADS_PER_STEP)
    head_lanes = [slice(hh * AUG, (hh + 1) * AUG) for hh in heads]
    head_rows = [slice(hh * V_ROWS, (hh + 1) * V_ROWS) for hh in heads]
    own_slot = 2

    def prepare(qsrc_ref, tile0):
        blk = lax.broadcasted_iota(jnp.int32, (nb, tq), 0).astype(F32)
        i_f = (tile0 + lax.broadcasted_iota(jnp.int32, (1, tq), 1) // MOBA_BLOCK).astype(F32)
        reach = jnp.zeros((1, 1), F32)
        qk_cap, slopes = [], []
        for hh in heads:
            slope = slope_ref[0, hh:hh + 1, 0:1]
            qh = jnp.concatenate([qsrc_ref[t, hh * ATTN_HEAD_DIM:(hh + 1) * ATTN_HEAD_DIM, :]
                                  for t in range(Q_TILES)], axis=1)
            km = kmean_ref[:, hh * AUG:hh * AUG + ATTN_HEAD_DIM]
            gate = jnp.dot(km, qh, precision=lax.Precision.HIGHEST, preferred_element_type=F32)
            valid = blk < i_f
            g = jnp.where(valid, gate, -jnp.inf)
            sel = jnp.zeros((nb, tq), jnp.bool_)
            for _ in range(MOBA_TOP_K):
                mx = jnp.max(g, axis=0, keepdims=True)
                idx = jnp.min(jnp.where(g == mx, blk, float(nb)), axis=0, keepdims=True)
                pick = blk == idx
                sel = jnp.logical_or(sel, pick)
                g = jnp.where(pick, -jnp.inf, g)
            bias = jnp.where(sel, (blk - i_f) * (slope * float(MOBA_BLOCK)), NEG_BIAS)
            bias = jnp.where(valid, bias, 0.0)
            bias = jnp.where(blk == float(nb - 1), 1.0, bias)
            qa_ref[hh] = jnp.concatenate([qh * (ATTN_HEAD_DIM ** -0.5), bias], axis=0).astype(BF16)
            k_norm = jnp.sqrt(jnp.max(kn2_ref[:, hh * AUG:hh * AUG + 1], axis=0, keepdims=True))
            qk_cap.append(NORM_SLACK * (ATTN_HEAD_DIM ** -0.5) * k_norm
                          * jnp.sqrt(jnp.sum(qh * qh, axis=0, keepdims=True)))
            slopes.append(slope)
        first = (tile0 // KEY_GROUP) * KEY_GROUP
        hidden = (lax.broadcasted_iota(jnp.int32, (group_rows, tq), 0)
                  - lax.broadcasted_iota(jnp.int32, (group_rows, tq), 1)) > (tile0 - first) * MOBA_BLOCK
        start = pl.multiple_of(first * MOBA_BLOCK, group_rows)
        for hh in heads:
            st = jnp.dot(k_ref[pl.ds(start, group_rows), head_lanes[hh]], qa_ref[hh], preferred_element_type=F32)
            st = jnp.where(hidden, -jnp.inf, st)
            s_ref[own_slot, hh] = st
            own_max = _col_max(st)
            cm_ref[hh] = jnp.broadcast_to(own_max, (8, tq))
            spread = jnp.max(qk_cap[hh] - own_max, axis=1, keepdims=True)
            reach = jnp.maximum(reach, (NEGLIGIBLE_LOG + spread) / (slopes[hh] * float(MOBA_BLOCK)) + 1.0)
        reach = jnp.ceil(reach)
        reach = jnp.where(reach < float(nb), reach, float(nb))
        reach_ref[...] = jnp.broadcast_to(reach, reach_ref.shape).astype(jnp.int32)

    def score(first, slot):
        start = pl.multiple_of(first * MOBA_BLOCK, group_rows)
        col_max = []
        for hh in heads:
            st = jnp.dot(k_ref[pl.ds(start, group_rows), head_lanes[hh]], qa_ref[hh], preferred_element_type=F32)
            s_ref[slot, hh] = st
            col_max.append(_col_max(st))
        return col_max

    def consume(first, slot, m_prev, m_cur, accs):
        ps = [jnp.exp(s_ref[slot, hh] - m_cur[hh]).astype(BF16) for hh in heads]
        accs = [jnp.exp(m_prev[hh] - m_cur[hh]) * accs[hh] for hh in heads]
        for b in range(KEY_GROUP):
            for hh in heads:
                accs[hh] = accs[hh] + jnp.dot(vt_ref[first + b, head_rows[hh], :],
                                              ps[hh][b * MOBA_BLOCK:(b + 1) * MOBA_BLOCK, :],
                                              preferred_element_type=F32)
        return accs

    def step(first, slot, src, carry):
        pend, m_prev, m_cur, accs = carry
        cm = score(first, slot)
        accs = consume(pend, src, m_prev, m_cur, accs)
        return (jnp.asarray(first, jnp.int32), m_cur,
                [jnp.maximum(m_cur[hh], cm[hh]) for hh in heads], accs)

    def pair(first_group, carry, src=0):
        g = first_group * KEY_GROUP
        return step(g + KEY_GROUP, 0, 1, step(g, 1, src, carry))

    @pl.when(step_id == 0)
    def _():
        prepare(q_ref, i)

    own_first = (i // KEY_GROUP) * KEY_GROUP
    first_block = jnp.maximum(i - jnp.max(reach_ref[...]) + 1, 0)
    first_group = first_block // KEY_GROUP
    n_past = i // KEY_GROUP - first_group
    cm0 = [cm_ref[hh, 0:1, :] for hh in heads]
    init = (jnp.asarray(own_first, jnp.int32), cm0, cm0, [jnp.zeros((V_ROWS, tq), F32) for _ in heads])

    def start_none(c):
        for hh in heads:
            s_ref[0, hh] = s_ref[own_slot, hh]
        return c

    def start_odd(c):
        return step(first_group * KEY_GROUP, 0, own_slot, c)

    def start_even(c):
        return pair(first_group, c, src=own_slot)

    odd = n_past % 2
    carry = lax.cond(n_past == 0, start_none,
                     lambda c: lax.cond(odd == 1, start_odd, start_even, c), init)
    done = first_group + jnp.where(n_past == 0, 0, 2 - odd)
    n_pairs = (i // KEY_GROUP - done) // 2
    odd_pair = n_pairs % 2
    carry = lax.cond(odd_pair == 1, functools.partial(pair, done), lambda c: c, carry)
    done = done + 2 * odd_pair

    def quad(u, carry):
        return pair(done + 4 * u + 2, pair(done + 4 * u, carry))

    pend, m_prev, m_cur, accs = lax.fori_loop(0, n_pairs // 2, quad, carry)
    accs = consume(pend, 0, m_prev, m_cur, accs)
    prepare(qn_ref, jnp.minimum(step_id + 1, n_steps - 1) * Q_TILES)
    outs = []
    for hh in heads:
        ot = accs[hh][:ATTN_HEAD_DIM, :] / accs[hh][ATTN_HEAD_DIM:ATTN_HEAD_DIM + 1, :]
        outs.append(ot.T)
    o_ref[...] = jnp.concatenate(outs, axis=1).astype(o_ref.dtype)


def _moba(slope_tab, qt, kmean, knorm2, k_aug, vt_aug):
    s = k_aug.shape[0]
    nb = s // MOBA_BLOCK
    n_pairs = ATTN_HEADS // HEADS_PER_STEP
    n_steps = nb // Q_TILES
    tq = Q_TILES * MOBA_BLOCK
    q_block = (Q_TILES, HEADS_PER_STEP * ATTN_HEAD_DIM, MOBA_BLOCK)
    return pl.pallas_call(
        _moba_kernel,
        grid=(n_pairs, n_steps),
        in_specs=[pl.BlockSpec((1, HEADS_PER_STEP, LANES), lambda p, i: (p, 0, 0)),
                  pl.BlockSpec(q_block, lambda p, i: (i, p, 0)),
                  pl.BlockSpec(q_block, lambda p, i: (jnp.minimum(i + 1, n_steps - 1), p, 0)),
                  pl.BlockSpec((N_BLOCK_COLS, HEADS_PER_STEP * AUG), lambda p, i: (0, p)),
                  pl.BlockSpec((N_BLOCK_COLS, HEADS_PER_STEP * AUG), lambda p, i: (0, p)),
                  pl.BlockSpec((s, HEADS_PER_STEP * AUG), lambda p, i: (0, p)),
                  pl.BlockSpec((nb, HEADS_PER_STEP * V_ROWS, MOBA_BLOCK), lambda p, i: (0, p, 0))],
        out_specs=pl.BlockSpec((tq, HEADS_PER_STEP * ATTN_HEAD_DIM), lambda p, i: (i, p)),
        out_shape=jax.ShapeDtypeStruct((s, ATTN_WIDTH), BF16),
        scratch_shapes=[pltpu.VMEM((3, HEADS_PER_STEP, KEY_GROUP * MOBA_BLOCK, tq), F32),
                        pltpu.VMEM((HEADS_PER_STEP, AUG, tq), BF16),
                        pltpu.VMEM((HEADS_PER_STEP, 8, tq), F32),
                        pltpu.VMEM((8, LANES), jnp.int32)],
        compiler_params=pltpu.CompilerParams(dimension_semantics=("arbitrary", "arbitrary"),
                                             vmem_limit_bytes=VMEM_LIMIT),
        name="moba",
    )(slope_tab, qt, qt, kmean, knorm2, k_aug, vt_aug)


def _mlstm_kernel(qk_ref, vot_ref, gates_ref, conv_ref, gbias_ref, ngt_ref, y_ref,
                  xbuf, ct_ref, n_ref, m_ref):
    L = MLSTM_CHUNK
    d = MLSTM_HEAD_DIM
    rows = qk_ref.shape[0]
    halo = xbuf.shape[0] - rows
    chunks = range(rows // L)
    heads = range(MLSTM_HEADS)

    @pl.when(pl.program_id(0) == 0)
    def _():
        xbuf[0:halo, :] = jnp.zeros((halo, xbuf.shape[1]), F32)
        ct_ref[...] = jnp.zeros(ct_ref.shape, F32)
        n_ref[...] = jnp.zeros(n_ref.shape, F32)
        m_ref[...] = jnp.zeros(m_ref.shape, F32)

    xbuf[halo:, :] = qk_ref[...]
    w = conv_ref[...]
    y = qk_ref[...] * w[CONV_WIDTH - 1:CONV_WIDTH, :]
    for j in range(CONV_WIDTH - 1):
        off = halo - CONV_WIDTH + 1 + j
        y = y + xbuf[off:off + rows, :] * w[j:j + 1, :]
    xbuf[0:halo, :] = xbuf[rows:rows + halo, :]
    qk = y * _sigmoid(y)

    a = gates_ref[...] + gbias_ref[...]
    lane = lax.broadcasted_iota(jnp.int32, (rows, LANES), 1)
    is_f = jnp.logical_and(lane >= MLSTM_HEADS, lane < 2 * MLSTM_HEADS)
    log_f = jnp.where(is_f, jnp.minimum(a, 0.0) - jnp.log(1.0 + jnp.exp(-jnp.abs(a))), 0.0)
    t_io = lax.broadcasted_iota(jnp.int32, (rows, rows), 0)
    s_io = lax.broadcasted_iota(jnp.int32, (rows, rows), 1)
    same_chunk = (t_io // L) == (s_io // L)
    tri = jnp.where(jnp.logical_and(s_io <= t_io, same_chunk), 1.0, 0.0)
    bcum = jnp.dot(tri, log_f, precision=lax.Precision.HIGHEST,
                   preferred_element_type=F32)
    colm = jnp.where(lane < MLSTM_HEADS, a, bcum)
    rowm = colm.T
    causal_t = (lax.broadcasted_iota(jnp.int32, (L, L), 0) <= lax.broadcasted_iota(jnp.int32, (L, L), 1))

    pairs = [(cc, hd) for cc in chunks for hd in heads]
    rs = lambda cc: slice(cc * L, (cc + 1) * L)
    fs = lambda hd: slice(hd * d, (hd + 1) * d)
    qtb = {(cc, hd): qk[rs(cc), fs(hd)].T.astype(BF16) for cc, hd in pairs}
    kb = {(cc, hd): (qk[rs(cc), MLSTM_WIDTH + hd * d:MLSTM_WIDTH + (hd + 1) * d] * (d ** -0.5)).astype(BF16)
          for cc, hd in pairs}
    vt = {(cc, hd): vot_ref[0, fs(hd), rs(cc)] for cc, hd in pairs}
    ic_r = {(cc, hd): rowm[hd:hd + 1, rs(cc)] for cc, hd in pairs}
    bc_r = {(cc, hd): rowm[MLSTM_HEADS + hd:MLSTM_HEADS + hd + 1, rs(cc)] for cc, hd in pairs}
    e_c = {(cc, hd): colm[rs(cc), hd:hd + 1] - colm[rs(cc), MLSTM_HEADS + hd:MLSTM_HEADS + hd + 1]
           for cc, hd in pairs}
    dmat_t = {p: jnp.where(causal_t, bc_r[p] + e_c[p], -jnp.inf) for p in pairs}
    a_row = {p: jnp.max(dmat_t[p], axis=0, keepdims=True) for p in pairs}
    sc_t = {p: jnp.dot(kb[p], qtb[p], preferred_element_type=F32) * jnp.exp(dmat_t[p] - a_row[p]) for p in pairs}
    sv_t = {p: jnp.dot(vt[p].astype(BF16), sc_t[p].astype(BF16), preferred_element_type=F32) for p in pairs}
    ssum = {p: jnp.sum(sc_t[p], axis=0, keepdims=True) for p in pairs}
    b_last = {p: bc_r[p][:, L - 1:L] for p in pairs}
    g_r = {p: b_last[p] - bc_r[p] + ic_r[p] for p in pairs}
    gmax = {p: jnp.max(g_r[p], axis=1, keepdims=True) for p in pairs}
    w_row = {p: jnp.exp(g_r[p] - gmax[p]) for p in pairs}
    upd_t = {p: jnp.dot((vt[p] * w_row[p]).astype(BF16), kb[p], preferred_element_type=F32) for p in pairs}
    ksum = {p: jnp.dot(jnp.broadcast_to(w_row[p], (8, L)).astype(BF16), kb[p],
                       preferred_element_type=F32)[0:1, :] for p in pairs}
    h_t = {}
    for hd in heads:
        ct_st, n_st, m_prev = ct_ref[hd], n_ref[hd], m_ref[hd:hd + 1, 0:1]
        for cc in chunks:
            p = (cc, hd)
            inter = bc_r[p] + m_prev
            m_row = jnp.maximum(inter, a_row[p])
            r_row = jnp.exp(a_row[p] - m_row)
            w_inter = jnp.exp(inter - m_row)
            qc_t = jnp.dot(ct_st.astype(BF16), qtb[p], preferred_element_type=F32)
            qn = jnp.dot(jnp.broadcast_to(n_st, (8, d)).astype(BF16), qtb[p],
                         preferred_element_type=F32)[0:1, :]
            den = w_inter * qn + r_row * ssum[p]
            scale = 1.0 / jnp.maximum(jnp.abs(den), jnp.exp(-m_row))
            h_t[p] = (w_inter * scale) * qc_t + (r_row * scale) * sv_t[p]
            m_new = jnp.maximum(b_last[p] + m_prev, gmax[p])
            decay = jnp.exp(b_last[p] + m_prev - m_new)
            gain = jnp.exp(gmax[p] - m_new)
            ct_st = decay * ct_st + gain * upd_t[p]
            n_st = decay * n_st + gain * ksum[p]
            m_prev = m_new
        ct_ref[hd], n_ref[hd] = ct_st, n_st
        m_ref[hd:hd + 1, :] = jnp.broadcast_to(m_prev, (1, LANES))
    for cc, hd in pairs:
        og_t = vot_ref[0, MLSTM_WIDTH + hd * d:MLSTM_WIDTH + (hd + 1) * d, rs(cc)]
        hg = h_t[(cc, hd)] * _sigmoid(og_t)
        mu = jnp.mean(hg, axis=0, keepdims=True)
        var = jnp.mean(jnp.square(hg - mu), axis=0, keepdims=True)
        yn = (hg - mu) * lax.rsqrt(var + NORM_EPS) * ngt_ref[fs(hd), :]
        y_ref[rs(cc), fs(hd)] = yn.T.astype(y_ref.dtype)


def _mlstm(qkm, vot, gates, conv, gbias, ngt):
    s = qkm.shape[0]
    rows = MOBA_BLOCK
    row = lambda w: pl.BlockSpec((rows, w), lambda t: (t, 0))
    whole = pl.BlockSpec(memory_space=pltpu.VMEM)
    return pl.pallas_call(
        _mlstm_kernel,
        grid=(s // rows,),
        in_specs=[row(2 * MLSTM_WIDTH), pl.BlockSpec((1, 2 * MLSTM_WIDTH, rows), lambda t: (t, 0, 0)),
                  row(LANES), whole, whole, whole],
        out_specs=row(MLSTM_WIDTH),
        out_shape=jax.ShapeDtypeStruct((s, MLSTM_WIDTH), BF16),
        scratch_shapes=[pltpu.VMEM((rows + CONV_HALO, 2 * MLSTM_WIDTH), F32),
                        pltpu.VMEM((MLSTM_HEADS, MLSTM_HEAD_DIM, MLSTM_HEAD_DIM), F32),
                        pltpu.VMEM((MLSTM_HEADS, 1, MLSTM_HEAD_DIM), F32),
                        pltpu.VMEM((8, LANES), F32)],
        compiler_params=pltpu.CompilerParams(dimension_semantics=("arbitrary",),
                                             vmem_limit_bytes=VMEM_LIMIT),
        name="mlstm",
    )(qkm, vot, gates, conv, gbias, ngt)


def _out_ffn_kernel(x_ref, ya_ref, ym_ref, gg_ref, woa_ref, wom_ref, wo_ref, fg_ref,
                    wgt_ref, wup_ref, wdn_ref, fin_ref, o_ref, *, final_norm):
    a = jnp.dot(ya_ref[...], woa_ref[...], preferred_element_type=F32)
    b = jnp.dot(ym_ref[...], wom_ref[...], preferred_element_type=F32)
    merged = _sigmoid(gg_ref[:, :D_MODEL]) * a + _sigmoid(gg_ref[:, D_MODEL:]) * b
    x1 = x_ref[...] + jnp.dot(merged.astype(BF16), wo_ref[...], preferred_element_type=F32)
    h2 = _rms(x1, fg_ref[...]).astype(BF16)
    acc = jnp.zeros(x1.shape, F32)
    for c in range(0, FFN_HIDDEN, FFN_CHUNK):
        gt = jnp.dot(h2, wgt_ref[:, c:c + FFN_CHUNK], preferred_element_type=F32)
        up = jnp.dot(h2, wup_ref[:, c:c + FFN_CHUNK], preferred_element_type=F32)
        act = (gt * _sigmoid(gt) * up).astype(BF16)
        acc = acc + jnp.dot(act, wdn_ref[c:c + FFN_CHUNK, :], preferred_element_type=F32)
    x2 = x1 + acc
    o_ref[...] = _rms(x2, fin_ref[...]) if final_norm else x2


def _out_ffn(x, ya, ym, gg, woa, wom, wo, fg, wgt, wup, wdn, fin, final_norm):
    s = x.shape[0]
    row = lambda w: pl.BlockSpec((OUT_TILE, w), lambda i: (i, 0))
    whole = pl.BlockSpec(memory_space=pltpu.VMEM)
    return pl.pallas_call(
        functools.partial(_out_ffn_kernel, final_norm=final_norm),
        grid=(s // OUT_TILE,),
        in_specs=[row(D_MODEL), row(ATTN_WIDTH), row(MLSTM_WIDTH), row(2 * D_MODEL)] + [whole] * 8,
        out_specs=row(D_MODEL),
        out_shape=jax.ShapeDtypeStruct((s, D_MODEL), F32),
        compiler_params=pltpu.CompilerParams(dimension_semantics=("arbitrary",),
                                             vmem_limit_bytes=VMEM_LIMIT),
        name="out_ffn",
    )(x, ya, ym, gg, woa, wom, wo, fg, wgt, wup, wdn, fin)


def _alibi_slopes():
    return np.exp2(-8.0 * np.arange(1, ATTN_HEADS + 1, dtype=np.float64) / ATTN_HEADS).astype(np.float32)


def _pad_heads(w, width):
    d = w.shape[0]
    w = w.reshape(d, ATTN_HEADS, ATTN_HEAD_DIM)
    w = jnp.pad(w, ((0, 0), (0, 0), (0, width - ATTN_HEAD_DIM)))
    return w.reshape(d, ATTN_HEADS * width)


def kernel(x, mix_norm_g, w_in, conv_qk, b_igate, b_fgate, mlstm_norm_g, w_out_attn, w_out_mlstm,
           w_o, ffn_norm_g, w_ffn_gate, w_ffn_up, w_ffn_down, final_norm_g):
    batch, seq, _ = x.shape
    depth = w_in.shape[0]
    assert seq % OUT_TILE == 0 and seq // MOBA_BLOCK <= N_BLOCK_COLS
    assert seq % (KEY_GROUP * MOBA_BLOCK) == 0
    slopes = _alibi_slopes()
    alibi = np.zeros((1, ATTN_AUG_WIDTH), np.float32)
    alibi[0, AUG - 1::AUG] = slopes
    alibi = jnp.asarray(alibi)
    slope_tab = jnp.asarray(np.broadcast_to(
        slopes.reshape(ATTN_HEADS // HEADS_PER_STEP, HEADS_PER_STEP, 1),
        (ATTN_HEADS // HEADS_PER_STEP, HEADS_PER_STEP, LANES)).copy())
    a0, a1, a2 = ATTN_WIDTH, 2 * ATTN_WIDTH, 3 * ATTN_WIDTH
    m1 = a2 + 4 * MLSTM_WIDTH
    g0 = m1 + 2 * MLSTM_HEADS

    outs = []
    for bi in range(batch):
        xb = x[bi]
        for layer in range(depth):
            w = w_in[layer]
            wq = w[:, :a0].T.astype(BF16)
            wk = w[:, a0:a1].astype(BF16)
            wv = _pad_heads(w[:, a1:a2], V_ROWS).T.astype(BF16)
            wm = w[:, a2:a2 + 2 * MLSTM_WIDTH].astype(BF16)
            wvo = w[:, a2 + 2 * MLSTM_WIDTH:m1].T.astype(BF16)
            wif = jnp.pad(w[:, m1:g0], ((0, 0), (0, LANES - 2 * MLSTM_HEADS))).astype(BF16)
            wg = w[:, g0:].astype(BF16)
            q, k_aug, v_aug, kstat, qkm, vom, gates, gg = _inproj(
                xb, mix_norm_g[layer][None, :], wq, wk, wv, wm, wvo, wif, wg, alibi)
            pad_blocks = ((0, N_BLOCK_COLS - kstat.shape[0]), (0, 0))
            ya = _moba(slope_tab, q, jnp.pad(kstat[:, 0, :], pad_blocks), jnp.pad(kstat[:, 1, :], pad_blocks),
                       k_aug, v_aug)
            gbias = jnp.pad(jnp.concatenate([b_igate[layer], b_fgate[layer]])[None, :],
                            ((0, 0), (0, LANES - 2 * MLSTM_HEADS))).astype(F32)
            ngt = jnp.broadcast_to(mlstm_norm_g[layer][:, None], (MLSTM_WIDTH, MLSTM_CHUNK)).astype(F32)
            ym = _mlstm(qkm, vom, gates, conv_qk[layer], gbias, ngt)
            xb = _out_ffn(xb, ya, ym, gg,
                          w_out_attn[layer].astype(BF16), w_out_mlstm[layer].astype(BF16),
                          w_o[layer].astype(BF16), ffn_norm_g[layer][None, :],
                          w_ffn_gate[layer].astype(BF16), w_ffn_up[layer].astype(BF16),
                          w_ffn_down[layer].astype(BF16), final_norm_g[None, :],
                          final_norm=(layer == depth - 1))
        outs.append(xb)
    return outs[0][None] if batch == 1 else jnp.stack(outs, axis=0)
```

```python
import functools

import numpy as np
import jax
import jax.numpy as jnp
from jax import lax
from jax.experimental import pallas as pl
from jax.experimental.pallas import tpu as pltpu

D_MODEL = 1024
ATTN_HEADS = 8
ATTN_HEAD_DIM = 64
ATTN_WIDTH = ATTN_HEADS * ATTN_HEAD_DIM
MOBA_BLOCK = 256
MOBA_TOP_K = 3
MLSTM_HEADS = 4
MLSTM_HEAD_DIM = 128
MLSTM_WIDTH = MLSTM_HEADS * MLSTM_HEAD_DIM
MLSTM_CHUNK = 128
CONV_WIDTH = 4
FFN_HIDDEN = 2816
NORM_EPS = 1e-6

LANES = 128
AUG = 2 * ATTN_HEAD_DIM
ATTN_AUG_WIDTH = ATTN_HEADS * AUG
V_ROWS = 80
N_BLOCK_COLS = AUG - ATTN_HEAD_DIM
HEADS_PER_STEP = 2
CONV_HALO = 8
Q_TILES = 2
KEY_GROUP = 2
NEGLIGIBLE_LOG = 50.0
NORM_SLACK = 1.02
NEG_BIAS = -1e9
FFN_CHUNK = 256
OUT_TILE = 512
VMEM_LIMIT = 56 * 1024 * 1024

F32 = jnp.float32
BF16 = jnp.bfloat16
NT_DIMS = (((1,), (1,)), ((), ()))


def _rms(x, g):
    return x * lax.rsqrt(jnp.mean(x * x, axis=-1, keepdims=True) + NORM_EPS) * g


def _sigmoid(x):
    return 1.0 / (1.0 + jnp.exp(-x))


def _inproj_kernel(x_ref, g_ref, wq_ref, wk_ref, wv_ref, wm_ref, wvo_ref, wif_ref, wg_ref, alibi_ref,
                   q_ref, k_ref, v_ref, kmean_ref, qkm_ref, vom_ref, gates_ref, gg_ref, *, blk):
    hb = _rms(x_ref[...], g_ref[...]).astype(BF16)
    q_ref[0] = lax.dot_general(wq_ref[...], hb, NT_DIMS, preferred_element_type=F32)
    kc = jnp.dot(hb, wk_ref[...], preferred_element_type=F32)
    gap = jnp.zeros((MOBA_BLOCK, AUG - ATTN_HEAD_DIM), F32)
    kf = jnp.concatenate([piece for h in range(ATTN_HEADS)
                          for piece in (kc[:, h * ATTN_HEAD_DIM:(h + 1) * ATTN_HEAD_DIM], gap)], axis=1)
    ksq = kf * kf
    norm2 = [jnp.max(jnp.sum(ksq[:, h * AUG:(h + 1) * AUG], axis=1, keepdims=True), axis=0, keepdims=True)
             for h in range(ATTN_HEADS)]
    kmean_ref[0] = jnp.concatenate(
        [jnp.mean(kf, axis=0, keepdims=True),
         jnp.concatenate([jnp.broadcast_to(n2, (1, AUG)) for n2 in norm2], axis=1)], axis=0)
    lane = lax.broadcasted_iota(jnp.int32, (1, ATTN_AUG_WIDTH), 1) & (AUG - 1)
    row = lax.broadcasted_iota(jnp.int32, (MOBA_BLOCK, 1), 0).astype(F32)
    onehot = jnp.where(lane - ATTN_HEAD_DIM == blk, 1.0, 0.0)
    onehot = jnp.where(lane == AUG - 1, 0.0, onehot)
    k_ref[...] = (kf + onehot + alibi_ref[...] * row).astype(BF16)
    gg_ref[:, :D_MODEL] = jnp.dot(hb, wg_ref[:, :D_MODEL], preferred_element_type=F32)
    yield
    vt = lax.dot_general(wv_ref[...], hb, NT_DIMS, preferred_element_type=F32)
    feat = lax.rem(lax.broadcasted_iota(jnp.int32, (ATTN_HEADS * V_ROWS, 1), 0), V_ROWS)
    v_ref[0] = (vt + jnp.where(feat == ATTN_HEAD_DIM, 1.0, 0.0)).astype(BF16)
    qkm_ref[...] = jnp.dot(hb, wm_ref[...], preferred_element_type=F32)
    gates_ref[...] = jnp.dot(hb, wif_ref[...], preferred_element_type=F32)
    yield
    vom_ref[0] = lax.dot_general(wvo_ref[...], hb, NT_DIMS, preferred_element_type=F32)
    gg_ref[:, D_MODEL:] = jnp.dot(hb, wg_ref[:, D_MODEL:], preferred_element_type=F32)


def _col_max(st):
    rows, n = st.shape
    fan = 8
    while rows > fan * 8 and rows % fan == 0:
        st = jnp.max(st.reshape(fan, rows // fan, n), axis=0)
        rows //= fan
    return jnp.max(st, axis=0, keepdims=True)


def _moba_kernel(slope_ref, q_ref, qn_ref, kmean_ref, kn2_ref, k_ref, vt_ref, o_ref,
                 s_ref, qa_ref, cm_ref, reach_ref):
    step_id = pl.program_id(1)
    n_steps = pl.num_programs(1)
    i = step_id * Q_TILES
    tq = Q_TILES * MOBA_BLOCK
    nb = N_BLOCK_COLS
    group_rows = KEY_GROUP * MOBA_BLOCK
    assert KEY_GROUP % Q_TILES == 0
    heads = range(HEADS_PER_STEP)
    head_lanes = [slice(hh * AUG, (hh + 1) * AUG) for hh in heads]
    head_rows = [slice(hh * V_ROWS, (hh + 1) * V_ROWS) for hh in heads]
    own_slot = 2

    def prepare(qsrc_ref, tile0):
        blk = lax.broadcasted_iota(jnp.int32, (nb, tq), 0).astype(F32)
        i_f = (tile0 + lax.broadcasted_iota(jnp.int32, (1, tq), 1) // MOBA_BLOCK).astype(F32)
        reach = jnp.zeros((1, 1), F32)
        qk_cap, slopes = [], []
        for hh in heads:
            slope = slope_ref[0, hh:hh + 1, 0:1]
            qh = jnp.concatenate([qsrc_ref[t, hh * ATTN_HEAD_DIM:(hh + 1) * ATTN_HEAD_DIM, :]
                                  for t in range(Q_TILES)], axis=1)
            km = kmean_ref[:, hh * AUG:hh * AUG + ATTN_HEAD_DIM]
            gate = jnp.dot(km, qh, precision=lax.Precision.HIGHEST, preferred_element_type=F32)
            valid = blk < i_f
            g = jnp.where(valid, gate, -jnp.inf)
            sel = jnp.zeros((nb, tq), jnp.bool_)
            for _ in range(MOBA_TOP_K):
                mx = jnp.max(g, axis=0, keepdims=True)
                idx = jnp.min(jnp.where(g == mx, blk, float(nb)), axis=0, keepdims=True)
                pick = blk == idx
                sel = jnp.logical_or(sel, pick)
                g = jnp.where(pick, -jnp.inf, g)
            bias = jnp.where(sel, (blk - i_f) * (slope * float(MOBA_BLOCK)), NEG_BIAS)
            bias = jnp.where(valid, bias, 0.0)
            bias = jnp.where(blk == float(nb - 1), 1.0, bias)
            qa_ref[hh] = jnp.concatenate([qh * (ATTN_HEAD_DIM ** -0.5), bias], axis=0).astype(BF16)
            k_norm = jnp.sqrt(jnp.max(kn2_ref[:, hh * AUG:hh * AUG + 1], axis=0, keepdims=True))
            qk_cap.append(NORM_SLACK * (ATTN_HEAD_DIM ** -0.5) * k_norm
                          * jnp.sqrt(jnp.sum(qh * qh, axis=0, keepdims=True)))
            slopes.append(slope)
        first = (tile0 // KEY_GROUP) * KEY_GROUP
        hidden = (lax.broadcasted_iota(jnp.int32, (group_rows, tq), 0)
                  - lax.broadcasted_iota(jnp.int32, (group_rows, tq), 1)) > (tile0 - first) * MOBA_BLOCK
        start = pl.multiple_of(first * MOBA_BLOCK, group_rows)
        for hh in heads:
            st = jnp.dot(k_ref[pl.ds(start, group_rows), head_lanes[hh]], qa_ref[hh], preferred_element_type=F32)
            st = jnp.where(hidden, -jnp.inf, st)
            s_ref[own_slot, hh] = st
            own_max = _col_max(st)
            cm_ref[hh] = jnp.broadcast_to(own_max, (8, tq))
            spread = jnp.max(qk_cap[hh] - own_max, axis=1, keepdims=True)
            reach = jnp.maximum(reach, (NEGLIGIBLE_LOG + spread) / (slopes[hh] * float(MOBA_BLOCK)) + 1.0)
        reach = jnp.ceil(reach)
        reach = jnp.where(reach < float(nb), reach, float(nb))
        reach_ref[...] = jnp.broadcast_to(reach, reach_ref.shape).astype(jnp.int32)

    def score(first, slot):
        start = pl.multiple_of(first * MOBA_BLOCK, group_rows)
        col_max = []
        for hh in heads:
            st = jnp.dot(k_ref[pl.ds(start, group_rows), head_lanes[hh]], qa_ref[hh], preferred_element_type=F32)
            s_ref[slot, hh] = st
            col_max.append(_col_max(st))
        return col_max

    def consume(first, slot, m_prev, m_cur, accs):
        ps = [jnp.exp(s_ref[slot, hh] - m_cur[hh]).astype(BF16) for hh in heads]
        accs = [jnp.exp(m_prev[hh] - m_cur[hh]) * accs[hh] for hh in heads]
        for b in range(KEY_GROUP):
            for hh in heads:
                accs[hh] = accs[hh] + jnp.dot(vt_ref[first + b, head_rows[hh], :],
                                              ps[hh][b * MOBA_BLOCK:(b + 1) * MOBA_BLOCK, :],
                                              preferred_element_type=F32)
        return accs

    def step(first, slot, src, carry):
        pend, m_prev, m_cur, accs = carry
        cm = score(first, slot)
        accs = consume(pend, src, m_prev, m_cur, accs)
        return (jnp.asarray(first, jnp.int32), m_cur,
                [jnp.maximum(m_cur[hh], cm[hh]) for hh in heads], accs)

    def pair(first_group, carry, src=0):
        g = first_group * KEY_GROUP
        return step(g + KEY_GROUP, 0, 1, step(g, 1, src, carry))

    @pl.when(step_id == 0)
    def _():
        prepare(q_ref, i)

    own_first = (i // KEY_GROUP) * KEY_GROUP
    first_block = jnp.maximum(i - jnp.max(reach_ref[...]) + 1, 0)
    first_group = first_block // KEY_GROUP
    n_past = i // KEY_GROUP - first_group
    cm0 = [cm_ref[hh, 0:1, :] for hh in heads]
    init = (jnp.asarray(own_first, jnp.int32), cm0, cm0, [jnp.zeros((V_ROWS, tq), F32) for _ in heads])

    def start_none(c):
        for hh in heads:
            s_ref[0, hh] = s_ref[own_slot, hh]
        return c

    def start_odd(c):
        return step(first_group * KEY_GROUP, 0, own_slot, c)

    def start_even(c):
        return pair(first_group, c, src=own_slot)

    odd = n_past % 2
    carry = lax.cond(n_past == 0, start_none,
                     lambda c: lax.cond(odd == 1, start_odd, start_even, c), init)
    done = first_group + jnp.where(n_past == 0, 0, 2 - odd)
    n_pairs = (i // KEY_GROUP - done) // 2
    odd_pair = n_pairs % 2
    carry = lax.cond(odd_pair == 1, functools.partial(pair, done), lambda c: c, carry)
    done = done + 2 * odd_pair

    def quad(u, carry):
        return pair(done + 4 * u + 2, pair(done + 4 * u, carry))

    pend, m_prev, m_cur, accs = lax.fori_loop(0, n_pairs // 2, quad, carry)
    accs = consume(pend, 0, m_prev, m_cur, accs)
    prepare(qn_ref, jnp.minimum(step_id + 1, n_steps - 1) * Q_TILES)
    outs = []
    for hh in heads:
        ot = accs[hh][:ATTN_HEAD_DIM, :] / accs[hh][ATTN_HEAD_DIM:ATTN_HEAD_DIM + 1, :]
        outs.append(ot.T)
    o_ref[...] = jnp.concatenate(outs, axis=1).astype(o_ref.dtype)


def _moba(slope_tab, qt, kmean, knorm2, k_aug, vt_aug):
    s = k_aug.shape[0]
    nb = s // MOBA_BLOCK
    n_pairs = ATTN_HEADS // HEADS_PER_STEP
    n_steps = nb // Q_TILES
    tq = Q_TILES * MOBA_BLOCK
    q_block = (Q_TILES, HEADS_PER_STEP * ATTN_HEAD_DIM, MOBA_BLOCK)
    return pl.pallas_call(
        _moba_kernel,
        grid=(n_pairs, n_steps),
        in_specs=[pl.BlockSpec((1, HEADS_PER_STEP, LANES), lambda p, i: (p, 0, 0)),
                  pl.BlockSpec(q_block, lambda p, i: (i, p, 0)),
                  pl.BlockSpec(q_block, lambda p, i: (jnp.minimum(i + 1, n_steps - 1), p, 0)),
                  pl.BlockSpec((N_BLOCK_COLS, HEADS_PER_STEP * AUG), lambda p, i: (0, p)),
                  pl.BlockSpec((N_BLOCK_COLS, HEADS_PER_STEP * AUG), lambda p, i: (0, p)),
                  pl.BlockSpec((s, HEADS_PER_STEP * AUG), lambda p, i: (0, p)),
                  pl.BlockSpec((nb, HEADS_PER_STEP * V_ROWS, MOBA_BLOCK), lambda p, i: (0, p, 0))],
        out_specs=pl.BlockSpec((tq, HEADS_PER_STEP * ATTN_HEAD_DIM), lambda p, i: (i, p)),
        out_shape=jax.ShapeDtypeStruct((s, ATTN_WIDTH), BF16),
        scratch_shapes=[pltpu.VMEM((3, HEADS_PER_STEP, KEY_GROUP * MOBA_BLOCK, tq), F32),
                        pltpu.VMEM((HEADS_PER_STEP, AUG, tq), BF16),
                        pltpu.VMEM((HEADS_PER_STEP, 8, tq), F32),
                        pltpu.VMEM((8, LANES), jnp.int32)],
        compiler_params=pltpu.CompilerParams(dimension_semantics=("arbitrary", "arbitrary"),
                                             vmem_limit_bytes=VMEM_LIMIT),
        name="moba",
    )(slope_tab, qt, qt, kmean, knorm2, k_aug, vt_aug)


def _mlstm_kernel(qk_ref, vot_ref, gates_ref, conv_ref, gbias_ref, ngt_ref, y_ref,
                  xbuf, ct_ref, n_ref, m_ref):
    L = MLSTM_CHUNK
    d = MLSTM_HEAD_DIM
    rows = qk_ref.shape[0]
    halo = xbuf.shape[0] - rows
    chunks = range(rows // L)
    heads = range(MLSTM_HEADS)

    @pl.when(pl.program_id(0) == 0)
    def _():
        xbuf[0:halo, :] = jnp.zeros((halo, xbuf.shape[1]), F32)
        ct_ref[...] = jnp.zeros(ct_ref.shape, F32)
        n_ref[...] = jnp.zeros(n_ref.shape, F32)
        m_ref[...] = jnp.zeros(m_ref.shape, F32)

    xbuf[halo:, :] = qk_ref[...]
    w = conv_ref[...]
    y = qk_ref[...] * w[CONV_WIDTH - 1:CONV_WIDTH, :]
    for j in range(CONV_WIDTH - 1):
        off = halo - CONV_WIDTH + 1 + j
        y = y + xbuf[off:off + rows, :] * w[j:j + 1, :]
    xbuf[0:halo, :] = xbuf[rows:rows + halo, :]
    qk = y * _sigmoid(y)

    a = gates_ref[...] + gbias_ref[...]
    lane = lax.broadcasted_iota(jnp.int32, (rows, LANES), 1)
    is_f = jnp.logical_and(lane >= MLSTM_HEADS, lane < 2 * MLSTM_HEADS)
    log_f = jnp.where(is_f, jnp.minimum(a, 0.0) - jnp.log(1.0 + jnp.exp(-jnp.abs(a))), 0.0)
    t_io = lax.broadcasted_iota(jnp.int32, (rows, rows), 0)
    s_io = lax.broadcasted_iota(jnp.int32, (rows, rows), 1)
    same_chunk = (t_io // L) == (s_io // L)
    tri = jnp.where(jnp.logical_and(s_io <= t_io, same_chunk), 1.0, 0.0)
    bcum = jnp.dot(tri, log_f, precision=lax.Precision.HIGHEST,
                   preferred_element_type=F32)
    colm = jnp.where(lane < MLSTM_HEADS, a, bcum)
    rowm = colm.T
    causal_t = (lax.broadcasted_iota(jnp.int32, (L, L), 0) <= lax.broadcasted_iota(jnp.int32, (L, L), 1))

    yield
    pairs = [(cc, hd) for cc in chunks for hd in heads]
    rs = lambda cc: slice(cc * L, (cc + 1) * L)
    fs = lambda hd: slice(hd * d, (hd + 1) * d)
    og_t = {(cc, hd): vot_ref[0, MLSTM_WIDTH + hd * d:MLSTM_WIDTH + (hd + 1) * d, rs(cc)] for cc, hd in pairs}
    qtb = {(cc, hd): qk[rs(cc), fs(hd)].T.astype(BF16) for cc, hd in pairs}
    kb = {(cc, hd): (qk[rs(cc), MLSTM_WIDTH + hd * d:MLSTM_WIDTH + (hd + 1) * d] * (d ** -0.5)).astype(BF16)
          for cc, hd in pairs}
    vt = {(cc, hd): vot_ref[0, fs(hd), rs(cc)] for cc, hd in pairs}
    ic_r = {(cc, hd): rowm[hd:hd + 1, rs(cc)] for cc, hd in pairs}
    bc_r = {(cc, hd): rowm[MLSTM_HEADS + hd:MLSTM_HEADS + hd + 1, rs(cc)] for cc, hd in pairs}
    e_c = {(cc, hd): colm[rs(cc), hd:hd + 1] - colm[rs(cc), MLSTM_HEADS + hd:MLSTM_HEADS + hd + 1]
           for cc, hd in pairs}
    dmat_t = {p: jnp.where(causal_t, bc_r[p] + e_c[p], -jnp.inf) for p in pairs}
    a_row = {p: jnp.max(dmat_t[p], axis=0, keepdims=True) for p in pairs}
    sc_t = {p: jnp.dot(kb[p], qtb[p], preferred_element_type=F32) * jnp.exp(dmat_t[p] - a_row[p]) for p in pairs}
    sv_t = {p: jnp.dot(vt[p].astype(BF16), sc_t[p].astype(BF16), preferred_element_type=F32) for p in pairs}
    ssum = {p: jnp.sum(sc_t[p], axis=0, keepdims=True) for p in pairs}
    b_last = {p: bc_r[p][:, L - 1:L] for p in pairs}
    g_r = {p: b_last[p] - bc_r[p] + ic_r[p] for p in pairs}
    gmax = {p: jnp.max(g_r[p], axis=1, keepdims=True) for p in pairs}
    w_row = {p: jnp.exp(g_r[p] - gmax[p]) for p in pairs}
    upd_t = {p: jnp.dot((vt[p] * w_row[p]).astype(BF16), kb[p], preferred_element_type=F32) for p in pairs}
    ksum = {p: jnp.dot(jnp.broadcast_to(w_row[p], (8, L)).astype(BF16), kb[p],
                       preferred_element_type=F32)[0:1, :] for p in pairs}
    yield
    h_t = {}
    for hd in heads:
        ct_st, n_st, m_prev = ct_ref[hd], n_ref[hd], m_ref[hd:hd + 1, 0:1]
        for cc in chunks:
            p = (cc, hd)
            inter = bc_r[p] + m_prev
            m_row = jnp.maximum(inter, a_row[p])
            r_row = jnp.exp(a_row[p] - m_row)
            w_inter = jnp.exp(inter - m_row)
            qc_t = jnp.dot(ct_st.astype(BF16), qtb[p], preferred_element_type=F32)
            qn = jnp.dot(jnp.broadcast_to(n_st, (8, d)).astype(BF16), qtb[p],
                         preferred_element_type=F32)[0:1, :]
            den = w_inter * qn + r_row * ssum[p]
            scale = 1.0 / jnp.maximum(jnp.abs(den), jnp.exp(-m_row))
            h_t[p] = (w_inter * scale) * qc_t + (r_row * scale) * sv_t[p]
            m_new = jnp.maximum(b_last[p] + m_prev, gmax[p])
            decay = jnp.exp(b_last[p] + m_prev - m_new)
            gain = jnp.exp(gmax[p] - m_new)
            ct_st = decay * ct_st + gain * upd_t[p]
            n_st = decay * n_st + gain * ksum[p]
            m_prev = m_new
        ct_ref[hd], n_ref[hd] = ct_st, n_st
        m_ref[hd:hd + 1, :] = jnp.broadcast_to(m_prev, (1, LANES))
    yield
    for cc, hd in pairs:
        hg = h_t[(cc, hd)] * _sigmoid(og_t[(cc, hd)])
        mu = jnp.mean(hg, axis=0, keepdims=True)
        var = jnp.mean(jnp.square(hg - mu), axis=0, keepdims=True)
        yn = (hg - mu) * lax.rsqrt(var + NORM_EPS) * ngt_ref[fs(hd), :]
        y_ref[rs(cc), fs(hd)] = yn.T.astype(y_ref.dtype)


def _mixer_in_kernel(x_ref, g_ref, wq_ref, wk_ref, wv_ref, wm_ref, wvo_ref, wif_ref, wg_ref, alibi_ref,
                     conv_ref, gbias_ref, ngt_ref,
                     q_ref, k_ref, v_ref, kmean_ref, gg_ref, y_ref,
                     qk_s, vot_s, gates_s, xbuf, ct_ref, n_ref, m_ref):
    t = pl.program_id(0)
    last = pl.num_programs(0) - 2

    @pl.when(t == 0)
    def _():
        qk_s[...] = jnp.zeros(qk_s.shape, F32)
        vot_s[...] = jnp.zeros(vot_s.shape, F32)
        gates_s[...] = jnp.zeros(gates_s.shape, F32)

    mlstm = _mlstm_kernel(qk_s, vot_s, gates_s, conv_ref, gbias_ref, ngt_ref, y_ref, xbuf, ct_ref, n_ref, m_ref)
    proj = _inproj_kernel(x_ref, g_ref, wq_ref, wk_ref, wv_ref, wm_ref, wvo_ref, wif_ref, wg_ref, alibi_ref,
                          q_ref, k_ref, v_ref, kmean_ref, qk_s, vot_s, gates_s, gg_ref, blk=jnp.minimum(t, last))
    for _ in range(3):
        next(mlstm)
        next(proj, None)
    for _ in mlstm:
        pass

    @pl.when(t == 0)
    def _():
        ct_ref[...] = jnp.zeros(ct_ref.shape, F32)
        n_ref[...] = jnp.zeros(n_ref.shape, F32)
        m_ref[...] = jnp.zeros(m_ref.shape, F32)


def _mixer_in(x, g, wq, wk, wv, wm, wvo, wif, wg, alibi, conv, gbias, ngt):
    s = x.shape[0]
    nb = s // MOBA_BLOCK
    cur = lambda t: jnp.minimum(t, nb - 1)
    row = lambda w: pl.BlockSpec((MOBA_BLOCK, w), lambda t: (cur(t), 0))
    tile3 = lambda r: pl.BlockSpec((1, r, MOBA_BLOCK), lambda t: (cur(t), 0, 0))
    whole = pl.BlockSpec(memory_space=pltpu.VMEM)
    return pl.pallas_call(
        _mixer_in_kernel,
        grid=(nb + 1,),
        in_specs=[row(D_MODEL)] + [whole] * 12,
        out_specs=[tile3(ATTN_WIDTH), row(ATTN_AUG_WIDTH), tile3(ATTN_HEADS * V_ROWS),
                   pl.BlockSpec((1, 2, ATTN_AUG_WIDTH), lambda t: (cur(t), 0, 0)),
                   row(2 * D_MODEL),
                   pl.BlockSpec((MOBA_BLOCK, MLSTM_WIDTH), lambda t: (jnp.maximum(t - 1, 0), 0))],
        out_shape=[jax.ShapeDtypeStruct((nb, ATTN_WIDTH, MOBA_BLOCK), F32),
                   jax.ShapeDtypeStruct((s, ATTN_AUG_WIDTH), BF16),
                   jax.ShapeDtypeStruct((nb, ATTN_HEADS * V_ROWS, MOBA_BLOCK), BF16),
                   jax.ShapeDtypeStruct((nb, 2, ATTN_AUG_WIDTH), F32),
                   jax.ShapeDtypeStruct((s, 2 * D_MODEL), F32),
                   jax.ShapeDtypeStruct((s, MLSTM_WIDTH), BF16)],
        scratch_shapes=[pltpu.VMEM((MOBA_BLOCK, 2 * MLSTM_WIDTH), F32),
                        pltpu.VMEM((1, 2 * MLSTM_WIDTH, MOBA_BLOCK), F32),
                        pltpu.VMEM((MOBA_BLOCK, LANES), F32),
                        pltpu.VMEM((MOBA_BLOCK + CONV_HALO, 2 * MLSTM_WIDTH), F32),
                        pltpu.VMEM((MLSTM_HEADS, MLSTM_HEAD_DIM, MLSTM_HEAD_DIM), F32),
                        pltpu.VMEM((MLSTM_HEADS, 1, MLSTM_HEAD_DIM), F32),
                        pltpu.VMEM((8, LANES), F32)],
        compiler_params=pltpu.CompilerParams(dimension_semantics=("arbitrary",),
                                             vmem_limit_bytes=VMEM_LIMIT),
        name="mixer_in",
    )(x, g, wq, wk, wv, wm, wvo, wif, wg, alibi, conv, gbias, ngt)


def _out_ffn_kernel(x_ref, ya_ref, ym_ref, gg_ref, woa_ref, wom_ref, wo_ref, fg_ref,
                    wgt_ref, wup_ref, wdn_ref, fin_ref, o_ref, *, final_norm):
    a = jnp.dot(ya_ref[...], woa_ref[...], preferred_element_type=F32)
    b = jnp.dot(ym_ref[...], wom_ref[...], preferred_element_type=F32)
    merged = _sigmoid(gg_ref[:, :D_MODEL]) * a + _sigmoid(gg_ref[:, D_MODEL:]) * b
    x1 = x_ref[...] + jnp.dot(merged.astype(BF16), wo_ref[...], preferred_element_type=F32)
    h2 = _rms(x1, fg_ref[...]).astype(BF16)
    acc = jnp.zeros(x1.shape, F32)
    for c in range(0, FFN_HIDDEN, FFN_CHUNK):
        gt = jnp.dot(h2, wgt_ref[:, c:c + FFN_CHUNK], preferred_element_type=F32)
        up = jnp.dot(h2, wup_ref[:, c:c + FFN_CHUNK], preferred_element_type=F32)
        act = (gt * _sigmoid(gt) * up).astype(BF16)
        acc = acc + jnp.dot(act, wdn_ref[c:c + FFN_CHUNK, :], preferred_element_type=F32)
    x2 = x1 + acc
    o_ref[...] = _rms(x2, fin_ref[...]) if final_norm else x2


def _out_ffn(x, ya, ym, gg, woa, wom, wo, fg, wgt, wup, wdn, fin, final_norm):
    s = x.shape[0]
    row = lambda w: pl.BlockSpec((OUT_TILE, w), lambda i: (i, 0))
    whole = pl.BlockSpec(memory_space=pltpu.VMEM)
    return pl.pallas_call(
        functools.partial(_out_ffn_kernel, final_norm=final_norm),
        grid=(s // OUT_TILE,),
        in_specs=[row(D_MODEL), row(ATTN_WIDTH), row(MLSTM_WIDTH), row(2 * D_MODEL)] + [whole] * 8,
        out_specs=row(D_MODEL),
        out_shape=jax.ShapeDtypeStruct((s, D_MODEL), F32),
        compiler_params=pltpu.CompilerParams(dimension_semantics=("arbitrary",),
                                             vmem_limit_bytes=VMEM_LIMIT),
        name="out_ffn",
    )(x, ya, ym, gg, woa, wom, wo, fg, wgt, wup, wdn, fin)


def _alibi_slopes():
    return np.exp2(-8.0 * np.arange(1, ATTN_HEADS + 1, dtype=np.float64) / ATTN_HEADS).astype(np.float32)


def _pad_heads(w, width):
    d = w.shape[0]
    w = w.reshape(d, ATTN_HEADS, ATTN_HEAD_DIM)
    w = jnp.pad(w, ((0, 0), (0, 0), (0, width - ATTN_HEAD_DIM)))
    return w.reshape(d, ATTN_HEADS * width)


def kernel(x, mix_norm_g, w_in, conv_qk, b_igate, b_fgate, mlstm_norm_g, w_out_attn, w_out_mlstm,
           w_o, ffn_norm_g, w_ffn_gate, w_ffn_up, w_ffn_down, final_norm_g):
    batch, seq, _ = x.shape
    depth = w_in.shape[0]
    assert seq % OUT_TILE == 0 and seq // MOBA_BLOCK <= N_BLOCK_COLS
    assert seq % (KEY_GROUP * MOBA_BLOCK) == 0
    slopes = _alibi_slopes()
    alibi = np.zeros((1, ATTN_AUG_WIDTH), np.float32)
    alibi[0, AUG - 1::AUG] = slopes
    alibi = jnp.asarray(alibi)
    slope_tab = jnp.asarray(np.broadcast_to(
        slopes.reshape(ATTN_HEADS // HEADS_PER_STEP, HEADS_PER_STEP, 1),
        (ATTN_HEADS // HEADS_PER_STEP, HEADS_PER_STEP, LANES)).copy())
    a0, a1, a2 = ATTN_WIDTH, 2 * ATTN_WIDTH, 3 * ATTN_WIDTH
    m1 = a2 + 4 * MLSTM_WIDTH
    g0 = m1 + 2 * MLSTM_HEADS

    outs = []
    for bi in range(batch):
        xb = x[bi]
        for layer in range(depth):
            w = w_in[layer]
            wq = w[:, :a0].T.astype(BF16)
            wk = w[:, a0:a1].astype(BF16)
            wv = _pad_heads(w[:, a1:a2], V_ROWS).T.astype(BF16)
            wm = w[:, a2:a2 + 2 * MLSTM_WIDTH].astype(BF16)
            wvo = w[:, a2 + 2 * MLSTM_WIDTH:m1].T.astype(BF16)
            wif = jnp.pad(w[:, m1:g0], ((0, 0), (0, LANES - 2 * MLSTM_HEADS))).astype(BF16)
            wg = w[:, g0:].astype(BF16)
            gbias = jnp.pad(jnp.concatenate([b_igate[layer], b_fgate[layer]])[None, :],
                            ((0, 0), (0, LANES - 2 * MLSTM_HEADS))).astype(F32)
            ngt = jnp.broadcast_to(mlstm_norm_g[layer][:, None], (MLSTM_WIDTH, MLSTM_CHUNK)).astype(F32)
            q, k_aug, v_aug, kstat, gg, ym = _mixer_in(
                xb, mix_norm_g[layer][None, :], wq, wk, wv, wm, wvo, wif, wg, alibi,
                conv_qk[layer], gbias, ngt)
            pad_blocks = ((0, N_BLOCK_COLS - kstat.shape[0]), (0, 0))
            ya = _moba(slope_tab, q, jnp.pad(kstat[:, 0, :], pad_blocks), jnp.pad(kstat[:, 1, :], pad_blocks),
                       k_aug, v_aug)
            xb = _out_ffn(xb, ya, ym, gg,
                          w_out_attn[layer].astype(BF16), w_out_mlstm[layer].astype(BF16),
                          w_o[layer].astype(BF16), ffn_norm_g[layer][None, :],
                          w_ffn_gate[layer].astype(BF16), w_ffn_up[layer].astype(BF16),
                          w_ffn_down[layer].astype(BF16), final_norm_g[None, :],
                          final_norm=(layer == depth - 1))
        outs.append(xb)
    return outs[0][None] if batch == 1 else jnp.stack(outs, axis=0)
```

```python
import functools

import numpy as np
import jax
import jax.numpy as jnp
from jax import lax
from jax.experimental import pallas as pl
from jax.experimental.pallas import tpu as pltpu

D_MODEL = 1024
ATTN_HEADS = 8
ATTN_HEAD_DIM = 64
ATTN_WIDTH = ATTN_HEADS * ATTN_HEAD_DIM
MOBA_BLOCK = 256
MOBA_TOP_K = 3
MLSTM_HEADS = 4
MLSTM_HEAD_DIM = 128
MLSTM_WIDTH = MLSTM_HEADS * MLSTM_HEAD_DIM
MLSTM_CHUNK = 128
CONV_WIDTH = 4
FFN_HIDDEN = 2816
NORM_EPS = 1e-6

LANES = 128
AUG = 2 * ATTN_HEAD_DIM
ATTN_AUG_WIDTH = ATTN_HEADS * AUG
V_ROWS = 80
N_BLOCK_COLS = AUG - ATTN_HEAD_DIM
HEADS_PER_STEP = 2
GATE_ROWS = 16
CONV_HALO = 8
Q_TILES = 2
KEY_GROUP = 2
NEGLIGIBLE_LOG = 50.0
NORM_SLACK = 1.02
NEG_BIAS = -1e9
FFN_CHUNK = 256
OUT_TILE = 512
VMEM_LIMIT = 56 * 1024 * 1024

F32 = jnp.float32
BF16 = jnp.bfloat16
NT_DIMS = (((1,), (1,)), ((), ()))


def _rms(x, g):
    return x * lax.rsqrt(jnp.mean(x * x, axis=-1, keepdims=True) + NORM_EPS) * g


def _sigmoid(x):
    return 1.0 / (1.0 + jnp.exp(-x))


def _inproj_kernel(x_ref, g_ref, wq_ref, wk_ref, wv_ref, wm_ref, wvo_ref, wif_ref, wg_ref, alibi_ref,
                   q_ref, k_ref, v_ref, kmean_ref, qkm_ref, vom_ref, gates_ref, gg_ref, *, blk):
    hb = _rms(x_ref[...], g_ref[...]).astype(BF16)
    q_ref[0] = lax.dot_general(wq_ref[...], hb, NT_DIMS, preferred_element_type=F32)
    kc = jnp.dot(hb, wk_ref[...], preferred_element_type=F32)
    gap = jnp.zeros((MOBA_BLOCK, AUG - ATTN_HEAD_DIM), F32)
    kf = jnp.concatenate([piece for h in range(ATTN_HEADS)
                          for piece in (kc[:, h * ATTN_HEAD_DIM:(h + 1) * ATTN_HEAD_DIM], gap)], axis=1)
    ksq = kf * kf
    norm2 = [jnp.max(jnp.sum(ksq[:, h * AUG:(h + 1) * AUG], axis=1, keepdims=True), axis=0, keepdims=True)
             for h in range(ATTN_HEADS)]
    kmean_ref[0] = jnp.concatenate(
        [jnp.mean(kf, axis=0, keepdims=True),
         jnp.concatenate([jnp.broadcast_to(n2, (1, AUG)) for n2 in norm2], axis=1)], axis=0)
    lane = lax.broadcasted_iota(jnp.int32, (1, ATTN_AUG_WIDTH), 1) & (AUG - 1)
    row = lax.broadcasted_iota(jnp.int32, (MOBA_BLOCK, 1), 0).astype(F32)
    onehot = jnp.where(lane - ATTN_HEAD_DIM == blk, 1.0, 0.0)
    onehot = jnp.where(lane == AUG - 1, 0.0, onehot)
    k_ref[...] = (kf + onehot + alibi_ref[...] * row).astype(BF16)
    yield
    gg_ref[:, :D_MODEL] = jnp.dot(hb, wg_ref[:, :D_MODEL], preferred_element_type=F32)
    yield
    vt = lax.dot_general(wv_ref[...], hb, NT_DIMS, preferred_element_type=F32)
    feat = lax.rem(lax.broadcasted_iota(jnp.int32, (ATTN_HEADS * V_ROWS, 1), 0), V_ROWS)
    v_ref[0] = (vt + jnp.where(feat == ATTN_HEAD_DIM, 1.0, 0.0)).astype(BF16)
    gates_ref[...] = lax.dot_general(wif_ref[...], hb, NT_DIMS, preferred_element_type=F32)
    yield
    qkm_ref[...] = jnp.dot(hb, wm_ref[...], preferred_element_type=F32)
    yield
    vom_ref[0] = lax.dot_general(wvo_ref[...], hb, NT_DIMS, preferred_element_type=F32)
    yield
    gg_ref[:, D_MODEL:] = jnp.dot(hb, wg_ref[:, D_MODEL:], preferred_element_type=F32)


def _col_max(st):
    rows, n = st.shape
    fan = 8
    while rows > fan * 8 and rows % fan == 0:
        st = jnp.max(st.reshape(fan, rows // fan, n), axis=0)
        rows //= fan
    return jnp.max(st, axis=0, keepdims=True)


def _moba_kernel(slope_ref, q_ref, qn_ref, kmean_ref, kn2_ref, k_ref, vt_ref, o_ref,
                 s_ref, qa_ref, cm_ref, reach_ref):
    step_id = pl.program_id(1)
    n_steps = pl.num_programs(1)
    i = step_id * Q_TILES
    tq = Q_TILES * MOBA_BLOCK
    nb = N_BLOCK_COLS
    group_rows = KEY_GROUP * MOBA_BLOCK
    assert KEY_GROUP % Q_TILES == 0
    heads = range(HEADS_PER_STEP)
    head_lanes = [slice(hh * AUG, (hh + 1) * AUG) for hh in heads]
    head_rows = [slice(hh * V_ROWS, (hh + 1) * V_ROWS) for hh in heads]
    own_slot = 2

    def prepare(qsrc_ref, tile0):
        blk = lax.broadcasted_iota(jnp.int32, (nb, tq), 0).astype(F32)
        i_f = (tile0 + lax.broadcasted_iota(jnp.int32, (1, tq), 1) // MOBA_BLOCK).astype(F32)
        reach = jnp.zeros((1, 1), F32)
        qk_cap, slopes = [], []
        for hh in heads:
            slope = slope_ref[0, hh:hh + 1, 0:1]
            qh = jnp.concatenate([qsrc_ref[t, hh * ATTN_HEAD_DIM:(hh + 1) * ATTN_HEAD_DIM, :]
                                  for t in range(Q_TILES)], axis=1)
            km = kmean_ref[:, hh * AUG:hh * AUG + ATTN_HEAD_DIM]
            gate = jnp.dot(km, qh, precision=lax.Precision.HIGHEST, preferred_element_type=F32)
            valid = blk < i_f
            g = jnp.where(valid, gate, -jnp.inf)
            sel = jnp.zeros((nb, tq), jnp.bool_)
            for _ in range(MOBA_TOP_K):
                mx = jnp.max(g, axis=0, keepdims=True)
                idx = jnp.min(jnp.where(g == mx, blk, float(nb)), axis=0, keepdims=True)
                pick = blk == idx
                sel = jnp.logical_or(sel, pick)
                g = jnp.where(pick, -jnp.inf, g)
            bias = jnp.where(sel, (blk - i_f) * (slope * float(MOBA_BLOCK)), NEG_BIAS)
            bias = jnp.where(valid, bias, 0.0)
            bias = jnp.where(blk == float(nb - 1), 1.0, bias)
            qa_ref[hh] = jnp.concatenate([qh * (ATTN_HEAD_DIM ** -0.5), bias], axis=0).astype(BF16)
            k_norm = jnp.sqrt(jnp.max(kn2_ref[:, hh * AUG:hh * AUG + 1], axis=0, keepdims=True))
            qk_cap.append(NORM_SLACK * (ATTN_HEAD_DIM ** -0.5) * k_norm
                          * jnp.sqrt(jnp.sum(qh * qh, axis=0, keepdims=True)))
            slopes.append(slope)
        first = (tile0 // KEY_GROUP) * KEY_GROUP
        hidden = (lax.broadcasted_iota(jnp.int32, (group_rows, tq), 0)
                  - lax.broadcasted_iota(jnp.int32, (group_rows, tq), 1)) > (tile0 - first) * MOBA_BLOCK
        start = pl.multiple_of(first * MOBA_BLOCK, group_rows)
        for hh in heads:
            st = jnp.dot(k_ref[pl.ds(start, group_rows), head_lanes[hh]], qa_ref[hh], preferred_element_type=F32)
            st = jnp.where(hidden, -jnp.inf, st)
            s_ref[own_slot, hh] = st
            own_max = _col_max(st)
            cm_ref[hh] = jnp.broadcast_to(own_max, (8, tq))
            spread = jnp.max(qk_cap[hh] - own_max, axis=1, keepdims=True)
            reach = jnp.maximum(reach, (NEGLIGIBLE_LOG + spread) / (slopes[hh] * float(MOBA_BLOCK)) + 1.0)
        reach = jnp.ceil(reach)
        reach = jnp.where(reach < float(nb), reach, float(nb))
        reach_ref[...] = jnp.broadcast_to(reach, reach_ref.shape).astype(jnp.int32)

    def score(first, slot):
        start = pl.multiple_of(first * MOBA_BLOCK, group_rows)
        col_max = []
        for hh in heads:
            st = jnp.dot(k_ref[pl.ds(start, group_rows), head_lanes[hh]], qa_ref[hh], preferred_element_type=F32)
            s_ref[slot, hh] = st
            col_max.append(_col_max(st))
        return col_max

    def consume(first, slot, m_prev, m_cur, accs):
        ps = [jnp.exp(s_ref[slot, hh] - m_cur[hh]).astype(BF16) for hh in heads]
        accs = [jnp.exp(m_prev[hh] - m_cur[hh]) * accs[hh] for hh in heads]
        for b in range(KEY_GROUP):
            for hh in heads:
                accs[hh] = accs[hh] + jnp.dot(vt_ref[first + b, head_rows[hh], :],
                                              ps[hh][b * MOBA_BLOCK:(b + 1) * MOBA_BLOCK, :],
                                              preferred_element_type=F32)
        return accs

    def step(first, slot, src, carry):
        pend, m_prev, m_cur, accs = carry
        cm = score(first, slot)
        accs = consume(pend, src, m_prev, m_cur, accs)
        return (jnp.asarray(first, jnp.int32), m_cur,
                [jnp.maximum(m_cur[hh], cm[hh]) for hh in heads], accs)

    def pair(first_group, carry, src=0):
        g = first_group * KEY_GROUP
        return step(g + KEY_GROUP, 0, 1, step(g, 1, src, carry))

    @pl.when(step_id == 0)
    def _():
        prepare(q_ref, i)

    own_first = (i // KEY_GROUP) * KEY_GROUP
    first_block = jnp.maximum(i - jnp.max(reach_ref[...]) + 1, 0)
    first_group = first_block // KEY_GROUP
    n_past = i // KEY_GROUP - first_group
    cm0 = [cm_ref[hh, 0:1, :] for hh in heads]
    init = (jnp.asarray(own_first, jnp.int32), cm0, cm0, [jnp.zeros((V_ROWS, tq), F32) for _ in heads])

    def start_none(c):
        for hh in heads:
            s_ref[0, hh] = s_ref[own_slot, hh]
        return c

    def start_odd(c):
        return step(first_group * KEY_GROUP, 0, own_slot, c)

    def start_even(c):
        return pair(first_group, c, src=own_slot)

    odd = n_past % 2
    carry = lax.cond(n_past == 0, start_none,
                     lambda c: lax.cond(odd == 1, start_odd, start_even, c), init)
    done = first_group + jnp.where(n_past == 0, 0, 2 - odd)
    n_pairs = (i // KEY_GROUP - done) // 2
    odd_pair = n_pairs % 2
    carry = lax.cond(odd_pair == 1, functools.partial(pair, done), lambda c: c, carry)
    done = done + 2 * odd_pair

    def quad(u, carry):
        return pair(done + 4 * u + 2, pair(done + 4 * u, carry))

    pend, m_prev, m_cur, accs = lax.fori_loop(0, n_pairs // 2, quad, carry)
    accs = consume(pend, 0, m_prev, m_cur, accs)
    prepare(qn_ref, jnp.minimum(step_id + 1, n_steps - 1) * Q_TILES)
    outs = []
    for hh in heads:
        ot = accs[hh][:ATTN_HEAD_DIM, :] / accs[hh][ATTN_HEAD_DIM:ATTN_HEAD_DIM + 1, :]
        outs.append(ot.T)
    o_ref[...] = jnp.concatenate(outs, axis=1).astype(o_ref.dtype)


def _moba(slope_tab, qt, kmean, knorm2, k_aug, vt_aug):
    s = k_aug.shape[0]
    nb = s // MOBA_BLOCK
    n_pairs = ATTN_HEADS // HEADS_PER_STEP
    n_steps = nb // Q_TILES
    tq = Q_TILES * MOBA_BLOCK
    q_block = (Q_TILES, HEADS_PER_STEP * ATTN_HEAD_DIM, MOBA_BLOCK)
    return pl.pallas_call(
        _moba_kernel,
        grid=(n_pairs, n_steps),
        in_specs=[pl.BlockSpec((1, HEADS_PER_STEP, LANES), lambda p, i: (p, 0, 0)),
                  pl.BlockSpec(q_block, lambda p, i: (i, p, 0)),
                  pl.BlockSpec(q_block, lambda p, i: (jnp.minimum(i + 1, n_steps - 1), p, 0)),
                  pl.BlockSpec((N_BLOCK_COLS, HEADS_PER_STEP * AUG), lambda p, i: (0, p)),
                  pl.BlockSpec((N_BLOCK_COLS, HEADS_PER_STEP * AUG), lambda p, i: (0, p)),
                  pl.BlockSpec((s, HEADS_PER_STEP * AUG), lambda p, i: (0, p)),
                  pl.BlockSpec((nb, HEADS_PER_STEP * V_ROWS, MOBA_BLOCK), lambda p, i: (0, p, 0))],
        out_specs=pl.BlockSpec((tq, HEADS_PER_STEP * ATTN_HEAD_DIM), lambda p, i: (i, p)),
        out_shape=jax.ShapeDtypeStruct((s, ATTN_WIDTH), BF16),
        scratch_shapes=[pltpu.VMEM((3, HEADS_PER_STEP, KEY_GROUP * MOBA_BLOCK, tq), F32),
                        pltpu.VMEM((HEADS_PER_STEP, AUG, tq), BF16),
                        pltpu.VMEM((HEADS_PER_STEP, 8, tq), F32),
                        pltpu.VMEM((8, LANES), jnp.int32)],
        compiler_params=pltpu.CompilerParams(dimension_semantics=("arbitrary", "arbitrary"),
                                             vmem_limit_bytes=VMEM_LIMIT),
        name="moba",
    )(slope_tab, qt, qt, kmean, knorm2, k_aug, vt_aug)


def _mlstm_kernel(qk_ref, vot_ref, gates_ref, conv_ref, gbias_ref, ngt_ref, y_ref,
                  xbuf, ct_ref, n_ref, m_ref):
    L = MLSTM_CHUNK
    d = MLSTM_HEAD_DIM
    rows = qk_ref.shape[0]
    halo = xbuf.shape[0] - rows
    chunks = range(rows // L)
    heads = range(MLSTM_HEADS)

    @pl.when(pl.program_id(0) == 0)
    def _():
        xbuf[0:halo, :] = jnp.zeros((halo, xbuf.shape[1]), F32)
        ct_ref[...] = jnp.zeros(ct_ref.shape, F32)
        n_ref[...] = jnp.zeros(n_ref.shape, F32)
        m_ref[...] = jnp.zeros(m_ref.shape, F32)

    xbuf[halo:, :] = qk_ref[...]
    w = conv_ref[...]
    y = qk_ref[...] * w[CONV_WIDTH - 1:CONV_WIDTH, :]
    for j in range(CONV_WIDTH - 1):
        off = halo - CONV_WIDTH + 1 + j
        y = y + xbuf[off:off + rows, :] * w[j:j + 1, :]
    xbuf[0:halo, :] = xbuf[rows:rows + halo, :]
    qk = y * _sigmoid(y)

    a = gates_ref[...] + gbias_ref[:, 0:1]
    gate_row = lax.broadcasted_iota(jnp.int32, a.shape, 0)
    is_f = jnp.logical_and(gate_row >= MLSTM_HEADS, gate_row < 2 * MLSTM_HEADS)
    log_f = jnp.where(is_f, jnp.minimum(a, 0.0) - jnp.log(1.0 + jnp.exp(-jnp.abs(a))), 0.0)
    s_io = lax.broadcasted_iota(jnp.int32, (rows, rows), 0)
    t_io = lax.broadcasted_iota(jnp.int32, (rows, rows), 1)
    same_chunk = (t_io // L) == (s_io // L)
    tri = jnp.where(jnp.logical_and(s_io <= t_io, same_chunk), 1.0, 0.0)
    bcum = jnp.dot(log_f, tri, precision=lax.Precision.HIGHEST,
                   preferred_element_type=F32)
    rowm = jnp.where(gate_row < MLSTM_HEADS, a, bcum)
    e_rows = rowm[0:MLSTM_HEADS, :] - rowm[MLSTM_HEADS:2 * MLSTM_HEADS, :]
    colm = jnp.concatenate([e_rows, jnp.zeros((LANES - MLSTM_HEADS, rows), F32)], axis=0).T
    causal_t = (lax.broadcasted_iota(jnp.int32, (L, L), 0) <= lax.broadcasted_iota(jnp.int32, (L, L), 1))

    yield
    pairs = [(cc, hd) for cc in chunks for hd in heads]
    rs = lambda cc: slice(cc * L, (cc + 1) * L)
    fs = lambda hd: slice(hd * d, (hd + 1) * d)
    og_t = {(cc, hd): vot_ref[0, MLSTM_WIDTH + hd * d:MLSTM_WIDTH + (hd + 1) * d, rs(cc)] for cc, hd in pairs}
    vt = {(cc, hd): vot_ref[0, fs(hd), rs(cc)] for cc, hd in pairs}
    qtb, kb, bc_r, a_row, sv_t, ssum, b_last, gmax, upd_t, ksum = ({} for _ in range(10))
    for cc in chunks:
        group = [(cc, hd) for hd in heads]
        qtb.update({(cc, hd): qk[rs(cc), fs(hd)].T.astype(BF16) for cc, hd in group})
        kb.update({(cc, hd): (qk[rs(cc), MLSTM_WIDTH + hd * d:MLSTM_WIDTH + (hd + 1) * d]
                              * (d ** -0.5)).astype(BF16) for cc, hd in group})
        ic_r = {(cc, hd): rowm[hd:hd + 1, rs(cc)] for cc, hd in group}
        bc_r.update({(cc, hd): rowm[MLSTM_HEADS + hd:MLSTM_HEADS + hd + 1, rs(cc)] for cc, hd in group})
        e_c = {(cc, hd): colm[rs(cc), hd:hd + 1] for cc, hd in group}
        dmat_t = {p: jnp.where(causal_t, bc_r[p] + e_c[p], -jnp.inf) for p in group}
        a_row.update({p: jnp.max(dmat_t[p], axis=0, keepdims=True) for p in group})
        sc_t = {p: jnp.dot(kb[p], qtb[p], preferred_element_type=F32) * jnp.exp(dmat_t[p] - a_row[p])
                for p in group}
        sv_t.update({p: jnp.dot(vt[p].astype(BF16), sc_t[p].astype(BF16), preferred_element_type=F32)
                     for p in group})
        ssum.update({p: jnp.sum(sc_t[p], axis=0, keepdims=True) for p in group})
        b_last.update({p: bc_r[p][:, L - 1:L] for p in group})
        g_r = {p: b_last[p] - bc_r[p] + ic_r[p] for p in group}
        gmax.update({p: jnp.max(g_r[p], axis=1, keepdims=True) for p in group})
        w_row = {p: jnp.exp(g_r[p] - gmax[p]) for p in group}
        upd_t.update({p: jnp.dot((vt[p] * w_row[p]).astype(BF16), kb[p], preferred_element_type=F32)
                      for p in group})
        ksum.update({p: jnp.dot(jnp.broadcast_to(w_row[p], (8, L)).astype(BF16), kb[p],
                                preferred_element_type=F32)[0:1, :] for p in group})
        yield
    h_t = {}
    for hd in heads:
        ct_st, n_st, m_prev = ct_ref[hd], n_ref[hd], m_ref[hd:hd + 1, 0:1]
        for cc in chunks:
            p = (cc, hd)
            inter = bc_r[p] + m_prev
            m_row = jnp.maximum(inter, a_row[p])
            r_row = jnp.exp(a_row[p] - m_row)
            w_inter = jnp.exp(inter - m_row)
            qc_t = jnp.dot(ct_st.astype(BF16), qtb[p], preferred_element_type=F32)
            qn = jnp.dot(jnp.broadcast_to(n_st, (8, d)).astype(BF16), qtb[p],
                         preferred_element_type=F32)[0:1, :]
            den = w_inter * qn + r_row * ssum[p]
            scale = 1.0 / jnp.maximum(jnp.abs(den), jnp.exp(-m_row))
            h_t[p] = (w_inter * scale) * qc_t + (r_row * scale) * sv_t[p]
            m_new = jnp.maximum(b_last[p] + m_prev, gmax[p])
            decay = jnp.exp(b_last[p] + m_prev - m_new)
            gain = jnp.exp(gmax[p] - m_new)
            ct_st = decay * ct_st + gain * upd_t[p]
            n_st = decay * n_st + gain * ksum[p]
            m_prev = m_new
        ct_ref[hd], n_ref[hd] = ct_st, n_st
        m_ref[hd:hd + 1, :] = jnp.broadcast_to(m_prev, (1, LANES))
    for cc, hd in pairs:
        if hd == 0:
            yield
        hg = h_t[(cc, hd)] * _sigmoid(og_t[(cc, hd)])
        mu = jnp.mean(hg, axis=0, keepdims=True)
        var = jnp.mean(jnp.square(hg - mu), axis=0, keepdims=True)
        yn = (hg - mu) * lax.rsqrt(var + NORM_EPS) * ngt_ref[fs(hd), :]
        y_ref[rs(cc), fs(hd)] = yn.T.astype(y_ref.dtype)


def _mixer_in_kernel(x_ref, g_ref, wq_ref, wk_ref, wv_ref, wm_ref, wvo_ref, wif_ref, wg_ref, alibi_ref,
                     conv_ref, gbias_ref, ngt_ref,
                     q_ref, k_ref, v_ref, kmean_ref, gg_ref, y_ref,
                     qk_s, vot_s, gates_s, xbuf, ct_ref, n_ref, m_ref):
    t = pl.program_id(0)
    last = pl.num_programs(0) - 2

    @pl.when(t == 0)
    def _():
        qk_s[...] = jnp.zeros(qk_s.shape, F32)
        vot_s[...] = jnp.zeros(vot_s.shape, F32)
        gates_s[...] = jnp.zeros(gates_s.shape, F32)

    mlstm = _mlstm_kernel(qk_s, vot_s, gates_s, conv_ref, gbias_ref, ngt_ref, y_ref, xbuf, ct_ref, n_ref, m_ref)
    proj = _inproj_kernel(x_ref, g_ref, wq_ref, wk_ref, wv_ref, wm_ref, wvo_ref, wif_ref, wg_ref, alibi_ref,
                          q_ref, k_ref, v_ref, kmean_ref, qk_s, vot_s, gates_s, gg_ref, blk=jnp.minimum(t, last))
    running = [mlstm, proj]
    while running:
        for body in list(running):
            if next(body, StopIteration) is StopIteration:
                running.remove(body)

    @pl.when(t == 0)
    def _():
        ct_ref[...] = jnp.zeros(ct_ref.shape, F32)
        n_ref[...] = jnp.zeros(n_ref.shape, F32)
        m_ref[...] = jnp.zeros(m_ref.shape, F32)


def _mixer_in(x, g, wq, wk, wv, wm, wvo, wif, wg, alibi, conv, gbias, ngt):
    s = x.shape[0]
    nb = s // MOBA_BLOCK
    cur = lambda t: jnp.minimum(t, nb - 1)
    row = lambda w: pl.BlockSpec((MOBA_BLOCK, w), lambda t: (cur(t), 0))
    tile3 = lambda r: pl.BlockSpec((1, r, MOBA_BLOCK), lambda t: (cur(t), 0, 0))
    whole = pl.BlockSpec(memory_space=pltpu.VMEM)
    return pl.pallas_call(
        _mixer_in_kernel,
        grid=(nb + 1,),
        in_specs=[row(D_MODEL)] + [whole] * 12,
        out_specs=[tile3(ATTN_WIDTH), row(ATTN_AUG_WIDTH), tile3(ATTN_HEADS * V_ROWS),
                   pl.BlockSpec((1, 2, ATTN_AUG_WIDTH), lambda t: (cur(t), 0, 0)),
                   row(2 * D_MODEL),
                   pl.BlockSpec((MOBA_BLOCK, MLSTM_WIDTH), lambda t: (jnp.maximum(t - 1, 0), 0))],
        out_shape=[jax.ShapeDtypeStruct((nb, ATTN_WIDTH, MOBA_BLOCK), F32),
                   jax.ShapeDtypeStruct((s, ATTN_AUG_WIDTH), BF16),
                   jax.ShapeDtypeStruct((nb, ATTN_HEADS * V_ROWS, MOBA_BLOCK), BF16),
                   jax.ShapeDtypeStruct((nb, 2, ATTN_AUG_WIDTH), F32),
                   jax.ShapeDtypeStruct((s, 2 * D_MODEL), F32),
                   jax.ShapeDtypeStruct((s, MLSTM_WIDTH), BF16)],
        scratch_shapes=[pltpu.VMEM((MOBA_BLOCK, 2 * MLSTM_WIDTH), F32),
                        pltpu.VMEM((1, 2 * MLSTM_WIDTH, MOBA_BLOCK), F32),
                        pltpu.VMEM((GATE_ROWS, MOBA_BLOCK), F32),
                        pltpu.VMEM((MOBA_BLOCK + CONV_HALO, 2 * MLSTM_WIDTH), F32),
                        pltpu.VMEM((MLSTM_HEADS, MLSTM_HEAD_DIM, MLSTM_HEAD_DIM), F32),
                        pltpu.VMEM((MLSTM_HEADS, 1, MLSTM_HEAD_DIM), F32),
                        pltpu.VMEM((8, LANES), F32)],
        compiler_params=pltpu.CompilerParams(dimension_semantics=("arbitrary",),
                                             vmem_limit_bytes=VMEM_LIMIT),
        name="mixer_in",
    )(x, g, wq, wk, wv, wm, wvo, wif, wg, alibi, conv, gbias, ngt)


def _out_ffn_kernel(x_ref, ya_ref, ym_ref, gg_ref, woa_ref, wom_ref, wo_ref, fg_ref,
                    wgt_ref, wup_ref, wdn_ref, fin_ref, o_ref, *, final_norm):
    a = jnp.dot(ya_ref[...], woa_ref[...], preferred_element_type=F32)
    b = jnp.dot(ym_ref[...], wom_ref[...], preferred_element_type=F32)
    merged = _sigmoid(gg_ref[:, :D_MODEL]) * a + _sigmoid(gg_ref[:, D_MODEL:]) * b
    x1 = x_ref[...] + jnp.dot(merged.astype(BF16), wo_ref[...], preferred_element_type=F32)
    h2 = _rms(x1, fg_ref[...]).astype(BF16)
    acc = jnp.zeros(x1.shape, F32)
    for c in range(0, FFN_HIDDEN, FFN_CHUNK):
        gt = jnp.dot(h2, wgt_ref[:, c:c + FFN_CHUNK], preferred_element_type=F32)
        up = jnp.dot(h2, wup_ref[:, c:c + FFN_CHUNK], preferred_element_type=F32)
        act = (gt * _sigmoid(gt) * up).astype(BF16)
        acc = acc + jnp.dot(act, wdn_ref[c:c + FFN_CHUNK, :], preferred_element_type=F32)
    x2 = x1 + acc
    o_ref[...] = _rms(x2, fin_ref[...]) if final_norm else x2


def _out_ffn(x, ya, ym, gg, woa, wom, wo, fg, wgt, wup, wdn, fin, final_norm):
    s = x.shape[0]
    row = lambda w: pl.BlockSpec((OUT_TILE, w), lambda i: (i, 0))
    whole = pl.BlockSpec(memory_space=pltpu.VMEM)
    return pl.pallas_call(
        functools.partial(_out_ffn_kernel, final_norm=final_norm),
        grid=(s // OUT_TILE,),
        in_specs=[row(D_MODEL), row(ATTN_WIDTH), row(MLSTM_WIDTH), row(2 * D_MODEL)] + [whole] * 8,
        out_specs=row(D_MODEL),
        out_shape=jax.ShapeDtypeStruct((s, D_MODEL), F32),
        compiler_params=pltpu.CompilerParams(dimension_semantics=("arbitrary",),
                                             vmem_limit_bytes=VMEM_LIMIT),
        name="out_ffn",
    )(x, ya, ym, gg, woa, wom, wo, fg, wgt, wup, wdn, fin)


def _alibi_slopes():
    return np.exp2(-8.0 * np.arange(1, ATTN_HEADS + 1, dtype=np.float64) / ATTN_HEADS).astype(np.float32)


def _pad_heads(w, width):
    d = w.shape[0]
    w = w.reshape(d, ATTN_HEADS, ATTN_HEAD_DIM)
    w = jnp.pad(w, ((0, 0), (0, 0), (0, width - ATTN_HEAD_DIM)))
    return w.reshape(d, ATTN_HEADS * width)


def kernel(x, mix_norm_g, w_in, conv_qk, b_igate, b_fgate, mlstm_norm_g, w_out_attn, w_out_mlstm,
           w_o, ffn_norm_g, w_ffn_gate, w_ffn_up, w_ffn_down, final_norm_g):
    batch, seq, _ = x.shape
    depth = w_in.shape[0]
    assert seq % OUT_TILE == 0 and seq // MOBA_BLOCK <= N_BLOCK_COLS
    assert seq % (KEY_GROUP * MOBA_BLOCK) == 0
    slopes = _alibi_slopes()
    alibi = np.zeros((1, ATTN_AUG_WIDTH), np.float32)
    alibi[0, AUG - 1::AUG] = slopes
    alibi = jnp.asarray(alibi)
    slope_tab = jnp.asarray(np.broadcast_to(
        slopes.reshape(ATTN_HEADS // HEADS_PER_STEP, HEADS_PER_STEP, 1),
        (ATTN_HEADS // HEADS_PER_STEP, HEADS_PER_STEP, LANES)).copy())
    a0, a1, a2 = ATTN_WIDTH, 2 * ATTN_WIDTH, 3 * ATTN_WIDTH
    m1 = a2 + 4 * MLSTM_WIDTH
    g0 = m1 + 2 * MLSTM_HEADS

    outs = []
    for bi in range(batch):
        xb = x[bi]
        for layer in range(depth):
            w = w_in[layer]
            wq = w[:, :a0].T.astype(BF16)
            wk = w[:, a0:a1].astype(BF16)
            wv = _pad_heads(w[:, a1:a2], V_ROWS).T.astype(BF16)
            wm = w[:, a2:a2 + 2 * MLSTM_WIDTH].astype(BF16)
            wvo = w[:, a2 + 2 * MLSTM_WIDTH:m1].T.astype(BF16)
            wif = jnp.pad(w[:, m1:g0].T, ((0, GATE_ROWS - 2 * MLSTM_HEADS), (0, 0))).astype(BF16)
            wg = w[:, g0:].astype(BF16)
            gbias = jnp.pad(jnp.concatenate([b_igate[layer], b_fgate[layer]]).astype(F32),
                            (0, GATE_ROWS - 2 * MLSTM_HEADS))
            gbias = jnp.broadcast_to(gbias[:, None], (GATE_ROWS, LANES))
            ngt = jnp.broadcast_to(mlstm_norm_g[layer][:, None], (MLSTM_WIDTH, MLSTM_CHUNK)).astype(F32)
            q, k_aug, v_aug, kstat, gg, ym = _mixer_in(
                xb, mix_norm_g[layer][None, :], wq, wk, wv, wm, wvo, wif, wg, alibi,
                conv_qk[layer], gbias, ngt)
            pad_blocks = ((0, N_BLOCK_COLS - kstat.shape[0]), (0, 0))
            ya = _moba(slope_tab, q, jnp.pad(kstat[:, 0, :], pad_blocks), jnp.pad(kstat[:, 1, :], pad_blocks),
                       k_aug, v_aug)
            xb = _out_ffn(xb, ya, ym, gg,
                          w_out_attn[layer].astype(BF16), w_out_mlstm[layer].astype(BF16),
                          w_o[layer].astype(BF16), ffn_norm_g[layer][None, :],
                          w_ffn_gate[layer].astype(BF16), w_ffn_up[layer].astype(BF16),
                          w_ffn_down[layer].astype(BF16), final_norm_g[None, :],
                          final_norm=(layer == depth - 1))
        outs.append(xb)
    return outs[0][None] if batch == 1 else jnp.stack(outs, axis=0)
```

```python
import functools

import numpy as np
import jax
import jax.numpy as jnp
from jax import lax
from jax.experimental import pallas as pl
from jax.experimental.pallas import tpu as pltpu

D_MODEL = 1024
ATTN_HEADS = 8
ATTN_HEAD_DIM = 64
ATTN_WIDTH = ATTN_HEADS * ATTN_HEAD_DIM
MOBA_BLOCK = 256
MOBA_TOP_K = 3
MLSTM_HEADS = 4
MLSTM_HEAD_DIM = 128
MLSTM_WIDTH = MLSTM_HEADS * MLSTM_HEAD_DIM
MLSTM_CHUNK = 128
CONV_WIDTH = 4
FFN_HIDDEN = 2816
NORM_EPS = 1e-6

LANES = 128
AUG = 2 * ATTN_HEAD_DIM
ATTN_AUG_WIDTH = ATTN_HEADS * AUG
V_ROWS = 80
N_BLOCK_COLS = AUG - ATTN_HEAD_DIM
HEADS_PER_STEP = 2
CONV_HALO = 8
Q_TILES = 2
KEY_GROUP = 2
NEGLIGIBLE_LOG = 50.0
NORM_SLACK = 1.02
NEG_BIAS = -1e9
FFN_CHUNK = 256
OUT_TILE = 512
VMEM_LIMIT = 56 * 1024 * 1024

F32 = jnp.float32
BF16 = jnp.bfloat16
NT_DIMS = (((1,), (1,)), ((), ()))


def _rms(x, g):
    return x * lax.rsqrt(jnp.mean(x * x, axis=-1, keepdims=True) + NORM_EPS) * g


def _sigmoid(x):
    return 1.0 / (1.0 + jnp.exp(-x))


def _inproj_kernel(x_ref, g_ref, wq_ref, wk_ref, wv_ref, wm_ref, wvo_ref, wif_ref, wg_ref, alibi_ref,
                   q_ref, k_ref, v_ref, kmean_ref, qkm_ref, vom_ref, gates_ref, gg_ref, *, blk):
    hb = _rms(x_ref[...], g_ref[...]).astype(BF16)
    q_ref[0] = lax.dot_general(wq_ref[...], hb, NT_DIMS, preferred_element_type=F32)
    kc = jnp.dot(hb, wk_ref[...], preferred_element_type=F32)
    gap = jnp.zeros((MOBA_BLOCK, AUG - ATTN_HEAD_DIM), F32)
    kf = jnp.concatenate([piece for h in range(ATTN_HEADS)
                          for piece in (kc[:, h * ATTN_HEAD_DIM:(h + 1) * ATTN_HEAD_DIM], gap)], axis=1)
    ksq = kf * kf
    norm2 = [jnp.max(jnp.sum(ksq[:, h * AUG:(h + 1) * AUG], axis=1, keepdims=True), axis=0, keepdims=True)
             for h in range(ATTN_HEADS)]
    kmean_ref[0] = jnp.concatenate(
        [jnp.mean(kf, axis=0, keepdims=True),
         jnp.concatenate([jnp.broadcast_to(n2, (1, AUG)) for n2 in norm2], axis=1)], axis=0)
    lane = lax.broadcasted_iota(jnp.int32, (1, ATTN_AUG_WIDTH), 1) & (AUG - 1)
    row = lax.broadcasted_iota(jnp.int32, (MOBA_BLOCK, 1), 0).astype(F32)
    onehot = jnp.where(lane - ATTN_HEAD_DIM == blk, 1.0, 0.0)
    onehot = jnp.where(lane == AUG - 1, 0.0, onehot)
    k_ref[...] = (kf + onehot + alibi_ref[...] * row).astype(BF16)
    gg_ref[:, :D_MODEL] = jnp.dot(hb, wg_ref[:, :D_MODEL], preferred_element_type=F32)
    yield
    vt = lax.dot_general(wv_ref[...], hb, NT_DIMS, preferred_element_type=F32)
    feat = lax.rem(lax.broadcasted_iota(jnp.int32, (ATTN_HEADS * V_ROWS, 1), 0), V_ROWS)
    v_ref[0] = (vt + jnp.where(feat == ATTN_HEAD_DIM, 1.0, 0.0)).astype(BF16)
    qkm_ref[...] = jnp.dot(hb, wm_ref[...], preferred_element_type=F32)
    gates_ref[...] = jnp.dot(hb, wif_ref[...], preferred_element_type=F32)
    yield
    vom_ref[0] = lax.dot_general(wvo_ref[...], hb, NT_DIMS, preferred_element_type=F32)
    gg_ref[:, D_MODEL:] = jnp.dot(hb, wg_ref[:, D_MODEL:], preferred_element_type=F32)


def _col_max(st):
    rows, n = st.shape
    fan = 8
    while rows > fan * 8 and rows % fan == 0:
        st = jnp.max(st.reshape(fan, rows // fan, n), axis=0)
        rows //= fan
    return jnp.max(st, axis=0, keepdims=True)


def _moba_kernel(slope_ref, q_ref, qn_ref, kmean_ref, kn2_ref, k_ref, vt_ref, o_ref,
                 s_ref, qa_ref, cm_ref, reach_ref):
    step_id = pl.program_id(1)
    n_steps = pl.num_programs(1)
    i = step_id * Q_TILES
    tq = Q_TILES * MOBA_BLOCK
    nb = N_BLOCK_COLS
    group_rows = KEY_GROUP * MOBA_BLOCK
    assert KEY_GROUP % Q_TILES == 0
    heads = range(HEADS_PER_STEP)
    head_lanes = [slice(hh * AUG, (hh + 1) * AUG) for hh in heads]
    head_rows = [slice(hh * V_ROWS, (hh + 1) * V_ROWS) for hh in heads]
    own_slot = 2

    def prepare(qsrc_ref, tile0):
        blk = lax.broadcasted_iota(jnp.int32, (nb, tq), 0).astype(F32)
        i_f = (tile0 + lax.broadcasted_iota(jnp.int32, (1, tq), 1) // MOBA_BLOCK).astype(F32)
        reach = jnp.zeros((1, 1), F32)
        qk_cap, slopes = [], []
        for hh in heads:
            slope = slope_ref[0, hh:hh + 1, 0:1]
            qh = jnp.concatenate([qsrc_ref[t, hh * ATTN_HEAD_DIM:(hh + 1) * ATTN_HEAD_DIM, :]
                                  for t in range(Q_TILES)], axis=1)
            km = kmean_ref[:, hh * AUG:hh * AUG + ATTN_HEAD_DIM]
            gate = jnp.dot(km, qh, precision=lax.Precision.HIGHEST, preferred_element_type=F32)
            valid = blk < i_f
            g = jnp.where(valid, gate, -jnp.inf)
            sel = jnp.zeros((nb, tq), jnp.bool_)
            for _ in range(MOBA_TOP_K):
                mx = jnp.max(g, axis=0, keepdims=True)
                idx = jnp.min(jnp.where(g == mx, blk, float(nb)), axis=0, keepdims=True)
                pick = blk == idx
                sel = jnp.logical_or(sel, pick)
                g = jnp.where(pick, -jnp.inf, g)
            bias = jnp.where(sel, (blk - i_f) * (slope * float(MOBA_BLOCK)), NEG_BIAS)
            bias = jnp.where(valid, bias, 0.0)
            bias = jnp.where(blk == float(nb - 1), 1.0, bias)
            qa_ref[hh] = jnp.concatenate([qh * (ATTN_HEAD_DIM ** -0.5), bias], axis=0).astype(BF16)
            k_norm = jnp.sqrt(jnp.max(kn2_ref[:, hh * AUG:hh * AUG + 1], axis=0, keepdims=True))
            qk_cap.append(NORM_SLACK * (ATTN_HEAD_DIM ** -0.5) * k_norm
                          * jnp.sqrt(jnp.sum(qh * qh, axis=0, keepdims=True)))
            slopes.append(slope)
        first = (tile0 // KEY_GROUP) * KEY_GROUP
        hidden = (lax.broadcasted_iota(jnp.int32, (group_rows, tq), 0)
                  - lax.broadcasted_iota(jnp.int32, (group_rows, tq), 1)) > (tile0 - first) * MOBA_BLOCK
        start = pl.multiple_of(first * MOBA_BLOCK, group_rows)
        for hh in heads:
            st = jnp.dot(k_ref[pl.ds(start, group_rows), head_lanes[hh]], qa_ref[hh], preferred_element_type=F32)
            st = jnp.where(hidden, -jnp.inf, st)
            s_ref[own_slot, hh] = st
            own_max = _col_max(st)
            cm_ref[hh] = jnp.broadcast_to(own_max, (8, tq))
            spread = jnp.max(qk_cap[hh] - own_max, axis=1, keepdims=True)
            reach = jnp.maximum(reach, (NEGLIGIBLE_LOG + spread) / (slopes[hh] * float(MOBA_BLOCK)) + 1.0)
        reach = jnp.ceil(reach)
        reach = jnp.where(reach < float(nb), reach, float(nb))
        reach_ref[...] = jnp.broadcast_to(reach, reach_ref.shape).astype(jnp.int32)

    def score(first, slot):
        start = pl.multiple_of(first * MOBA_BLOCK, group_rows)
        col_max = []
        for hh in heads:
            st = jnp.dot(k_ref[pl.ds(start, group_rows), head_lanes[hh]], qa_ref[hh], preferred_element_type=F32)
            s_ref[slot, hh] = st
            col_max.append(_col_max(st))
        return col_max

    def consume(first, slot, m_prev, m_cur, accs):
        ps = [jnp.exp(s_ref[slot, hh] - m_cur[hh]).astype(BF16) for hh in heads]
        accs = [jnp.exp(m_prev[hh] - m_cur[hh]) * accs[hh] for hh in heads]
        for b in range(KEY_GROUP):
            for hh in heads:
                accs[hh] = accs[hh] + jnp.dot(vt_ref[first + b, head_rows[hh], :],
                                              ps[hh][b * MOBA_BLOCK:(b + 1) * MOBA_BLOCK, :],
                                              preferred_element_type=F32)
        return accs

    def step(first, slot, src, carry):
        pend, m_prev, m_cur, accs = carry
        cm = score(first, slot)
        accs = consume(pend, src, m_prev, m_cur, accs)
        return (jnp.asarray(first, jnp.int32), m_cur,
                [jnp.maximum(m_cur[hh], cm[hh]) for hh in heads], accs)

    def pair(first_group, carry, src=0):
        g = first_group * KEY_GROUP
        return step(g + KEY_GROUP, 0, 1, step(g, 1, src, carry))

    @pl.when(step_id == 0)
    def _():
        prepare(q_ref, i)

    own_first = (i // KEY_GROUP) * KEY_GROUP
    first_block = jnp.maximum(i - jnp.max(reach_ref[...]) + 1, 0)
    first_group = first_block // KEY_GROUP
    n_past = i // KEY_GROUP - first_group
    cm0 = [cm_ref[hh, 0:1, :] for hh in heads]
    init = (jnp.asarray(own_first, jnp.int32), cm0, cm0, [jnp.zeros((V_ROWS, tq), F32) for _ in heads])

    def start_none(c):
        for hh in heads:
            s_ref[0, hh] = s_ref[own_slot, hh]
        return c

    def start_odd(c):
        return step(first_group * KEY_GROUP, 0, own_slot, c)

    def start_even(c):
        return pair(first_group, c, src=own_slot)

    odd = n_past % 2
    carry = lax.cond(n_past == 0, start_none,
                     lambda c: lax.cond(odd == 1, start_odd, start_even, c), init)
    done = first_group + jnp.where(n_past == 0, 0, 2 - odd)
    n_pairs = (i // KEY_GROUP - done) // 2
    odd_pair = n_pairs % 2
    carry = lax.cond(odd_pair == 1, functools.partial(pair, done), lambda c: c, carry)
    done = done + 2 * odd_pair

    def quad(u, carry):
        return pair(done + 4 * u + 2, pair(done + 4 * u, carry))

    pend, m_prev, m_cur, accs = lax.fori_loop(0, n_pairs // 2, quad, carry)
    accs = consume(pend, 0, m_prev, m_cur, accs)
    prepare(qn_ref, jnp.minimum(step_id + 1, n_steps - 1) * Q_TILES)
    outs = []
    for hh in heads:
        ot = accs[hh][:ATTN_HEAD_DIM, :] / accs[hh][ATTN_HEAD_DIM:ATTN_HEAD_DIM + 1, :]
        outs.append(ot.T)
    o_ref[...] = jnp.concatenate(outs, axis=1).astype(o_ref.dtype)


def _moba(slope_tab, qt, kmean, knorm2, k_aug, vt_aug):
    s = k_aug.shape[0]
    nb = s // MOBA_BLOCK
    n_pairs = ATTN_HEADS // HEADS_PER_STEP
    n_steps = nb // Q_TILES
    tq = Q_TILES * MOBA_BLOCK
    q_block = (Q_TILES, HEADS_PER_STEP * ATTN_HEAD_DIM, MOBA_BLOCK)
    return pl.pallas_call(
        _moba_kernel,
        grid=(n_pairs, n_steps),
        in_specs=[pl.BlockSpec((1, HEADS_PER_STEP, LANES), lambda p, i: (p, 0, 0)),
                  pl.BlockSpec(q_block, lambda p, i: (i, p, 0)),
                  pl.BlockSpec(q_block, lambda p, i: (jnp.minimum(i + 1, n_steps - 1), p, 0)),
                  pl.BlockSpec((N_BLOCK_COLS, HEADS_PER_STEP * AUG), lambda p, i: (0, p)),
                  pl.BlockSpec((N_BLOCK_COLS, HEADS_PER_STEP * AUG), lambda p, i: (0, p)),
                  pl.BlockSpec((s, HEADS_PER_STEP * AUG), lambda p, i: (0, p)),
                  pl.BlockSpec((nb, HEADS_PER_STEP * V_ROWS, MOBA_BLOCK), lambda p, i: (0, p, 0))],
        out_specs=pl.BlockSpec((tq, HEADS_PER_STEP * ATTN_HEAD_DIM), lambda p, i: (i, p)),
        out_shape=jax.ShapeDtypeStruct((s, ATTN_WIDTH), BF16),
        scratch_shapes=[pltpu.VMEM((3, HEADS_PER_STEP, KEY_GROUP * MOBA_BLOCK, tq), F32),
                        pltpu.VMEM((HEADS_PER_STEP, AUG, tq), BF16),
                        pltpu.VMEM((HEADS_PER_STEP, 8, tq), F32),
                        pltpu.VMEM((8, LANES), jnp.int32)],
        compiler_params=pltpu.CompilerParams(dimension_semantics=("arbitrary", "arbitrary"),
                                             vmem_limit_bytes=VMEM_LIMIT),
        name="moba",
    )(slope_tab, qt, qt, kmean, knorm2, k_aug, vt_aug)


def _mlstm_kernel(qk_ref, vot_ref, gates_ref, conv_ref, gbias_ref, ngt_ref, y_ref,
                  xbuf, ct_ref, n_ref, m_ref):
    L = MLSTM_CHUNK
    d = MLSTM_HEAD_DIM
    rows = qk_ref.shape[0]
    halo = xbuf.shape[0] - rows
    chunks = range(rows // L)
    heads = range(MLSTM_HEADS)

    @pl.when(pl.program_id(0) == 0)
    def _():
        xbuf[0:halo, :] = jnp.zeros((halo, xbuf.shape[1]), F32)
        ct_ref[...] = jnp.zeros(ct_ref.shape, F32)
        n_ref[...] = jnp.zeros(n_ref.shape, F32)
        m_ref[...] = jnp.zeros(m_ref.shape, F32)

    xbuf[halo:, :] = qk_ref[...]
    w = conv_ref[...]
    y = qk_ref[...] * w[CONV_WIDTH - 1:CONV_WIDTH, :]
    for j in range(CONV_WIDTH - 1):
        off = halo - CONV_WIDTH + 1 + j
        y = y + xbuf[off:off + rows, :] * w[j:j + 1, :]
    xbuf[0:halo, :] = xbuf[rows:rows + halo, :]
    qk = y * _sigmoid(y)

    a = gates_ref[...] + gbias_ref[...]
    lane = lax.broadcasted_iota(jnp.int32, (rows, LANES), 1)
    is_f = jnp.logical_and(lane >= MLSTM_HEADS, lane < 2 * MLSTM_HEADS)
    log_f = jnp.where(is_f, jnp.minimum(a, 0.0) - jnp.log(1.0 + jnp.exp(-jnp.abs(a))), 0.0)
    t_io = lax.broadcasted_iota(jnp.int32, (rows, rows), 0)
    s_io = lax.broadcasted_iota(jnp.int32, (rows, rows), 1)
    same_chunk = (t_io // L) == (s_io // L)
    tri = jnp.where(jnp.logical_and(s_io <= t_io, same_chunk), 1.0, 0.0)
    bcum = jnp.dot(tri, log_f, precision=lax.Precision.HIGHEST,
                   preferred_element_type=F32)
    colm = jnp.where(lane < MLSTM_HEADS, a, bcum)
    rowm = colm.T
    causal_t = (lax.broadcasted_iota(jnp.int32, (L, L), 0) <= lax.broadcasted_iota(jnp.int32, (L, L), 1))

    yield
    pairs = [(cc, hd) for cc in chunks for hd in heads]
    rs = lambda cc: slice(cc * L, (cc + 1) * L)
    fs = lambda hd: slice(hd * d, (hd + 1) * d)
    og_t = {(cc, hd): vot_ref[0, MLSTM_WIDTH + hd * d:MLSTM_WIDTH + (hd + 1) * d, rs(cc)] for cc, hd in pairs}
    qtb = {(cc, hd): qk[rs(cc), fs(hd)].T.astype(BF16) for cc, hd in pairs}
    kb = {(cc, hd): (qk[rs(cc), MLSTM_WIDTH + hd * d:MLSTM_WIDTH + (hd + 1) * d] * (d ** -0.5)).astype(BF16)
          for cc, hd in pairs}
    vt = {(cc, hd): vot_ref[0, fs(hd), rs(cc)] for cc, hd in pairs}
    ic_r = {(cc, hd): rowm[hd:hd + 1, rs(cc)] for cc, hd in pairs}
    bc_r = {(cc, hd): rowm[MLSTM_HEADS + hd:MLSTM_HEADS + hd + 1, rs(cc)] for cc, hd in pairs}
    e_c = {(cc, hd): colm[rs(cc), hd:hd + 1] - colm[rs(cc), MLSTM_HEADS + hd:MLSTM_HEADS + hd + 1]
           for cc, hd in pairs}
    dmat_t = {p: jnp.where(causal_t, bc_r[p] + e_c[p], -jnp.inf) for p in pairs}
    a_row = {p: jnp.max(dmat_t[p], axis=0, keepdims=True) for p in pairs}
    sc_t = {p: jnp.dot(kb[p], qtb[p], preferred_element_type=F32) * jnp.exp(dmat_t[p] - a_row[p]) for p in pairs}
    sv_t = {p: jnp.dot(vt[p].astype(BF16), sc_t[p].astype(BF16), preferred_element_type=F32) for p in pairs}
    ssum = {p: jnp.sum(sc_t[p], axis=0, keepdims=True) for p in pairs}
    b_last = {p: bc_r[p][:, L - 1:L] for p in pairs}
    g_r = {p: b_last[p] - bc_r[p] + ic_r[p] for p in pairs}
    gmax = {p: jnp.max(g_r[p], axis=1, keepdims=True) for p in pairs}
    w_row = {p: jnp.exp(g_r[p] - gmax[p]) for p in pairs}
    upd_t = {p: jnp.dot((vt[p] * w_row[p]).astype(BF16), kb[p], preferred_element_type=F32) for p in pairs}
    ksum = {p: jnp.dot(jnp.broadcast_to(w_row[p], (8, L)).astype(BF16), kb[p],
                       preferred_element_type=F32)[0:1, :] for p in pairs}
    yield
    h_t = {}
    for hd in heads:
        ct_st, n_st, m_prev = ct_ref[hd], n_ref[hd], m_ref[hd:hd + 1, 0:1]
        for cc in chunks:
            p = (cc, hd)
            inter = bc_r[p] + m_prev
            m_row = jnp.maximum(inter, a_row[p])
            r_row = jnp.exp(a_row[p] - m_row)
            w_inter = jnp.exp(inter - m_row)
            qc_t = jnp.dot(ct_st.astype(BF16), qtb[p], preferred_element_type=F32)
            qn = jnp.dot(jnp.broadcast_to(n_st, (8, d)).astype(BF16), qtb[p],
                         preferred_element_type=F32)[0:1, :]
            den = w_inter * qn + r_row * ssum[p]
            scale = 1.0 / jnp.maximum(jnp.abs(den), jnp.exp(-m_row))
            h_t[p] = (w_inter * scale) * qc_t + (r_row * scale) * sv_t[p]
            m_new = jnp.maximum(b_last[p] + m_prev, gmax[p])
            decay = jnp.exp(b_last[p] + m_prev - m_new)
            gain = jnp.exp(gmax[p] - m_new)
            ct_st = decay * ct_st + gain * upd_t[p]
            n_st = decay * n_st + gain * ksum[p]
            m_prev = m_new
        ct_ref[hd], n_ref[hd] = ct_st, n_st
        m_ref[hd:hd + 1, :] = jnp.broadcast_to(m_prev, (1, LANES))
    yield
    for cc, hd in pairs:
        hg = h_t[(cc, hd)] * _sigmoid(og_t[(cc, hd)])
        mu = jnp.mean(hg, axis=0, keepdims=True)
        var = jnp.mean(jnp.square(hg - mu), axis=0, keepdims=True)
        yn = (hg - mu) * lax.rsqrt(var + NORM_EPS) * ngt_ref[fs(hd), :]
        y_ref[rs(cc), fs(hd)] = yn.T.astype(y_ref.dtype)


def _mixer_in_kernel(x_ref, g_ref, wq_ref, wk_ref, wv_ref, wm_ref, wvo_ref, wif_ref, wg_ref, alibi_ref,
                     conv_ref, gbias_ref, ngt_ref,
                     q_ref, k_ref, v_ref, kmean_ref, gg_ref, y_ref,
                     qk_s, vot_s, gates_s, xbuf, ct_ref, n_ref, m_ref):
    t = pl.program_id(0)
    last = pl.num_programs(0) - 2

    @pl.when(t == 0)
    def _():
        qk_s[...] = jnp.zeros(qk_s.shape, F32)
        vot_s[...] = jnp.zeros(vot_s.shape, F32)
        gates_s[...] = jnp.zeros(gates_s.shape, F32)

    mlstm = _mlstm_kernel(qk_s, vot_s, gates_s, conv_ref, gbias_ref, ngt_ref, y_ref, xbuf, ct_ref, n_ref, m_ref)
    proj = _inproj_kernel(x_ref, g_ref, wq_ref, wk_ref, wv_ref, wm_ref, wvo_ref, wif_ref, wg_ref, alibi_ref,
                          q_ref, k_ref, v_ref, kmean_ref, qk_s, vot_s, gates_s, gg_ref, blk=jnp.minimum(t, last))
    for _ in range(3):
        next(mlstm)
        next(proj, None)
    for _ in mlstm:
        pass

    @pl.when(t == 0)
    def _():
        ct_ref[...] = jnp.zeros(ct_ref.shape, F32)
        n_ref[...] = jnp.zeros(n_ref.shape, F32)
        m_ref[...] = jnp.zeros(m_ref.shape, F32)


def _mixer_in(x, g, wq, wk, wv, wm, wvo, wif, wg, alibi, conv, gbias, ngt):
    s = x.shape[0]
    nb = s // MOBA_BLOCK
    cur = lambda t: jnp.minimum(t, nb - 1)
    row = lambda w: pl.BlockSpec((MOBA_BLOCK, w), lambda t: (cur(t), 0))
    tile3 = lambda r: pl.BlockSpec((1, r, MOBA_BLOCK), lambda t: (cur(t), 0, 0))
    whole = pl.BlockSpec(memory_space=pltpu.VMEM)
    return pl.pallas_call(
        _mixer_in_kernel,
        grid=(nb + 1,),
        in_specs=[row(D_MODEL)] + [whole] * 12,
        out_specs=[tile3(ATTN_WIDTH), row(ATTN_AUG_WIDTH), tile3(ATTN_HEADS * V_ROWS),
                   pl.BlockSpec((1, 2, ATTN_AUG_WIDTH), lambda t: (cur(t), 0, 0)),
                   row(2 * D_MODEL),
                   pl.BlockSpec((MOBA_BLOCK, MLSTM_WIDTH), lambda t: (jnp.maximum(t - 1, 0), 0))],
        out_shape=[jax.ShapeDtypeStruct((nb, ATTN_WIDTH, MOBA_BLOCK), F32),
                   jax.ShapeDtypeStruct((s, ATTN_AUG_WIDTH), BF16),
                   jax.ShapeDtypeStruct((nb, ATTN_HEADS * V_ROWS, MOBA_BLOCK), BF16),
                   jax.ShapeDtypeStruct((nb, 2, ATTN_AUG_WIDTH), F32),
                   jax.ShapeDtypeStruct((s, 2 * D_MODEL), F32),
                   jax.ShapeDtypeStruct((s, MLSTM_WIDTH), BF16)],
        scratch_shapes=[pltpu.VMEM((MOBA_BLOCK, 2 * MLSTM_WIDTH), F32),
                        pltpu.VMEM((1, 2 * MLSTM_WIDTH, MOBA_BLOCK), F32),
                        pltpu.VMEM((MOBA_BLOCK, LANES), F32),
                        pltpu.VMEM((MOBA_BLOCK + CONV_HALO, 2 * MLSTM_WIDTH), F32),
                        pltpu.VMEM((MLSTM_HEADS, MLSTM_HEAD_DIM, MLSTM_HEAD_DIM), F32),
                        pltpu.VMEM((MLSTM_HEADS, 1, MLSTM_HEAD_DIM), F32),
                        pltpu.VMEM((8, LANES), F32)],
        compiler_params=pltpu.CompilerParams(dimension_semantics=("arbitrary",),
                                             vmem_limit_bytes=VMEM_LIMIT),
        name="mixer_in",
    )(x, g, wq, wk, wv, wm, wvo, wif, wg, alibi, conv, gbias, ngt)


def _out_ffn_kernel(x_ref, ya_ref, ym_ref, gg_ref, woa_ref, wom_ref, wo_ref, fg_ref,
                    wgt_ref, wup_ref, wdn_ref, fin_ref, o_ref, *, final_norm):
    a = jnp.dot(ya_ref[...], woa_ref[...], preferred_element_type=F32)
    b = jnp.dot(ym_ref[...], wom_ref[...], preferred_element_type=F32)
    merged = _sigmoid(gg_ref[:, :D_MODEL]) * a + _sigmoid(gg_ref[:, D_MODEL:]) * b
    x1 = x_ref[...] + jnp.dot(merged.astype(BF16), wo_ref[...], preferred_element_type=F32)
    h2 = _rms(x1, fg_ref[...]).astype(BF16)
    acc = jnp.zeros(x1.shape, F32)
    for c in range(0, FFN_HIDDEN, FFN_CHUNK):
        gt = jnp.dot(h2, wgt_ref[:, c:c + FFN_CHUNK], preferred_element_type=F32)
        up = jnp.dot(h2, wup_ref[:, c:c + FFN_CHUNK], preferred_element_type=F32)
        act = (gt * _sigmoid(gt) * up).astype(BF16)
        acc = acc + jnp.dot(act, wdn_ref[c:c + FFN_CHUNK, :], preferred_element_type=F32)
    x2 = x1 + acc
    o_ref[...] = _rms(x2, fin_ref[...]) if final_norm else x2


def _out_ffn(x, ya, ym, gg, woa, wom, wo, fg, wgt, wup, wdn, fin, final_norm):
    s = x.shape[0]
    row = lambda w: pl.BlockSpec((OUT_TILE, w), lambda i: (i, 0))
    whole = pl.BlockSpec(memory_space=pltpu.VMEM)
    return pl.pallas_call(
        functools.partial(_out_ffn_kernel, final_norm=final_norm),
        grid=(s // OUT_TILE,),
        in_specs=[row(D_MODEL), row(ATTN_WIDTH), row(MLSTM_WIDTH), row(2 * D_MODEL)] + [whole] * 8,
        out_specs=row(D_MODEL),
        out_shape=jax.ShapeDtypeStruct((s, D_MODEL), F32),
        compiler_params=pltpu.CompilerParams(dimension_semantics=("arbitrary",),
                                             vmem_limit_bytes=VMEM_LIMIT),
        name="out_ffn",
    )(x, ya, ym, gg, woa, wom, wo, fg, wgt, wup, wdn, fin)


def _alibi_slopes():
    return np.exp2(-8.0 * np.arange(1, ATTN_HEADS + 1, dtype=np.float64) / ATTN_HEADS).astype(np.float32)


def _pad_heads(w, width):
    d = w.shape[0]
    w = w.reshape(d, ATTN_HEADS, ATTN_HEAD_DIM)
    w = jnp.pad(w, ((0, 0), (0, 0), (0, width - ATTN_HEAD_DIM)))
    return w.reshape(d, ATTN_HEADS * width)


def kernel(x, mix_norm_g, w_in, conv_qk, b_igate, b_fgate, mlstm_norm_g, w_out_attn, w_out_mlstm,
           w_o, ffn_norm_g, w_ffn_gate, w_ffn_up, w_ffn_down, final_norm_g):
    batch, seq, _ = x.shape
    depth = w_in.shape[0]
    assert seq % OUT_TILE == 0 and seq // MOBA_BLOCK <= N_BLOCK_COLS
    assert seq % (KEY_GROUP * MOBA_BLOCK) == 0
    slopes = _alibi_slopes()
    alibi = np.zeros((1, ATTN_AUG_WIDTH), np.float32)
    alibi[0, AUG - 1::AUG] = slopes
    alibi = jnp.asarray(alibi)
    slope_tab = jnp.asarray(np.broadcast_to(
        slopes.reshape(ATTN_HEADS // HEADS_PER_STEP, HEADS_PER_STEP, 1),
        (ATTN_HEADS // HEADS_PER_STEP, HEADS_PER_STEP, LANES)).copy())
    a0, a1, a2 = ATTN_WIDTH, 2 * ATTN_WIDTH, 3 * ATTN_WIDTH
    m1 = a2 + 4 * MLSTM_WIDTH
    g0 = m1 + 2 * MLSTM_HEADS

    outs = []
    for bi in range(batch):
        xb = x[bi]
        for layer in range(depth):
            w = w_in[layer]
            wq = w[:, :a0].T.astype(BF16)
            wk = w[:, a0:a1].astype(BF16)
            wv = _pad_heads(w[:, a1:a2], V_ROWS).T.astype(BF16)
            wm = w[:, a2:a2 + 2 * MLSTM_WIDTH].astype(BF16)
            wvo = w[:, a2 + 2 * MLSTM_WIDTH:m1].T.astype(BF16)
            wif = jnp.pad(w[:, m1:g0], ((0, 0), (0, LANES - 2 * MLSTM_HEADS))).astype(BF16)
            wg = w[:, g0:].astype(BF16)
            gbias = jnp.pad(jnp.concatenate([b_igate[layer], b_fgate[layer]])[None, :],
                            ((0, 0), (0, LANES - 2 * MLSTM_HEADS))).astype(F32)
            ngt = jnp.broadcast_to(mlstm_norm_g[layer][:, None], (MLSTM_WIDTH, MLSTM_CHUNK)).astype(F32)
            q, k_aug, v_aug, kstat, gg, ym = _mixer_in(
                xb, mix_norm_g[layer][None, :], wq, wk, wv, wm, wvo, wif, wg, alibi,
                conv_qk[layer], gbias, ngt)
            pad_blocks = ((0, N_BLOCK_COLS - kstat.shape[0]), (0, 0))
            ya = _moba(slope_tab, q, jnp.pad(kstat[:, 0, :], pad_blocks), jnp.pad(kstat[:, 1, :], pad_blocks),
                       k_aug, v_aug)
            xb = _out_ffn(xb, ya, ym, gg,
                          w_out_attn[layer].astype(BF16), w_out_mlstm[layer].astype(BF16),
                          w_o[layer].astype(BF16), ffn_norm_g[layer][None, :],
                          w_ffn_gate[layer].astype(BF16), w_ffn_up[layer].astype(BF16),
                          w_ffn_down[layer].astype(BF16), final_norm_g[None, :],
                          final_norm=(layer == depth - 1))
        outs.append(xb)
    return outs[0][None] if batch == 1 else jnp.stack(outs, axis=0)
```

```python
import functools

import numpy as np
import jax
import jax.numpy as jnp
from jax import lax
from jax.experimental import pallas as pl
from jax.experimental.pallas import tpu as pltpu

D_MODEL = 1024
ATTN_HEADS = 8
ATTN_HEAD_DIM = 64
ATTN_WIDTH = ATTN_HEADS * ATTN_HEAD_DIM
MOBA_BLOCK = 256
MOBA_TOP_K = 3
MLSTM_HEADS = 4
MLSTM_HEAD_DIM = 128
MLSTM_WIDTH = MLSTM_HEADS * MLSTM_HEAD_DIM
MLSTM_CHUNK = 128
CONV_WIDTH = 4
FFN_HIDDEN = 2816
NORM_EPS = 1e-6

LANES = 128
AUG = 2 * ATTN_HEAD_DIM
ATTN_AUG_WIDTH = ATTN_HEADS * AUG
V_ROWS = 80
N_BLOCK_COLS = AUG - ATTN_HEAD_DIM
HEADS_PER_STEP = 2
CONV_HALO = 8
Q_TILES = 2
KEY_GROUP = 2
NEGLIGIBLE_LOG = float("inf")
NORM_SLACK = 1.02
NEG_BIAS = -1e9
FFN_CHUNK = 256
OUT_TILE = 512
VMEM_LIMIT = 56 * 1024 * 1024

F32 = jnp.float32
BF16 = jnp.bfloat16
NT_DIMS = (((1,), (1,)), ((), ()))


def _rms(x, g):
    return x * lax.rsqrt(jnp.mean(x * x, axis=-1, keepdims=True) + NORM_EPS) * g


def _sigmoid(x):
    return 1.0 / (1.0 + jnp.exp(-x))


def _inproj_kernel(x_ref, g_ref, wq_ref, wk_ref, wv_ref, wm_ref, wvo_ref, wif_ref, wg_ref, alibi_ref,
                   q_ref, k_ref, v_ref, kmean_ref, qkm_ref, vom_ref, gates_ref, gg_ref, *, blk):
    hb = _rms(x_ref[...], g_ref[...]).astype(BF16)
    q_ref[0] = lax.dot_general(wq_ref[...], hb, NT_DIMS, preferred_element_type=F32)
    kc = jnp.dot(hb, wk_ref[...], preferred_element_type=F32)
    gap = jnp.zeros((MOBA_BLOCK, AUG - ATTN_HEAD_DIM), F32)
    kf = jnp.concatenate([piece for h in range(ATTN_HEADS)
                          for piece in (kc[:, h * ATTN_HEAD_DIM:(h + 1) * ATTN_HEAD_DIM], gap)], axis=1)
    ksq = kf * kf
    norm2 = [jnp.max(jnp.sum(ksq[:, h * AUG:(h + 1) * AUG], axis=1, keepdims=True), axis=0, keepdims=True)
             for h in range(ATTN_HEADS)]
    kmean_ref[0] = jnp.concatenate(
        [jnp.mean(kf, axis=0, keepdims=True),
         jnp.concatenate([jnp.broadcast_to(n2, (1, AUG)) for n2 in norm2], axis=1)], axis=0)
    lane = lax.broadcasted_iota(jnp.int32, (1, ATTN_AUG_WIDTH), 1) & (AUG - 1)
    row = lax.broadcasted_iota(jnp.int32, (MOBA_BLOCK, 1), 0).astype(F32)
    onehot = jnp.where(lane - ATTN_HEAD_DIM == blk, 1.0, 0.0)
    onehot = jnp.where(lane == AUG - 1, 0.0, onehot)
    k_ref[...] = (kf + onehot + alibi_ref[...] * row).astype(BF16)
    gg_ref[:, :D_MODEL] = jnp.dot(hb, wg_ref[:, :D_MODEL], preferred_element_type=F32)
    yield
    vt = lax.dot_general(wv_ref[...], hb, NT_DIMS, preferred_element_type=F32)
    feat = lax.rem(lax.broadcasted_iota(jnp.int32, (ATTN_HEADS * V_ROWS, 1), 0), V_ROWS)
    v_ref[0] = (vt + jnp.where(feat == ATTN_HEAD_DIM, 1.0, 0.0)).astype(BF16)
    qkm_ref[...] = jnp.dot(hb, wm_ref[...], preferred_element_type=F32)
    gates_ref[...] = jnp.dot(hb, wif_ref[...], preferred_element_type=F32)
    yield
    vom_ref[0] = lax.dot_general(wvo_ref[...], hb, NT_DIMS, preferred_element_type=F32)
    gg_ref[:, D_MODEL:] = jnp.dot(hb, wg_ref[:, D_MODEL:], preferred_element_type=F32)


def _col_max(st):
    rows, n = st.shape
    fan = 8
    while rows > fan * 8 and rows % fan == 0:
        st = jnp.max(st.reshape(fan, rows // fan, n), axis=0)
        rows //= fan
    return jnp.max(st, axis=0, keepdims=True)


def _moba_kernel(slope_ref, q_ref, qn_ref, kmean_ref, kn2_ref, k_ref, vt_ref, o_ref,
                 s_ref, qa_ref, cm_ref, reach_ref):
    step_id = pl.program_id(1)
    n_steps = pl.num_programs(1)
    i = step_id * Q_TILES
    tq = Q_TILES * MOBA_BLOCK
    nb = N_BLOCK_COLS
    group_rows = KEY_GROUP * MOBA_BLOCK
    assert KEY_GROUP % Q_TILES == 0
    heads = range(HEADS_PER_STEP)
    head_lanes = [slice(hh * AUG, (hh + 1) * AUG) for hh in heads]
    head_rows = [slice(hh * V_ROWS, (hh + 1) * V_ROWS) for hh in heads]
    own_slot = 2

    def prepare(qsrc_ref, tile0):
        blk = lax.broadcasted_iota(jnp.int32, (nb, tq), 0).astype(F32)
        i_f = (tile0 + lax.broadcasted_iota(jnp.int32, (1, tq), 1) // MOBA_BLOCK).astype(F32)
        reach = jnp.zeros((1, 1), F32)
        qk_cap, slopes = [], []
        for hh in heads:
            slope = slope_ref[0, hh:hh + 1, 0:1]
            qh = jnp.concatenate([qsrc_ref[t, hh * ATTN_HEAD_DIM:(hh + 1) * ATTN_HEAD_DIM, :]
                                  for t in range(Q_TILES)], axis=1)
            km = kmean_ref[:, hh * AUG:hh * AUG + ATTN_HEAD_DIM]
            gate = jnp.dot(km, qh, precision=lax.Precision.HIGHEST, preferred_element_type=F32)
            valid = blk < i_f
            g = jnp.where(valid, gate, -jnp.inf)
            sel = jnp.zeros((nb, tq), jnp.bool_)
            for _ in range(MOBA_TOP_K):
                mx = jnp.max(g, axis=0, keepdims=True)
                idx = jnp.min(jnp.where(g == mx, blk, float(nb)), axis=0, keepdims=True)
                pick = blk == idx
                sel = jnp.logical_or(sel, pick)
                g = jnp.where(pick, -jnp.inf, g)
            bias = jnp.where(sel, (blk - i_f) * (slope * float(MOBA_BLOCK)), NEG_BIAS)
            bias = jnp.where(valid, bias, 0.0)
            bias = jnp.where(blk == float(nb - 1), 1.0, bias)
            qa_ref[hh] = jnp.concatenate([qh * (ATTN_HEAD_DIM ** -0.5), bias], axis=0).astype(BF16)
            k_norm = jnp.sqrt(jnp.max(kn2_ref[:, hh * AUG:hh * AUG + 1], axis=0, keepdims=True))
            qk_cap.append(NORM_SLACK * (ATTN_HEAD_DIM ** -0.5) * k_norm
                          * jnp.sqrt(jnp.sum(qh * qh, axis=0, keepdims=True)))
            slopes.append(slope)
        first = (tile0 // KEY_GROUP) * KEY_GROUP
        hidden = (lax.broadcasted_iota(jnp.int32, (group_rows, tq), 0)
                  - lax.broadcasted_iota(jnp.int32, (group_rows, tq), 1)) > (tile0 - first) * MOBA_BLOCK
        start = pl.multiple_of(first * MOBA_BLOCK, group_rows)
        for hh in heads:
            st = jnp.dot(k_ref[pl.ds(start, group_rows), head_lanes[hh]], qa_ref[hh], preferred_element_type=F32)
            st = jnp.where(hidden, -jnp.inf, st)
            s_ref[own_slot, hh] = st
            own_max = _col_max(st)
            cm_ref[hh] = jnp.broadcast_to(own_max, (8, tq))
            spread = jnp.max(qk_cap[hh] - own_max, axis=1, keepdims=True)
            reach = jnp.maximum(reach, (NEGLIGIBLE_LOG + spread) / (slopes[hh] * float(MOBA_BLOCK)) + 1.0)
        reach = jnp.ceil(reach)
        reach = jnp.where(reach < float(nb), reach, float(nb))
        reach_ref[...] = jnp.broadcast_to(reach, reach_ref.shape).astype(jnp.int32)

    def score(first, slot):
        start = pl.multiple_of(first * MOBA_BLOCK, group_rows)
        col_max = []
        for hh in heads:
            st = jnp.dot(k_ref[pl.ds(start, group_rows), head_lanes[hh]], qa_ref[hh], preferred_element_type=F32)
            s_ref[slot, hh] = st
            col_max.append(_col_max(st))
        return col_max

    def consume(first, slot, m_prev, m_cur, accs):
        ps = [jnp.exp(s_ref[slot, hh] - m_cur[hh]).astype(BF16) for hh in heads]
        accs = [jnp.exp(m_prev[hh] - m_cur[hh]) * accs[hh] for hh in heads]
        for b in range(KEY_GROUP):
            for hh in heads:
                accs[hh] = accs[hh] + jnp.dot(vt_ref[first + b, head_rows[hh], :],
                                              ps[hh][b * MOBA_BLOCK:(b + 1) * MOBA_BLOCK, :],
                                              preferred_element_type=F32)
        return accs

    def step(first, slot, src, carry):
        pend, m_prev, m_cur, accs = carry
        cm = score(first, slot)
        accs = consume(pend, src, m_prev, m_cur, accs)
        return (jnp.asarray(first, jnp.int32), m_cur,
                [jnp.maximum(m_cur[hh], cm[hh]) for hh in heads], accs)

    def pair(first_group, carry, src=0):
        g = first_group * KEY_GROUP
        return step(g + KEY_GROUP, 0, 1, step(g, 1, src, carry))

    @pl.when(step_id == 0)
    def _():
        prepare(q_ref, i)

    own_first = (i // KEY_GROUP) * KEY_GROUP
    first_block = jnp.maximum(i - jnp.max(reach_ref[...]) + 1, 0)
    first_group = first_block // KEY_GROUP
    n_past = i // KEY_GROUP - first_group
    cm0 = [cm_ref[hh, 0:1, :] for hh in heads]
    init = (jnp.asarray(own_first, jnp.int32), cm0, cm0, [jnp.zeros((V_ROWS, tq), F32) for _ in heads])

    def start_none(c):
        for hh in heads:
            s_ref[0, hh] = s_ref[own_slot, hh]
        return c

    def start_odd(c):
        return step(first_group * KEY_GROUP, 0, own_slot, c)

    def start_even(c):
        return pair(first_group, c, src=own_slot)

    odd = n_past % 2
    carry = lax.cond(n_past == 0, start_none,
                     lambda c: lax.cond(odd == 1, start_odd, start_even, c), init)
    done = first_group + jnp.where(n_past == 0, 0, 2 - odd)
    n_pairs = (i // KEY_GROUP - done) // 2
    odd_pair = n_pairs % 2
    carry = lax.cond(odd_pair == 1, functools.partial(pair, done), lambda c: c, carry)
    done = done + 2 * odd_pair

    def quad(u, carry):
        return pair(done + 4 * u + 2, pair(done + 4 * u, carry))

    pend, m_prev, m_cur, accs = lax.fori_loop(0, n_pairs // 2, quad, carry)
    accs = consume(pend, 0, m_prev, m_cur, accs)
    prepare(qn_ref, jnp.minimum(step_id + 1, n_steps - 1) * Q_TILES)
    outs = []
    for hh in heads:
        ot = accs[hh][:ATTN_HEAD_DIM, :] / accs[hh][ATTN_HEAD_DIM:ATTN_HEAD_DIM + 1, :]
        outs.append(ot.T)
    o_ref[...] = jnp.concatenate(outs, axis=1).astype(o_ref.dtype)


def _moba(slope_tab, qt, kmean, knorm2, k_aug, vt_aug):
    s = k_aug.shape[0]
    nb = s // MOBA_BLOCK
    n_pairs = ATTN_HEADS // HEADS_PER_STEP
    n_steps = nb // Q_TILES
    tq = Q_TILES * MOBA_BLOCK
    q_block = (Q_TILES, HEADS_PER_STEP * ATTN_HEAD_DIM, MOBA_BLOCK)
    return pl.pallas_call(
        _moba_kernel,
        grid=(n_pairs, n_steps),
        in_specs=[pl.BlockSpec((1, HEADS_PER_STEP, LANES), lambda p, i: (p, 0, 0)),
                  pl.BlockSpec(q_block, lambda p, i: (i, p, 0)),
                  pl.BlockSpec(q_block, lambda p, i: (jnp.minimum(i + 1, n_steps - 1), p, 0)),
                  pl.BlockSpec((N_BLOCK_COLS, HEADS_PER_STEP * AUG), lambda p, i: (0, p)),
                  pl.BlockSpec((N_BLOCK_COLS, HEADS_PER_STEP * AUG), lambda p, i: (0, p)),
                  pl.BlockSpec((s, HEADS_PER_STEP * AUG), lambda p, i: (0, p)),
                  pl.BlockSpec((nb, HEADS_PER_STEP * V_ROWS, MOBA_BLOCK), lambda p, i: (0, p, 0))],
        out_specs=pl.BlockSpec((tq, HEADS_PER_STEP * ATTN_HEAD_DIM), lambda p, i: (i, p)),
        out_shape=jax.ShapeDtypeStruct((s, ATTN_WIDTH), BF16),
        scratch_shapes=[pltpu.VMEM((3, HEADS_PER_STEP, KEY_GROUP * MOBA_BLOCK, tq), F32),
                        pltpu.VMEM((HEADS_PER_STEP, AUG, tq), BF16),
                        pltpu.VMEM((HEADS_PER_STEP, 8, tq), F32),
                        pltpu.VMEM((8, LANES), jnp.int32)],
        compiler_params=pltpu.CompilerParams(dimension_semantics=("arbitrary", "arbitrary"),
                                             vmem_limit_bytes=VMEM_LIMIT),
        name="moba",
    )(slope_tab, qt, qt, kmean, knorm2, k_aug, vt_aug)


def _mlstm_kernel(qk_ref, vot_ref, gates_ref, conv_ref, gbias_ref, ngt_ref, y_ref,
                  xbuf, ct_ref, n_ref, m_ref):
    L = MLSTM_CHUNK
    d = MLSTM_HEAD_DIM
    rows = qk_ref.shape[0]
    halo = xbuf.shape[0] - rows
    chunks = range(rows // L)
    heads = range(MLSTM_HEADS)

    @pl.when(pl.program_id(0) == 0)
    def _():
        xbuf[0:halo, :] = jnp.zeros((halo, xbuf.shape[1]), F32)
        ct_ref[...] = jnp.zeros(ct_ref.shape, F32)
        n_ref[...] = jnp.zeros(n_ref.shape, F32)
        m_ref[...] = jnp.zeros(m_ref.shape, F32)

    xbuf[halo:, :] = qk_ref[...]
    w = conv_ref[...]
    y = qk_ref[...] * w[CONV_WIDTH - 1:CONV_WIDTH, :]
    for j in range(CONV_WIDTH - 1):
        off = halo - CONV_WIDTH + 1 + j
        y = y + xbuf[off:off + rows, :] * w[j:j + 1, :]
    xbuf[0:halo, :] = xbuf[rows:rows + halo, :]
    qk = y * _sigmoid(y)

    a = gates_ref[...] + gbias_ref[...]
    lane = lax.broadcasted_iota(jnp.int32, (rows, LANES), 1)
    is_f = jnp.logical_and(lane >= MLSTM_HEADS, lane < 2 * MLSTM_HEADS)
    log_f = jnp.where(is_f, jnp.minimum(a, 0.0) - jnp.log(1.0 + jnp.exp(-jnp.abs(a))), 0.0)
    t_io = lax.broadcasted_iota(jnp.int32, (rows, rows), 0)
    s_io = lax.broadcasted_iota(jnp.int32, (rows, rows), 1)
    same_chunk = (t_io // L) == (s_io // L)
    tri = jnp.where(jnp.logical_and(s_io <= t_io, same_chunk), 1.0, 0.0)
    bcum = jnp.dot(tri, log_f, precision=lax.Precision.HIGHEST,
                   preferred_element_type=F32)
    colm = jnp.where(lane < MLSTM_HEADS, a, bcum)
    rowm = colm.T
    causal_t = (lax.broadcasted_iota(jnp.int32, (L, L), 0) <= lax.broadcasted_iota(jnp.int32, (L, L), 1))

    yield
    pairs = [(cc, hd) for cc in chunks for hd in heads]
    rs = lambda cc: slice(cc * L, (cc + 1) * L)
    fs = lambda hd: slice(hd * d, (hd + 1) * d)
    og_t = {(cc, hd): vot_ref[0, MLSTM_WIDTH + hd * d:MLSTM_WIDTH + (hd + 1) * d, rs(cc)] for cc, hd in pairs}
    qtb = {(cc, hd): qk[rs(cc), fs(hd)].T.astype(BF16) for cc, hd in pairs}
    kb = {(cc, hd): (qk[rs(cc), MLSTM_WIDTH + hd * d:MLSTM_WIDTH + (hd + 1) * d] * (d ** -0.5)).astype(BF16)
          for cc, hd in pairs}
    vt = {(cc, hd): vot_ref[0, fs(hd), rs(cc)] for cc, hd in pairs}
    ic_r = {(cc, hd): rowm[hd:hd + 1, rs(cc)] for cc, hd in pairs}
    bc_r = {(cc, hd): rowm[MLSTM_HEADS + hd:MLSTM_HEADS + hd + 1, rs(cc)] for cc, hd in pairs}
    e_c = {(cc, hd): colm[rs(cc), hd:hd + 1] - colm[rs(cc), MLSTM_HEADS + hd:MLSTM_HEADS + hd + 1]
           for cc, hd in pairs}
    dmat_t = {p: jnp.where(causal_t, bc_r[p] + e_c[p], -jnp.inf) for p in pairs}
    a_row = {p: jnp.max(dmat_t[p], axis=0, keepdims=True) for p in pairs}
    sc_t = {p: jnp.dot(kb[p], qtb[p], preferred_element_type=F32) * jnp.exp(dmat_t[p] - a_row[p]) for p in pairs}
    sv_t = {p: jnp.dot(vt[p].astype(BF16), sc_t[p].astype(BF16), preferred_element_type=F32) for p in pairs}
    ssum = {p: jnp.sum(sc_t[p], axis=0, keepdims=True) for p in pairs}
    b_last = {p: bc_r[p][:, L - 1:L] for p in pairs}
    g_r = {p: b_last[p] - bc_r[p] + ic_r[p] for p in pairs}
    gmax = {p: jnp.max(g_r[p], axis=1, keepdims=True) for p in pairs}
    w_row = {p: jnp.exp(g_r[p] - gmax[p]) for p in pairs}
    upd_t = {p: jnp.dot((vt[p] * w_row[p]).astype(BF16), kb[p], preferred_element_type=F32) for p in pairs}
    ksum = {p: jnp.dot(jnp.broadcast_to(w_row[p], (8, L)).astype(BF16), kb[p],
                       preferred_element_type=F32)[0:1, :] for p in pairs}
    yield
    h_t = {}
    for hd in heads:
        ct_st, n_st, m_prev = ct_ref[hd], n_ref[hd], m_ref[hd:hd + 1, 0:1]
        for cc in chunks:
            p = (cc, hd)
            inter = bc_r[p] + m_prev
            m_row = jnp.maximum(inter, a_row[p])
            r_row = jnp.exp(a_row[p] - m_row)
            w_inter = jnp.exp(inter - m_row)
            qc_t = jnp.dot(ct_st.astype(BF16), qtb[p], preferred_element_type=F32)
            qn = jnp.dot(jnp.broadcast_to(n_st, (8, d)).astype(BF16), qtb[p],
                         preferred_element_type=F32)[0:1, :]
            den = w_inter * qn + r_row * ssum[p]
            scale = 1.0 / jnp.maximum(jnp.abs(den), jnp.exp(-m_row))
            h_t[p] = (w_inter * scale) * qc_t + (r_row * scale) * sv_t[p]
            m_new = jnp.maximum(b_last[p] + m_prev, gmax[p])
            decay = jnp.exp(b_last[p] + m_prev - m_new)
            gain = jnp.exp(gmax[p] - m_new)
            ct_st = decay * ct_st + gain * upd_t[p]
            n_st = decay * n_st + gain * ksum[p]
            m_prev = m_new
        ct_ref[hd], n_ref[hd] = ct_st, n_st
        m_ref[hd:hd + 1, :] = jnp.broadcast_to(m_prev, (1, LANES))
    yield
    for cc, hd in pairs:
        hg = h_t[(cc, hd)] * _sigmoid(og_t[(cc, hd)])
        mu = jnp.mean(hg, axis=0, keepdims=True)
        var = jnp.mean(jnp.square(hg - mu), axis=0, keepdims=True)
        yn = (hg - mu) * lax.rsqrt(var + NORM_EPS) * ngt_ref[fs(hd), :]
        y_ref[rs(cc), fs(hd)] = yn.T.astype(y_ref.dtype)


def _mixer_in_kernel(x_ref, g_ref, wq_ref, wk_ref, wv_ref, wm_ref, wvo_ref, wif_ref, wg_ref, alibi_ref,
                     conv_ref, gbias_ref, ngt_ref,
                     q_ref, k_ref, v_ref, kmean_ref, gg_ref, y_ref,
                     qk_s, vot_s, gates_s, xbuf, ct_ref, n_ref, m_ref):
    t = pl.program_id(0)
    last = pl.num_programs(0) - 2

    @pl.when(t == 0)
    def _():
        qk_s[...] = jnp.zeros(qk_s.shape, F32)
        vot_s[...] = jnp.zeros(vot_s.shape, F32)
        gates_s[...] = jnp.zeros(gates_s.shape, F32)

    mlstm = _mlstm_kernel(qk_s, vot_s, gates_s, conv_ref, gbias_ref, ngt_ref, y_ref, xbuf, ct_ref, n_ref, m_ref)
    proj = _inproj_kernel(x_ref, g_ref, wq_ref, wk_ref, wv_ref, wm_ref, wvo_ref, wif_ref, wg_ref, alibi_ref,
                          q_ref, k_ref, v_ref, kmean_ref, qk_s, vot_s, gates_s, gg_ref, blk=jnp.minimum(t, last))
    for _ in range(3):
        next(mlstm)
        next(proj, None)
    for _ in mlstm:
        pass

    @pl.when(t == 0)
    def _():
        ct_ref[...] = jnp.zeros(ct_ref.shape, F32)
        n_ref[...] = jnp.zeros(n_ref.shape, F32)
        m_ref[...] = jnp.zeros(m_ref.shape, F32)


def _mixer_in(x, g, wq, wk, wv, wm, wvo, wif, wg, alibi, conv, gbias, ngt):
    s = x.shape[0]
    nb = s // MOBA_BLOCK
    cur = lambda t: jnp.minimum(t, nb - 1)
    row = lambda w: pl.BlockSpec((MOBA_BLOCK, w), lambda t: (cur(t), 0))
    tile3 = lambda r: pl.BlockSpec((1, r, MOBA_BLOCK), lambda t: (cur(t), 0, 0))
    whole = pl.BlockSpec(memory_space=pltpu.VMEM)
    return pl.pallas_call(
        _mixer_in_kernel,
        grid=(nb + 1,),
        in_specs=[row(D_MODEL)] + [whole] * 12,
        out_specs=[tile3(ATTN_WIDTH), row(ATTN_AUG_WIDTH), tile3(ATTN_HEADS * V_ROWS),
                   pl.BlockSpec((1, 2, ATTN_AUG_WIDTH), lambda t: (cur(t), 0, 0)),
                   row(2 * D_MODEL),
                   pl.BlockSpec((MOBA_BLOCK, MLSTM_WIDTH), lambda t: (jnp.maximum(t - 1, 0), 0))],
        out_shape=[jax.ShapeDtypeStruct((nb, ATTN_WIDTH, MOBA_BLOCK), F32),
                   jax.ShapeDtypeStruct((s, ATTN_AUG_WIDTH), BF16),
                   jax.ShapeDtypeStruct((nb, ATTN_HEADS * V_ROWS, MOBA_BLOCK), BF16),
                   jax.ShapeDtypeStruct((nb, 2, ATTN_AUG_WIDTH), F32),
                   jax.ShapeDtypeStruct((s, 2 * D_MODEL), F32),
                   jax.ShapeDtypeStruct((s, MLSTM_WIDTH), BF16)],
        scratch_shapes=[pltpu.VMEM((MOBA_BLOCK, 2 * MLSTM_WIDTH), F32),
                        pltpu.VMEM((1, 2 * MLSTM_WIDTH, MOBA_BLOCK), F32),
                        pltpu.VMEM((MOBA_BLOCK, LANES), F32),
                        pltpu.VMEM((MOBA_BLOCK + CONV_HALO, 2 * MLSTM_WIDTH), F32),
                        pltpu.VMEM((MLSTM_HEADS, MLSTM_HEAD_DIM, MLSTM_HEAD_DIM), F32),
                        pltpu.VMEM((MLSTM_HEADS, 1, MLSTM_HEAD_DIM), F32),
                        pltpu.VMEM((8, LANES), F32)],
        compiler_params=pltpu.CompilerParams(dimension_semantics=("arbitrary",),
                                             vmem_limit_bytes=VMEM_LIMIT),
        name="mixer_in",
    )(x, g, wq, wk, wv, wm, wvo, wif, wg, alibi, conv, gbias, ngt)


def _out_ffn_kernel(x_ref, ya_ref, ym_ref, gg_ref, woa_ref, wom_ref, wo_ref, fg_ref,
                    wgt_ref, wup_ref, wdn_ref, fin_ref, o_ref, *, final_norm):
    a = jnp.dot(ya_ref[...], woa_ref[...], preferred_element_type=F32)
    b = jnp.dot(ym_ref[...], wom_ref[...], preferred_element_type=F32)
    merged = _sigmoid(gg_ref[:, :D_MODEL]) * a + _sigmoid(gg_ref[:, D_MODEL:]) * b
    x1 = x_ref[...] + jnp.dot(merged.astype(BF16), wo_ref[...], preferred_element_type=F32)
    h2 = _rms(x1, fg_ref[...]).astype(BF16)
    acc = jnp.zeros(x1.shape, F32)
    for c in range(0, FFN_HIDDEN, FFN_CHUNK):
        gt = jnp.dot(h2, wgt_ref[:, c:c + FFN_CHUNK], preferred_element_type=F32)
        up = jnp.dot(h2, wup_ref[:, c:c + FFN_CHUNK], preferred_element_type=F32)
        act = (gt * _sigmoid(gt) * up).astype(BF16)
        acc = acc + jnp.dot(act, wdn_ref[c:c + FFN_CHUNK, :], preferred_element_type=F32)
    x2 = x1 + acc
    o_ref[...] = _rms(x2, fin_ref[...]) if final_norm else x2


def _out_ffn(x, ya, ym, gg, woa, wom, wo, fg, wgt, wup, wdn, fin, final_norm):
    s = x.shape[0]
    row = lambda w: pl.BlockSpec((OUT_TILE, w), lambda i: (i, 0))
    whole = pl.BlockSpec(memory_space=pltpu.VMEM)
    return pl.pallas_call(
        functools.partial(_out_ffn_kernel, final_norm=final_norm),
        grid=(s // OUT_TILE,),
        in_specs=[row(D_MODEL), row(ATTN_WIDTH), row(MLSTM_WIDTH), row(2 * D_MODEL)] + [whole] * 8,
        out_specs=row(D_MODEL),
        out_shape=jax.ShapeDtypeStruct((s, D_MODEL), F32),
        compiler_params=pltpu.CompilerParams(dimension_semantics=("arbitrary",),
                                             vmem_limit_bytes=VMEM_LIMIT),
        name="out_ffn",
    )(x, ya, ym, gg, woa, wom, wo, fg, wgt, wup, wdn, fin)


def _alibi_slopes():
    return np.exp2(-8.0 * np.arange(1, ATTN_HEADS + 1, dtype=np.float64) / ATTN_HEADS).astype(np.float32)


def _pad_heads(w, width):
    d = w.shape[0]
    w = w.reshape(d, ATTN_HEADS, ATTN_HEAD_DIM)
    w = jnp.pad(w, ((0, 0), (0, 0), (0, width - ATTN_HEAD_DIM)))
    return w.reshape(d, ATTN_HEADS * width)


def kernel(x, mix_norm_g, w_in, conv_qk, b_igate, b_fgate, mlstm_norm_g, w_out_attn, w_out_mlstm,
           w_o, ffn_norm_g, w_ffn_gate, w_ffn_up, w_ffn_down, final_norm_g):
    batch, seq, _ = x.shape
    depth = w_in.shape[0]
    assert seq % OUT_TILE == 0 and seq // MOBA_BLOCK <= N_BLOCK_COLS
    assert seq % (KEY_GROUP * MOBA_BLOCK) == 0
    slopes = _alibi_slopes()
    alibi = np.zeros((1, ATTN_AUG_WIDTH), np.float32)
    alibi[0, AUG - 1::AUG] = slopes
    alibi = jnp.asarray(alibi)
    slope_tab = jnp.asarray(np.broadcast_to(
        slopes.reshape(ATTN_HEADS // HEADS_PER_STEP, HEADS_PER_STEP, 1),
        (ATTN_HEADS // HEADS_PER_STEP, HEADS_PER_STEP, LANES)).copy())
    a0, a1, a2 = ATTN_WIDTH, 2 * ATTN_WIDTH, 3 * ATTN_WIDTH
    m1 = a2 + 4 * MLSTM_WIDTH
    g0 = m1 + 2 * MLSTM_HEADS

    outs = []
    for bi in range(batch):
        xb = x[bi]
        for layer in range(depth):
            w = w_in[layer]
            wq = w[:, :a0].T.astype(BF16)
            wk = w[:, a0:a1].astype(BF16)
            wv = _pad_heads(w[:, a1:a2], V_ROWS).T.astype(BF16)
            wm = w[:, a2:a2 + 2 * MLSTM_WIDTH].astype(BF16)
            wvo = w[:, a2 + 2 * MLSTM_WIDTH:m1].T.astype(BF16)
            wif = jnp.pad(w[:, m1:g0], ((0, 0), (0, LANES - 2 * MLSTM_HEADS))).astype(BF16)
            wg = w[:, g0:].astype(BF16)
            gbias = jnp.pad(jnp.concatenate([b_igate[layer], b_fgate[layer]])[None, :],
                            ((0, 0), (0, LANES - 2 * MLSTM_HEADS))).astype(F32)
            ngt = jnp.broadcast_to(mlstm_norm_g[layer][:, None], (MLSTM_WIDTH, MLSTM_CHUNK)).astype(F32)
            q, k_aug, v_aug, kstat, gg, ym = _mixer_in(
                xb, mix_norm_g[layer][None, :], wq, wk, wv, wm, wvo, wif, wg, alibi,
                conv_qk[layer], gbias, ngt)
            pad_blocks = ((0, N_BLOCK_COLS - kstat.shape[0]), (0, 0))
            ya = _moba(slope_tab, q, jnp.pad(kstat[:, 0, :], pad_blocks), jnp.pad(kstat[:, 1, :], pad_blocks),
                       k_aug, v_aug)
            xb = _out_ffn(xb, ya, ym, gg,
                          w_out_attn[layer].astype(BF16), w_out_mlstm[layer].astype(BF16),
                          w_o[layer].astype(BF16), ffn_norm_g[layer][None, :],
                          w_ffn_gate[layer].astype(BF16), w_ffn_up[layer].astype(BF16),
                          w_ffn_down[layer].astype(BF16), final_norm_g[None, :],
                          final_norm=(layer == depth - 1))
        outs.append(xb)
    return outs[0][None] if batch == 1 else jnp.stack(outs, axis=0)
```
